```python
import jax, jax.numpy as jnp
from jax import lax
import numpy as np

D_MODEL = 1024
BATCH = 8
SEQ = 4096
DEPTH = 1
DEC_BATCH = 128
DEC_SEQ = 1
PAST_LEN = 8192
PAGE_SIZE = 128

HEAD_DIM = 64
N_MEM_HEADS = 4
D_MEM = N_MEM_HEADS * HEAD_DIM
D_ATT = D_MODEL // 2
N_ATT_HEADS = D_ATT // HEAD_DIM
N_KV_HEADS = 2
D_KV = N_KV_HEADS * HEAD_DIM
D_CONV = D_MODEL - D_ATT - D_MEM
N_CONV_GROUPS = D_CONV // HEAD_DIM
D_MIX = D_ATT + D_CONV + D_MEM
N_IDX_HEADS = 8
IDX_DIM = 64
TOPK_MAX = 256
CONV_WIDTH = 3
N_MEM = 256
ROPE_THETA = 10000.0
RMS_EPS = 1e-6
Q_BLOCK = 128

PROJ_SPLITS = (
    ("q", D_ATT), ("k", D_KV), ("v", D_KV), ("g_att", D_ATT),
    ("q_idx", N_IDX_HEADS * IDX_DIM), ("k_idx", IDX_DIM), ("w_idx", N_IDX_HEADS),
    ("u", D_CONV), ("b_conv", D_CONV), ("c_conv", D_CONV), ("g_conv", D_CONV),
    ("q_mem", D_MEM), ("g_mem", D_MEM),
)
D_IN_PROJ = sum(w for _, w in PROJ_SPLITS)

kernel_name = "hybrid_dsa_shortconv_memory_step"


def rms_norm(x, g):
    xf = x.astype(jnp.float32)
    y = xf * lax.rsqrt(jnp.mean(xf * xf, axis=-1, keepdims=True) + RMS_EPS)
    return (y * g.astype(jnp.float32)).astype(x.dtype)


def rope(x, pos):
    half = x.shape[-1] // 2
    inv = ROPE_THETA ** (-jnp.arange(half, dtype=jnp.float32) / half)
    ang = pos.astype(jnp.float32)[:, None] * inv[None, :]
    cos = jnp.cos(ang)[:, None, :]
    sin = jnp.sin(ang)[:, None, :]
    xf = x.astype(jnp.float32)
    x1, x2 = xf[..., :half], xf[..., half:]
    return jnp.concatenate([x1 * cos - x2 * sin, x2 * cos + x1 * sin], axis=-1).astype(x.dtype)


def split_proj(zz):
    out = {}
    off = 0
    for name, w in PROJ_SPLITS:
        out[name] = zz[..., off:off + w]
        off += w
    return out


def front(x, pos, g_pre, w_in):
    B, S, _ = x.shape
    z = split_proj(rms_norm(x, g_pre) @ w_in)
    q = rope(z["q"].reshape(B, S, N_ATT_HEADS, HEAD_DIM), pos)
    k = rope(z["k"].reshape(B, S, N_KV_HEADS, HEAD_DIM), pos)
    v = z["v"].reshape(B, S, N_KV_HEADS, HEAD_DIM)
    q_idx = rope(z["q_idx"].reshape(B, S, N_IDX_HEADS, IDX_DIM), pos)
    k_idx = rope(z["k_idx"].reshape(B, S, 1, IDX_DIM), pos)[:, :, 0]
    return z, q, k, v, q_idx, k_idx, z["w_idx"]


def indexer_scores(q_idx, w_idx, k_idx, valid):
    dots = jnp.einsum("bqhd,bld->bqhl", q_idx, k_idx, preferred_element_type=jnp.float32) * (IDX_DIM ** -0.5)
    s = jnp.einsum("bqh,bqhl->bql", w_idx.astype(jnp.float32) * (N_IDX_HEADS ** -0.5), jax.nn.relu(dots))
    return jnp.where(valid, s, -jnp.inf)


def attend_selected(q, k_sel, v_sel, valid):
    B, Q, H, D = q.shape
    qg = q.reshape(B, Q, N_KV_HEADS, H // N_KV_HEADS, D)
    s = jnp.einsum("bqhgd,bqnhd->bqhgn", qg, k_sel, preferred_element_type=jnp.float32) * (D ** -0.5)
    s = jnp.where(valid[:, :, None, None, :], s, -jnp.inf)
    p = jax.nn.softmax(s, axis=-1)
    o = jnp.einsum("bqhgn,bqnhd->bqhgd", p.astype(v_sel.dtype), v_sel)
    return o.reshape(B, Q, H * D)


def mem_kv(mem, g_memnorm, w_mem_kv):
    B, M, _ = mem.shape
    kv = rms_norm(mem, g_memnorm) @ w_mem_kv
    mk = kv[..., :D_MEM].reshape(B, M, N_MEM_HEADS, HEAD_DIM)
    mv = kv[..., D_MEM:].reshape(B, M, N_MEM_HEADS, HEAD_DIM)
    return mk, mv


def mem_attend(q_mem, mk, mv):
    B, S, _ = q_mem.shape
    q = q_mem.reshape(B, S, N_MEM_HEADS, HEAD_DIM)
    s = jnp.einsum("bqhd,bmhd->bhqm", q, mk, preferred_element_type=jnp.float32) * (HEAD_DIM ** -0.5)
    p = jax.nn.softmax(s, axis=-1)
    return jnp.einsum("bhqm,bmhd->bqhd", p.astype(mv.dtype), mv).reshape(B, S, D_MEM)


def short_conv(u_ext, conv_w):
    T = u_ext.shape[1] - (CONV_WIDTH - 1)
    out = conv_w[0] * u_ext[:, 0:T]
    for j in range(1, CONV_WIDTH):
        out = out + conv_w[j] * u_ext[:, j:j + T]
    return out


def merge(x, z, att, conv, mo, w_out, g_post):
    o = jnp.concatenate([att * jax.nn.silu(z["g_att"]),
                         conv * jax.nn.silu(z["g_conv"]),
                         mo * jax.nn.silu(z["g_mem"])], axis=-1) @ w_out
    return x + rms_norm(o, g_post)


def layer_prompt(x, mem, g_pre, w_in, conv_w, g_memnorm, w_mem_kv, w_out, g_post):
    B, S, _ = x.shape
    pos = jnp.arange(S, dtype=jnp.int32)
    z, q, k, v, q_idx, k_idx, w_idx = front(x, pos, g_pre, w_in)
    n_sel = min(TOPK_MAX, S // 4)
    nblk = S // Q_BLOCK

    def blocks(a):
        return a.reshape((B, nblk, Q_BLOCK) + a.shape[2:]).swapaxes(0, 1)

    def one_block(args):
        qb, qib, wb, tb = args
        valid = (pos[None, :] <= tb[:, None])[None]
        scores = indexer_scores(qib, wb, k_idx, valid)
        _, sel = lax.top_k(scores, n_sel)
        k_sel = jax.vmap(lambda kk, ii: kk[ii])(k, sel)
        v_sel = jax.vmap(lambda vv, ii: vv[ii])(v, sel)
        return attend_selected(qb, k_sel, v_sel, sel <= tb[None, :, None])

    att = lax.map(one_block, (blocks(q), blocks(q_idx), blocks(w_idx), pos.reshape(nblk, Q_BLOCK)))
    att = att.swapaxes(0, 1).reshape(B, S, D_ATT)

    u = z["c_conv"] * z["u"]
    u_ext = jnp.concatenate([jnp.zeros((B, CONV_WIDTH - 1, D_CONV), u.dtype), u], axis=1)
    conv = z["b_conv"] * short_conv(u_ext, conv_w)
    conv_state = u_ext[:, -(CONV_WIDTH - 1):]

    mk, mv = mem_kv(mem, g_memnorm, w_mem_kv)
    mo = mem_attend(z["q_mem"], mk, mv)

    y = merge(x, z, att, conv, mo, w_out, g_post)
    return y, k, v, k_idx, conv_state, mk, mv


def layer_sample(x, cache_k, cache_v, cache_kidx, mem_k, mem_v, conv_state, page_table,
                 g_pre, w_in, conv_w, w_out, g_post):
    B, T, _ = x.shape
    n_pages = page_table.shape[1]
    page = cache_k.shape[1]
    past = n_pages * page
    L = past + T
    pos = past + jnp.arange(T, dtype=jnp.int32)
    z, q, k, v, q_idx, k_idx, w_idx = front(x, pos, g_pre, w_in)

    kidx_past = cache_kidx[page_table].reshape(B, past, IDX_DIM)
    kidx_all = jnp.concatenate([kidx_past, k_idx], axis=1)
    key_pos = jnp.arange(L, dtype=jnp.int32)
    valid = (key_pos[None, :] <= pos[:, None])[None]
    scores = indexer_scores(q_idx, w_idx, kidx_all, valid)
    n_sel = min(TOPK_MAX, L // 4)
    _, sel = lax.top_k(scores, n_sel)

    in_past = (sel < past)[..., None, None]
    pg = jnp.clip(sel // page, 0, n_pages - 1)
    phys_page = jax.vmap(lambda pt, g: pt[g])(page_table, pg)
    row = phys_page * page + sel % page
    k_flat = cache_k.reshape(-1, N_KV_HEADS, HEAD_DIM)
    v_flat = cache_v.reshape(-1, N_KV_HEADS, HEAD_DIM)
    j_new = jnp.clip(sel - past, 0, T - 1)
    k_sel = jnp.where(in_past, k_flat[row], jax.vmap(lambda kk, jj: kk[jj])(k, j_new))
    v_sel = jnp.where(in_past, v_flat[row], jax.vmap(lambda vv, jj: vv[jj])(v, j_new))
    att = attend_selected(q, k_sel, v_sel, sel <= pos[None, :, None])

    u = z["c_conv"] * z["u"]
    u_ext = jnp.concatenate([conv_state.astype(u.dtype), u], axis=1)
    conv = z["b_conv"] * short_conv(u_ext, conv_w)
    new_conv_state = u_ext[:, -(CONV_WIDTH - 1):]

    mo = mem_attend(z["q_mem"], mem_k, mem_v)

    y = merge(x, z, att, conv, mo, w_out, g_post)
    return y, k, v, k_idx, new_conv_state


def setup_inputs(seed: int = 0) -> dict:
    key = jax.random.key(seed)
    ks = jax.random.split(key, 20)
    n_pages = PAST_LEN // PAGE_SIZE
    n_used = DEC_BATCH * n_pages
    n_pool = n_used + max(1, n_used // 4)
    f32 = jnp.float32
    nrm = lambda k, shape, s=1.0: (jax.random.normal(k, shape, f32) * s)
    page_table = jax.random.permutation(ks[10], n_pool)[:n_used].reshape(DEC_BATCH, n_pages).astype(jnp.int32)
    return {
        "x_prompt": nrm(ks[0], (BATCH, SEQ, D_MODEL)),
        "x_sample": nrm(ks[1], (DEC_BATCH, DEC_SEQ, D_MODEL)),
        "mem_prompt": nrm(ks[2], (BATCH, N_MEM, D_MODEL)),
        "cache_k": nrm(ks[3], (DEPTH, n_pool, PAGE_SIZE, N_KV_HEADS, HEAD_DIM)),
        "cache_v": nrm(ks[4], (DEPTH, n_pool, PAGE_SIZE, N_KV_HEADS, HEAD_DIM)),
        "cache_kidx": nrm(ks[5], (DEPTH, n_pool, PAGE_SIZE, IDX_DIM)),
        "cache_mem_k": nrm(ks[6], (DEPTH, DEC_BATCH, N_MEM, N_MEM_HEADS, HEAD_DIM)),
        "cache_mem_v": nrm(ks[7], (DEPTH, DEC_BATCH, N_MEM, N_MEM_HEADS, HEAD_DIM)),
        "state_conv": nrm(ks[8], (DEPTH, DEC_BATCH, CONV_WIDTH - 1, D_CONV)),
        "page_table": page_table,
        "g_pre": 1.0 + nrm(ks[11], (DEPTH, D_MODEL), 0.1),
        "w_in": nrm(ks[12], (DEPTH, D_MODEL, D_IN_PROJ), D_MODEL ** -0.5),
        "conv_w": nrm(ks[13], (DEPTH, CONV_WIDTH, D_CONV), CONV_WIDTH ** -0.5),
        "g_memnorm": 1.0 + nrm(ks[14], (DEPTH, D_MODEL), 0.1),
        "w_mem_kv": nrm(ks[15], (DEPTH, D_MODEL, 2 * D_MEM), D_MODEL ** -0.5),
        "w_out": nrm(ks[16], (DEPTH, D_MIX, D_MODEL), D_MIX ** -0.5),
        "g_post": 1.0 + nrm(ks[17], (DEPTH, D_MODEL), 0.1),
    }


def reference(x_prompt, x_sample, mem_prompt, cache_k, cache_v, cache_kidx, cache_mem_k, cache_mem_v,
              state_conv, page_table, g_pre, w_in, conv_w, g_memnorm, w_mem_kv, w_out, g_post):
    xp, xs = x_prompt, x_sample
    kp, vp, ip, cp, mkp, mvp = [], [], [], [], [], []
    ksm, vsm, ism, csm = [], [], [], []
    for l in range(DEPTH):
        xp, k_l, v_l, i_l, c_l, mk_l, mv_l = layer_prompt(
            xp, mem_prompt, g_pre[l], w_in[l], conv_w[l], g_memnorm[l], w_mem_kv[l], w_out[l], g_post[l])
        kp.append(k_l); vp.append(v_l); ip.append(i_l); cp.append(c_l); mkp.append(mk_l); mvp.append(mv_l)
        xs, k_s, v_s, i_s, c_s = layer_sample(
            xs, cache_k[l], cache_v[l], cache_kidx[l], cache_mem_k[l], cache_mem_v[l], state_conv[l],
            page_table, g_pre[l], w_in[l], conv_w[l], w_out[l], g_post[l])
        ksm.append(k_s); vsm.append(v_s); ism.append(i_s); csm.append(c_s)
    return (xp, xs,
            jnp.stack(kp), jnp.stack(vp), jnp.stack(ip), jnp.stack(cp), jnp.stack(mkp), jnp.stack(mvp),
            jnp.stack(ksm), jnp.stack(vsm), jnp.stack(ism), jnp.stack(csm))
```

```python
import functools

import jax
import jax.numpy as jnp
from jax import lax
from jax.experimental import pallas as pl
from jax.experimental.pallas import tpu as pltpu

D_MODEL = 1024
HEAD_DIM = 64
N_MEM_HEADS = 4
D_MEM = N_MEM_HEADS * HEAD_DIM
D_ATT = 512
N_ATT_HEADS = 8
N_KV_HEADS = 2
Q_PER_KV = N_ATT_HEADS // N_KV_HEADS
D_KV = N_KV_HEADS * HEAD_DIM
D_CONV = 256
N_IDX_HEADS = 8
IDX_DIM = 64
D_QIDX = N_IDX_HEADS * IDX_DIM
TOPK_MAX = 256
CONV_WIDTH = 3
ROPE_THETA = 10000.0
RMS_EPS = 1e-6
LANES = 128

P_Q = 0
P_K = P_Q + D_ATT
P_V = P_K + D_KV
P_GATT = P_V + D_KV
P_QIDX = P_GATT + D_ATT
P_KW = P_QIDX + D_QIDX
P_U = P_KW + LANES
P_B = P_U + D_CONV
P_C = P_B + D_CONV
P_GCONV = P_C + D_CONV
P_QMEM = P_GCONV + D_CONV
P_GMEM = P_QMEM + D_MEM
P_END = P_GMEM + D_MEM
RAW_KW_END = 2 * D_ATT + 2 * D_KV + D_QIDX + IDX_DIM + N_IDX_HEADS

QK_SCALE = HEAD_DIM ** -0.5
IDX_SCALE = IDX_DIM ** -0.5
W_SCALE = N_IDX_HEADS ** -0.5

INT_MIN = -2 ** 31
NEG = -1e30
VMEM_LIMIT = 56 * 1024 * 1024

bf16 = jnp.bfloat16
f32 = jnp.float32
NT = (((1,), (1,)), ((), ()))


def _dot(a, b):
    return jnp.dot(a, b, preferred_element_type=f32)


def _dot_nt(a, b):
    return lax.dot_general(a, b, NT, preferred_element_type=f32)


def _rms(x, g):
    return x * lax.rsqrt(jnp.mean(x * x, axis=-1, keepdims=True) + RMS_EPS) * g


def _silu(x):
    return x / (1.0 + jnp.exp(-x))


def _rope(z, cos, sa, sb):
    return z * cos + pltpu.roll(z, 96, 1) * sa + pltpu.roll(z, 32, 1) * sb


def _to_key(s):
    bits = lax.bitcast_convert_type(s, jnp.int32)
    return bits ^ ((bits >> 31) & jnp.int32(0x7FFFFFFF))


def _softmax_parts(s):
    m = jnp.max(s, axis=-1, keepdims=True)
    p = jnp.exp(s - m)
    return p, jnp.sum(p, axis=-1, keepdims=True)


def _select_bias(key_ref, bias_ref, nc, k_eff):
    _, R, TK = key_ref.shape

    def count(cmp, thr):
        thr_b = jnp.broadcast_to(thr, (R, LANES))

        def body(c, acc):
            kc = key_ref[c]
            for j in range(TK // LANES):
                acc = acc + jnp.where(cmp(kc[:, j * LANES:(j + 1) * LANES], thr_b), 1.0, 0.0)
            return acc

        acc = lax.fori_loop(0, nc, body, jnp.zeros((R, LANES), f32))
        return jnp.sum(acc, axis=1, keepdims=True)

    def bit_body(it, u):
        cand = u | jnp.left_shift(jnp.int32(1), 31 - it)
        cnt = count(lambda a, b: a >= b, cand ^ jnp.int32(INT_MIN))
        return jnp.where(cnt >= k_eff, cand, u)

    u = lax.fori_loop(0, 32, bit_body, jnp.zeros((R, 1), jnp.int32))
    thr = u ^ jnp.int32(INT_MIN)
    need = k_eff - count(lambda a, b: a > b, thr)

    r_i = lax.broadcasted_iota(jnp.int32, (TK, TK), 0)
    c_i = lax.broadcasted_iota(jnp.int32, (TK, TK), 1)
    tri = jnp.where(r_i <= c_i, 1.0, 0.0).astype(bf16)

    def bias_body(c, run):
        kc = key_ref[c]
        eq = kc == thr
        eqf = jnp.where(eq, 1.0, 0.0)
        incl = _dot(eqf.astype(bf16), tri)
        sel = (kc > thr) | (eq & (run + incl - eqf < need))
        bias_ref[c] = jnp.where(sel, 0.0, NEG)
        return run + incl[:, TK - 1:TK]

    lax.fori_loop(0, nc, bias_body, jnp.zeros((R, 1), f32))


def _merge(att_g, cvg, mg, x, wo_ref, gp):
    o = (_dot(att_g, wo_ref[0:D_ATT, :])
         + _dot(cvg, wo_ref[D_ATT:D_ATT + D_CONV, :])
         + _dot(mg, wo_ref[D_ATT + D_CONV:D_MODEL, :]))
    return x + _rms(o, gp)


def _memkv_kernel(mem_ref, g_ref, w_ref, mk_ref, mv_ref, mkb_ref, mvb_ref):
    kv = _dot(_rms(mem_ref[0], g_ref[...]).astype(bf16), w_ref[...])
    mk = kv[:, :D_MEM]
    mv = kv[:, D_MEM:]
    mk_ref[0] = mk
    mv_ref[0] = mv
    mkb_ref[0] = mk.astype(bf16)
    mvb_ref[0] = mv.astype(bf16)


def _memkv(mem, g, w):
    B, M, _ = mem.shape
    blk = lambda: pl.BlockSpec((1, M, D_MEM), lambda b: (b, 0, 0))
    return pl.pallas_call(
        _memkv_kernel,
        grid=(B,),
        in_specs=[pl.BlockSpec((1, M, D_MODEL), lambda b: (b, 0, 0)),
                  pl.BlockSpec((1, D_MODEL), lambda b: (0, 0)),
                  pl.BlockSpec((D_MODEL, 2 * D_MEM), lambda b: (0, 0))],
        out_specs=[blk(), blk(), blk(), blk()],
        out_shape=[jax.ShapeDtypeStruct((B, M, D_MEM), f32)] * 2
        + [jax.ShapeDtypeStruct((B, M, D_MEM), bf16)] * 2,
        compiler_params=pltpu.CompilerParams(dimension_semantics=("arbitrary",)),
        name="memkv",
    )(mem, g, w)


def _front_common(x_ref, g_ref, w_ref, cos_ref, sa_ref, sb_ref,
                  q_ref, k_ref, v_ref, kb_ref, vb_ref, ga_ref, qi_ref, ki_ref, kib_ref, wi_ref):
    xb = _rms(x_ref[0], g_ref[...]).astype(bf16)
    cos, sa, sb = cos_ref[...], sa_ref[...], sb_ref[...]
    proj = lambda lo, hi: _dot(xb, w_ref[:, lo:hi])
    rope = lambda z: _rope(z, cos, sa, sb)

    for s in range(D_ATT // LANES):
        lo = s * LANES
        q_ref[0, :, lo:lo + LANES] = (rope(proj(P_Q + lo, P_Q + lo + LANES)) * QK_SCALE).astype(bf16)
        qi_ref[0, :, lo:lo + LANES] = (rope(proj(P_QIDX + lo, P_QIDX + lo + LANES)) * IDX_SCALE).astype(bf16)
    zk = rope(proj(P_K, P_V))
    k_ref[0] = zk
    kb_ref[0] = zk.astype(bf16)
    zv = proj(P_V, P_GATT)
    v_ref[0] = zv
    vb_ref[0] = zv.astype(bf16)
    ga_ref[0] = _silu(proj(P_GATT, P_QIDX))
    zkw = proj(P_KW, P_U)
    zki = rope(zkw)[:, :IDX_DIM]
    ki_ref[0] = zki
    kib_ref[0] = zki.astype(bf16)
    wi_ref[0] = zkw[:, IDX_DIM:IDX_DIM + N_IDX_HEADS] * W_SCALE
    return proj


def _front_prompt_kernel(x_ref, g_ref, w_ref, cos_ref, sa_ref, sb_ref, cw_ref, mkb_ref, mvb_ref,
                         q_ref, k_ref, v_ref, kb_ref, vb_ref, ga_ref, qi_ref, ki_ref, kib_ref, wi_ref,
                         cvg_ref, cst_ref, mg_ref, ubuf):
    TM = x_ref.shape[1]
    proj = _front_common(x_ref, g_ref, w_ref, cos_ref, sa_ref, sb_ref,
                         q_ref, k_ref, v_ref, kb_ref, vb_ref, ga_ref, qi_ref, ki_ref, kib_ref, wi_ref)

    @pl.when(pl.program_id(1) == 0)
    def _():
        ubuf[0:8, :] = jnp.zeros((8, D_CONV), f32)

    u = proj(P_C, P_GCONV) * proj(P_U, P_B)
    ubuf[8:TM + 8, :] = u
    cw = cw_ref[...]
    conv = proj(P_B, P_C) * (cw[0:1] * ubuf[6:TM + 6, :] + cw[1:2] * ubuf[7:TM + 7, :] + cw[2:3] * u)
    cvg_ref[0] = (conv * _silu(proj(P_GCONV, P_QMEM))).astype(bf16)
    tail = u[TM - 2:TM, :]
    ubuf[6:8, :] = tail
    cst_ref[0] = tail

    qm = (proj(P_QMEM, P_GMEM) * QK_SCALE).astype(bf16)
    mkb = mkb_ref[0]
    mvb = mvb_ref[0]
    outs = []
    for h in range(N_MEM_HEADS):
        sl = slice(h * HEAD_DIM, (h + 1) * HEAD_DIM)
        p, l = _softmax_parts(_dot_nt(qm[:, sl], mkb[:, sl]))
        outs.append(_dot(p.astype(bf16), mvb[:, sl]) / l)
    mo = jnp.concatenate(outs, axis=1)
    mg_ref[0] = (mo * _silu(proj(P_GMEM, P_END))).astype(bf16)


def _front_sample_kernel(x_ref, g_ref, w_ref, cos_ref, sa_ref, sb_ref, cw_ref, st0_ref, st1_ref,
                         q_ref, k_ref, v_ref, kb_ref, vb_ref, ga_ref, qi_ref, ki_ref, kib_ref, wi_ref,
                         cvg_ref, u_ref, qm_ref, gm_ref):
    proj = _front_common(x_ref, g_ref, w_ref, cos_ref, sa_ref, sb_ref,
                         q_ref, k_ref, v_ref, kb_ref, vb_ref, ga_ref, qi_ref, ki_ref, kib_ref, wi_ref)
    u = proj(P_C, P_GCONV) * proj(P_U, P_B)
    cw = cw_ref[...]
    conv = proj(P_B, P_C) * (cw[0:1] * st0_ref[...] + cw[1:2] * st1_ref[...] + cw[2:3] * u)
    cvg_ref[0] = (conv * _silu(proj(P_GCONV, P_QMEM))).astype(bf16)
    u_ref[0] = u
    qm_ref[0] = (proj(P_QMEM, P_GMEM) * QK_SCALE).astype(bf16)
    gm_ref[0] = _silu(proj(P_GMEM, P_END))


def _front_out(B, S, TM):
    tok = lambda d: pl.BlockSpec((1, TM, d), lambda b, j: (b, j, 0))
    specs = [tok(D_ATT), tok(D_KV), tok(D_KV), tok(D_KV), tok(D_KV), tok(D_ATT), tok(D_QIDX),
             tok(IDX_DIM), tok(IDX_DIM), tok(N_IDX_HEADS)]
    sds = lambda d, t: jax.ShapeDtypeStruct((B, S, d), t)
    shapes = [sds(D_ATT, bf16), sds(D_KV, f32), sds(D_KV, f32), sds(D_KV, bf16), sds(D_KV, bf16),
              sds(D_ATT, f32), sds(D_QIDX, bf16), sds(IDX_DIM, f32), sds(IDX_DIM, bf16),
              sds(N_IDX_HEADS, f32)]
    return tok, sds, specs, shapes


def _front_in(TM):
    full = lambda r, c: pl.BlockSpec((r, c), lambda b, j: (0, 0))
    rows = lambda: pl.BlockSpec((TM, LANES), lambda b, j: (j, 0))
    return [pl.BlockSpec((1, TM, D_MODEL), lambda b, j: (b, j, 0)), full(1, D_MODEL), full(D_MODEL, P_END),
            rows(), rows(), rows(), full(CONV_WIDTH, D_CONV)]


def _front_prompt(x, g, w, cos, sa, sb, cw, mkb, mvb, TM):
    B, S, _ = x.shape
    M = mkb.shape[1]
    tok, sds, specs, shapes = _front_out(B, S, TM)
    mem = lambda: pl.BlockSpec((1, M, D_MEM), lambda b, j: (b, 0, 0))
    return pl.pallas_call(
        _front_prompt_kernel,
        grid=(B, S // TM),
        in_specs=_front_in(TM) + [mem(), mem()],
        out_specs=specs + [tok(D_CONV), pl.BlockSpec((1, CONV_WIDTH - 1, D_CONV), lambda b, j: (b, 0, 0)),
                           tok(D_MEM)],
        out_shape=shapes + [sds(D_CONV, bf16), jax.ShapeDtypeStruct((B, CONV_WIDTH - 1, D_CONV), f32),
                            sds(D_MEM, bf16)],
        scratch_shapes=[pltpu.VMEM((TM + 8, D_CONV), f32)],
        compiler_params=pltpu.CompilerParams(dimension_semantics=("arbitrary", "arbitrary"),
                                             vmem_limit_bytes=VMEM_LIMIT),
        name="front_prompt",
    )(x, g, w, cos, sa, sb, cw, mkb, mvb)


def _front_sample(x, g, w, cos, sa, sb, cw, st0, st1):
    _, T, _ = x.shape
    tok, sds, specs, shapes = _front_out(1, T, T)
    st = lambda: pl.BlockSpec((T, D_CONV), lambda b, j: (0, 0))
    return pl.pallas_call(
        _front_sample_kernel,
        grid=(1, 1),
        in_specs=_front_in(T) + [st(), st()],
        out_specs=specs + [tok(D_CONV), tok(D_CONV), tok(D_MEM), tok(D_MEM)],
        out_shape=shapes + [sds(D_CONV, bf16), sds(D_CONV, f32), sds(D_MEM, bf16), sds(D_MEM, f32)],
        compiler_params=pltpu.CompilerParams(dimension_semantics=("arbitrary", "arbitrary"),
                                             vmem_limit_bytes=VMEM_LIMIT),
        name="front_sample",
    )(x, g, w, cos, sa, sb, cw, st0, st1)


def _attn_prompt_kernel(q_ref, qi_ref, wi_ref, kb_ref, vb_ref, kib_ref, ga_ref, cvg_ref, mg_ref, x_ref,
                        wo_ref, gp_ref, y_ref, key_ref, bias_ref, att_ref):
    TQ = q_ref.shape[1]
    i = pl.program_id(1)
    chunk = lambda c: pl.ds(pl.multiple_of(c * TQ, TQ), TQ)

    qi = qi_ref[0]
    wi = wi_ref[0]
    qis = [qi[:, h * IDX_DIM:(h + 1) * IDX_DIM] for h in range(N_IDX_HEADS)]
    wcols = [wi[:, h:h + 1] for h in range(N_IDX_HEADS)]

    def score_keys(c):
        kc = kib_ref[0, chunk(c), :]
        acc = jnp.zeros((TQ, TQ), f32)
        for h in range(N_IDX_HEADS):
            acc = acc + jnp.maximum(_dot_nt(qis[h], kc), 0.0) * wcols[h]
        return _to_key(acc)

    def score_body(c, carry):
        key_ref[c] = score_keys(c)
        return carry

    lax.fori_loop(0, i, score_body, 0)
    row = lax.broadcasted_iota(jnp.int32, (TQ, TQ), 0)
    col = lax.broadcasted_iota(jnp.int32, (TQ, TQ), 1)
    key_ref[i] = jnp.where(col <= row, score_keys(i), jnp.int32(INT_MIN))

    t1 = i * TQ + 1 + lax.broadcasted_iota(jnp.int32, (TQ, 1), 0)
    k_eff = jnp.minimum(t1, min(TOPK_MAX, kb_ref.shape[1] // 4)).astype(f32)
    _select_bias(key_ref, bias_ref, i + 1, k_eff)

    q = q_ref[0]
    for h in range(N_ATT_HEADS):
        g = h // Q_PER_KV
        hs = slice(h * HEAD_DIM, (h + 1) * HEAD_DIM)
        gs = slice(g * HEAD_DIM, (g + 1) * HEAD_DIM)
        qh = q[:, hs]

        def att_body(c, carry):
            m, l, acc = carry
            s = _dot_nt(qh, kb_ref[0, chunk(c), gs]) + bias_ref[c]
            m_new = jnp.maximum(m, jnp.max(s, axis=1, keepdims=True))
            alpha = jnp.exp(m - m_new)
            p = jnp.exp(s - m_new)
            l = alpha * l + jnp.sum(p, axis=1, keepdims=True)
            acc = alpha * acc + _dot(p.astype(bf16), vb_ref[0, chunk(c), gs])
            return m_new, l, acc

        init = (jnp.full((TQ, 1), NEG, f32), jnp.zeros((TQ, 1), f32), jnp.zeros((TQ, HEAD_DIM), f32))
        _, l, acc = lax.fori_loop(0, i + 1, att_body, init)
        att_ref[:, hs] = (acc / l) * ga_ref[0, :, hs]

    y_ref[0] = _merge(att_ref[...].astype(bf16), cvg_ref[0], mg_ref[0], x_ref[0], wo_ref, gp_ref[...])


def _attn_prompt(q, qi, wi, kb, vb, kib, ga, cvg, mg, x, wo, gp, TQ):
    B, S, _ = x.shape
    NC = S // TQ
    tok = lambda d: pl.BlockSpec((1, TQ, d), lambda b, i: (b, i, 0))
    seq = lambda d: pl.BlockSpec((1, S, d), lambda b, i: (b, 0, 0))
    full = lambda r, c: pl.BlockSpec((r, c), lambda b, i: (0, 0))
    return pl.pallas_call(
        _attn_prompt_kernel,
        grid=(B, NC),
        in_specs=[tok(D_ATT), tok(D_QIDX), tok(N_IDX_HEADS), seq(D_KV), seq(D_KV), seq(IDX_DIM),
                  tok(D_ATT), tok(D_CONV), tok(D_MEM), tok(D_MODEL), full(D_MODEL, D_MODEL), full(1, D_MODEL)],
        out_specs=tok(D_MODEL),
        out_shape=jax.ShapeDtypeStruct((B, S, D_MODEL), f32),
        scratch_shapes=[pltpu.VMEM((NC, TQ, TQ), jnp.int32), pltpu.VMEM((NC, TQ, TQ), f32),
                        pltpu.VMEM((TQ, D_ATT), f32)],
        compiler_params=pltpu.CompilerParams(dimension_semantics=("arbitrary", "arbitrary"),
                                             vmem_limit_bytes=VMEM_LIMIT),
        name="attn_prompt",
    )(q, qi, wi, kb, vb, kib, ga, cvg, mg, x, wo, gp)


def _paged_copies(pt_ref, b, hbm_ref, buf, sem, slot, n_pages, page):
    return [pltpu.make_async_copy(hbm_ref.at[pt_ref[b, p]], buf.at[slot, pl.ds(p * page, page), :], sem.at[slot])
            for p in range(n_pages)]


def _sidx_kernel(pt_ref, qi_ref, wi_ref, knew_ref, kidx_hbm, sc_ref, dn_ref, buf, sem):
    b = pl.program_id(0)
    nb = pl.num_programs(0)
    n_pages = pt_ref.shape[1]
    page = kidx_hbm.shape[1]
    copies = lambda bb, slot: _paged_copies(pt_ref, bb, kidx_hbm, buf, sem, slot, n_pages, page)
    slot = lax.rem(b, 2)

    @pl.when(b == 0)
    def _():
        for cp in copies(0, 0):
            cp.start()

    @pl.when(b + 1 < nb)
    def _():
        for cp in copies(b + 1, 1 - slot):
            cp.start()

    for cp in copies(b, slot):
        cp.wait()

    qi = qi_ref[0]
    wi = wi_ref[0]
    score = lambda keys: jnp.sum(jnp.maximum(_dot_nt(qi, keys), 0.0) * wi, axis=0, keepdims=True)
    sub = 1024
    for j in range(n_pages * page // sub):
        sc_ref[0, :, j * sub:(j + 1) * sub] = score(buf[slot, j * sub:(j + 1) * sub, :].astype(bf16))
    dn_ref[0] = score(knew_ref[...])


def _sidx(pt, qi8, wi8, knew, kidx_pool):
    nb, n_pages = pt.shape
    page = kidx_pool.shape[1]
    past = n_pages * page
    return pl.pallas_call(
        _sidx_kernel,
        grid_spec=pltpu.PrefetchScalarGridSpec(
            num_scalar_prefetch=1,
            grid=(nb,),
            in_specs=[pl.BlockSpec((1, N_IDX_HEADS, IDX_DIM), lambda b, pt: (b, 0, 0)),
                      pl.BlockSpec((1, N_IDX_HEADS, 1), lambda b, pt: (b, 0, 0)),
                      pl.BlockSpec((nb, IDX_DIM), lambda b, pt: (0, 0)),
                      pl.BlockSpec(memory_space=pl.ANY)],
            out_specs=[pl.BlockSpec((1, 1, past), lambda b, pt: (b, 0, 0)),
                       pl.BlockSpec((1, 1, nb), lambda b, pt: (b, 0, 0))],
            scratch_shapes=[pltpu.VMEM((2, past, IDX_DIM), f32), pltpu.SemaphoreType.DMA((2,))]),
        out_shape=[jax.ShapeDtypeStruct((nb, 1, past), f32), jax.ShapeDtypeStruct((nb, 1, nb), f32)],
        compiler_params=pltpu.CompilerParams(dimension_semantics=("arbitrary",), vmem_limit_bytes=VMEM_LIMIT),
        name="sample_indexer",
    )(pt, qi8, wi8, knew, kidx_pool)


def _ssel_kernel(sc_ref, dn_ref, out_ref, key_ref, bias_ref):
    nc, R, TK = key_ref.shape
    for c in range(nc - 1):
        key_ref[c] = _to_key(sc_ref[:, c * TK:(c + 1) * TK])
    row = lax.broadcasted_iota(jnp.int32, (R, TK), 0)
    col = lax.broadcasted_iota(jnp.int32, (R, TK), 1)
    key_ref[nc - 1] = jnp.where(row == col, _to_key(dn_ref[...]), jnp.int32(INT_MIN))
    n_valid = (nc - 1) * TK + 1
    k_eff = jnp.full((R, 1), float(min(TOPK_MAX, n_valid // 4)), f32)
    _select_bias(key_ref, bias_ref, nc, k_eff)
    for c in range(nc):
        out_ref[:, c * TK:(c + 1) * TK] = bias_ref[c]


def _ssel(sc, dn):
    nb, past = sc.shape
    nc = past // nb + 1
    return pl.pallas_call(
        _ssel_kernel,
        out_shape=jax.ShapeDtypeStruct((nb, past + nb), f32),
        scratch_shapes=[pltpu.VMEM((nc, nb, nb), jnp.int32), pltpu.VMEM((nc, nb, nb), f32)],
        compiler_params=pltpu.CompilerParams(vmem_limit_bytes=VMEM_LIMIT),
        name="sample_select",
    )(sc, dn)


def _sattn_kernel(pt_ref, qbd_ref, bias_ref, knew_ref, vnew_ref, qmbd_ref, cmk_ref, cmv_ref, k_hbm, v_hbm,
                  o_ref, mo_ref, kbuf, vbuf, sem):
    b = pl.program_id(0)
    nb = pl.num_programs(0)
    n_pages = pt_ref.shape[1]
    page = k_hbm.shape[1]
    past = n_pages * page
    slot = lax.rem(b, 2)

    def copies(bb, sl):
        return (_paged_copies(pt_ref, bb, k_hbm, kbuf, sem.at[0], sl, n_pages, page)
                + _paged_copies(pt_ref, bb, v_hbm, vbuf, sem.at[1], sl, n_pages, page))

    @pl.when(b == 0)
    def _():
        for cp in copies(0, 0):
            cp.start()

    @pl.when(b + 1 < nb)
    def _():
        for cp in copies(b + 1, 1 - slot):
            cp.start()

    qm = qmbd_ref[0]
    pm, lm = _softmax_parts(_dot_nt(qm, cmk_ref[0].astype(bf16)))
    mo = _dot(pm.astype(bf16), cmv_ref[0].astype(bf16)) / lm
    for h in range(N_MEM_HEADS):
        hs = slice(h * HEAD_DIM, (h + 1) * HEAD_DIM)
        mo_ref[0, :, hs] = mo[h:h + 1, hs]

    for cp in copies(b, slot):
        cp.wait()

    qbd = qbd_ref[0]
    s_past = _dot_nt(qbd, kbuf[slot].astype(bf16)) + bias_ref[0, :, 0:past]
    s_new = _dot_nt(qbd, knew_ref[...]) + bias_ref[0, :, past:past + nb]
    m = jnp.maximum(jnp.max(s_past, axis=1, keepdims=True), jnp.max(s_new, axis=1, keepdims=True))
    p_past = jnp.exp(s_past - m)
    p_new = jnp.exp(s_new - m)
    l = jnp.sum(p_past, axis=1, keepdims=True) + jnp.sum(p_new, axis=1, keepdims=True)
    o = _dot(p_past.astype(bf16), vbuf[slot].astype(bf16)) + _dot(p_new.astype(bf16), vnew_ref[...])
    o = o / l
    for h in range(N_ATT_HEADS):
        g = h // Q_PER_KV
        o_ref[0, :, h * HEAD_DIM:(h + 1) * HEAD_DIM] = o[h:h + 1, g * HEAD_DIM:(g + 1) * HEAD_DIM]


def _sattn(pt, qbd, bias, knew, vnew, qmbd, cmk, cmv, k_pool, v_pool):
    nb, n_pages = pt.shape
    page = k_pool.shape[1]
    past = n_pages * page
    M = cmk.shape[1]
    per = lambda r, c: pl.BlockSpec((1, r, c), lambda b, pt: (b, 0, 0))
    full = lambda r, c: pl.BlockSpec((r, c), lambda b, pt: (0, 0))
    return pl.pallas_call(
        _sattn_kernel,
        grid_spec=pltpu.PrefetchScalarGridSpec(
            num_scalar_prefetch=1,
            grid=(nb,),
            in_specs=[per(N_ATT_HEADS, D_KV), per(1, past + nb), full(nb, D_KV), full(nb, D_KV),
                      per(N_MEM_HEADS, D_MEM), per(M, D_MEM), per(M, D_MEM),
                      pl.BlockSpec(memory_space=pl.ANY), pl.BlockSpec(memory_space=pl.ANY)],
            out_specs=[per(1, D_ATT), per(1, D_MEM)],
            scratch_shapes=[pltpu.VMEM((2, past, D_KV), f32), pltpu.VMEM((2, past, D_KV), f32),
                            pltpu.SemaphoreType.DMA((2, 2))]),
        out_shape=[jax.ShapeDtypeStruct((nb, 1, D_ATT), f32), jax.ShapeDtypeStruct((nb, 1, D_MEM), f32)],
        compiler_params=pltpu.CompilerParams(dimension_semantics=("arbitrary",), vmem_limit_bytes=VMEM_LIMIT),
        name="sample_attention",
    )(pt, qbd, bias, knew, vnew, qmbd, cmk, cmv, k_pool, v_pool)


def _smerge_kernel(att_ref, ga_ref, cvg_ref, mo_ref, gm_ref, x_ref, wo_ref, gp_ref, y_ref):
    att_g = (att_ref[...] * ga_ref[...]).astype(bf16)
    mg = (mo_ref[...] * gm_ref[...]).astype(bf16)
    y_ref[...] = _merge(att_g, cvg_ref[...], mg, x_ref[...], wo_ref, gp_ref[...])


def _smerge(att, ga, cvg, mo, gm, x, wo, gp):
    return pl.pallas_call(
        _smerge_kernel,
        out_shape=jax.ShapeDtypeStruct(x.shape, f32),
        compiler_params=pltpu.CompilerParams(vmem_limit_bytes=VMEM_LIMIT),
        name="sample_merge",
    )(att, ga, cvg, mo, gm, x, wo, gp)


def _rope_tables(pos):
    half = HEAD_DIM // 2
    inv = ROPE_THETA ** (-jnp.arange(half, dtype=f32) / half)
    ang = pos.astype(f32)[:, None] * inv[None, :]
    cos, sin = jnp.cos(ang), jnp.sin(ang)
    zero = jnp.zeros_like(sin)
    rep = LANES // HEAD_DIM
    return (jnp.tile(jnp.concatenate([cos, cos], axis=1), (1, rep)),
            jnp.tile(jnp.concatenate([-sin, zero], axis=1), (1, rep)),
            jnp.tile(jnp.concatenate([zero, sin], axis=1), (1, rep)))


def _pack_w_in(w):
    pad = jnp.zeros((D_MODEL, LANES - IDX_DIM - N_IDX_HEADS), w.dtype)
    return jnp.concatenate([w[:, :RAW_KW_END], pad, w[:, RAW_KW_END:]], axis=1).astype(bf16)


def _head_blocks(x, n_heads, n_blocks):
    T = x.shape[0]
    xh = x.reshape(T, n_heads, 1, HEAD_DIM)
    blk = (jnp.arange(n_heads) * n_blocks // n_heads)[None, :, None, None]
    keep = blk == jnp.arange(n_blocks)[None, None, :, None]
    return jnp.where(keep, xh, jnp.zeros((), x.dtype)).reshape(T, n_heads, n_blocks * HEAD_DIM)


def _layer(l, xp, xs, mem_prompt, cache_k, cache_v, cache_kidx, cache_mem_k, cache_mem_v, state_conv,
           page_table, g_pre, w_in, conv_w, g_memnorm, w_mem_kv, w_out, g_post):
    B, S, _ = xp.shape
    nb, T, _ = xs.shape
    assert T == 1
    n_pages = page_table.shape[1]
    page = cache_k.shape[2]
    past = n_pages * page
    M = mem_prompt.shape[1]
    assert page == nb and past % 1024 == 0

    row = lambda a: a[l][None, :]
    w_pack = _pack_w_in(w_in[l])
    wo = w_out[l].astype(bf16)
    gp = row(g_post)

    mk, mv, mkb, mvb = _memkv(mem_prompt, row(g_memnorm), w_mem_kv[l].astype(bf16))
    cos, sa, sb = _rope_tables(jnp.arange(S, dtype=jnp.int32))
    TM = min(512, S)
    (q, k, v, kb, vb, ga, qi, ki, kib, wi, cvg, cst, mg) = _front_prompt(
        xp, row(g_pre), w_pack, cos, sa, sb, conv_w[l], mkb, mvb, TM)
    TQ = min(256, S)
    yp = _attn_prompt(q, qi, wi, kb, vb, kib, ga, cvg, mg, xp, wo, gp, TQ)

    cos, sa, sb = _rope_tables(jnp.full((nb,), past, jnp.int32))
    st = state_conv[l]
    (q_s, k_s, v_s, kb_s, vb_s, ga_s, qi_s, ki_s, kib_s, wi_s, cvg_s, u_s, qm_s, gm_s) = _front_sample(
        xs.reshape(1, nb, D_MODEL), row(g_pre), w_pack, cos, sa, sb, conv_w[l], st[:, 0, :], st[:, 1, :])
    sc, dn = _sidx(page_table, qi_s.reshape(nb, N_IDX_HEADS, IDX_DIM), wi_s.reshape(nb, N_IDX_HEADS, 1),
                   kib_s[0], cache_kidx[l])
    bias = _ssel(sc.reshape(nb, past), dn.reshape(nb, nb))
    att_s, mo_s = _sattn(page_table, _head_blocks(q_s[0], N_ATT_HEADS, N_KV_HEADS), bias.reshape(nb, 1, past + nb),
                     kb_s[0], vb_s[0], _head_blocks(qm_s[0], N_MEM_HEADS, N_MEM_HEADS),
                     cache_mem_k[l].reshape(nb, M, D_MEM), cache_mem_v[l].reshape(nb, M, D_MEM),
                     cache_k[l].reshape(-1, page, D_KV), cache_v[l].reshape(-1, page, D_KV))
    ys = _smerge(att_s.reshape(nb, D_ATT), ga_s[0], cvg_s[0], mo_s.reshape(nb, D_MEM), gm_s[0],
                 xs.reshape(nb, D_MODEL), wo, gp)

    outs_p = (k.reshape(B, S, N_KV_HEADS, HEAD_DIM), v.reshape(B, S, N_KV_HEADS, HEAD_DIM), ki, cst,
              mk.reshape(B, M, N_MEM_HEADS, HEAD_DIM), mv.reshape(B, M, N_MEM_HEADS, HEAD_DIM))
    outs_s = (k_s.reshape(nb, 1, N_KV_HEADS, HEAD_DIM), v_s.reshape(nb, 1, N_KV_HEADS, HEAD_DIM),
              ki_s.reshape(nb, 1, IDX_DIM), jnp.stack([st[:, 1, :], u_s[0]], axis=1))
    return yp, ys.reshape(nb, 1, D_MODEL), outs_p, outs_s


def kernel(x_prompt, x_sample, mem_prompt, cache_k, cache_v, cache_kidx, cache_mem_k, cache_mem_v, state_conv,
           page_table, g_pre, w_in, conv_w, g_memnorm, w_mem_kv, w_out, g_post):
    depth = w_in.shape[0]
    xp, xs = x_prompt, x_sample
    per_p, per_s = [], []
    for l in range(depth):
        xp, xs, op, os_ = _layer(l, xp, xs, mem_prompt, cache_k, cache_v, cache_kidx, cache_mem_k, cache_mem_v,
                                 state_conv, page_table, g_pre, w_in, conv_w, g_memnorm, w_mem_kv, w_out, g_post)
        per_p.append(op)
        per_s.append(os_)
    stack = lambda outs, i: jnp.stack([o[i] for o in outs])
    return ((xp, xs) + tuple(stack(per_p, i) for i in range(6)) + tuple(stack(per_s, i) for i in range(4)))
```

```python
import jax
import jax.numpy as jnp
from jax import lax
from jax.experimental import pallas as pl
from jax.experimental.pallas import tpu as pltpu

D_MODEL = 1024
HEAD_DIM = 64
N_MEM_HEADS = 4
D_MEM = N_MEM_HEADS * HEAD_DIM
D_ATT = 512
N_ATT_HEADS = 8
N_KV_HEADS = 2
Q_PER_KV = N_ATT_HEADS // N_KV_HEADS
D_KV = N_KV_HEADS * HEAD_DIM
D_CONV = 256
N_IDX_HEADS = 8
IDX_DIM = 64
D_QIDX = N_IDX_HEADS * IDX_DIM
TOPK_MAX = 256
CONV_WIDTH = 3
ROPE_THETA = 10000.0
RMS_EPS = 1e-6
LANES = 128
SUBLANES = 8

R_Q = 0
R_K = R_Q + D_ATT
R_V = R_K + D_KV
R_GATT = R_V + D_KV
R_QIDX = R_GATT + D_ATT
R_KIDX = R_QIDX + D_QIDX
R_WIDX = R_KIDX + IDX_DIM
R_U = R_WIDX + N_IDX_HEADS

ATT_ORDER = (0, 4, 1, 5, 2, 6, 3, 7)
P_Q = 0
P_K = P_Q + D_ATT
P_V = P_K + D_KV
P_GATT = P_V + D_KV
P_QIDX = P_GATT + D_ATT
P_KK = P_QIDX + D_QIDX
P_U = P_KK + LANES
P_B = P_U + D_CONV
P_C = P_B + D_CONV
P_GCONV = P_C + D_CONV
P_QMEM = P_GCONV + D_CONV
P_GMEM = P_QMEM + D_MEM
P_END = P_GMEM + D_MEM

QK_SCALE = HEAD_DIM ** -0.5
IDX_SCALE = IDX_DIM ** -0.5
W_SCALE = N_IDX_HEADS ** -0.5

INT_MIN = -2 ** 31
NEG = -1e30
VMEM_LIMIT = 56 * 1024 * 1024
ACC_ROWS = 32
D_T = D_KV + 16

bf16 = jnp.bfloat16
f32 = jnp.float32
NT = (((1,), (1,)), ((), ()))


def _dot(a, b):
    return jnp.dot(a, b, preferred_element_type=f32)


def _dot_nt(a, b):
    return lax.dot_general(a, b, NT, preferred_element_type=f32)


def _rms(x, g):
    return x * lax.rsqrt(jnp.mean(x * x, axis=-1, keepdims=True) + RMS_EPS) * g


def _silu(x):
    return x / (1.0 + jnp.exp(-x))


def _rope(z, cos, sa, sb):
    return z * cos + pltpu.roll(z, 96, 1) * sa + pltpu.roll(z, 32, 1) * sb


def _key_to_f32(k):
    return lax.bitcast_convert_type(k ^ ((k >> 31) & jnp.int32(0x7FFFFFFF)), f32)


def _softmax_parts(s):
    m = jnp.max(s, axis=-1, keepdims=True)
    p = jnp.exp(s - m)
    return p, jnp.sum(p, axis=-1, keepdims=True)


def _select_bias_t(sc_ref, bias_ref, nc, k_eff):
    _, TK, Q = sc_ref.shape

    def count(cmp, thr):
        thr_b = jnp.broadcast_to(thr, (ACC_ROWS, Q))

        def body(c, acc):
            for j in range(TK // ACC_ROWS):
                acc = acc + jnp.where(cmp(sc_ref[c, j * ACC_ROWS:(j + 1) * ACC_ROWS, :], thr_b), 1.0, 0.0)
            return acc

        acc = lax.fori_loop(0, nc, body, jnp.zeros((ACC_ROWS, Q), f32))
        return jnp.sum(acc, axis=0, keepdims=True)

    def bit_body(it, u):
        cand = u | jnp.left_shift(jnp.int32(1), 31 - it)
        cnt = count(lambda a, b: a >= b, _key_to_f32(cand ^ jnp.int32(INT_MIN)))
        return jnp.where(cnt >= k_eff, cand, u)

    u = lax.fori_loop(0, 32, bit_body, jnp.zeros((1, Q), jnp.int32))
    thr = _key_to_f32(u ^ jnp.int32(INT_MIN))
    need = k_eff - count(lambda a, b: a > b, thr)

    r_i = lax.broadcasted_iota(jnp.int32, (TK, TK), 0)
    c_i = lax.broadcasted_iota(jnp.int32, (TK, TK), 1)
    tri = jnp.where(c_i <= r_i, 1.0, 0.0).astype(bf16)

    def bias_body(c, run):
        x = sc_ref[c]
        eqf = jnp.where(x == thr, 1.0, 0.0)
        incl = _dot(tri, eqf.astype(bf16))
        tie_ok = jnp.where(run + incl - eqf < need, eqf, 0.0)
        bias_ref[c] = jnp.where(x > thr, 0.0, jnp.where(tie_ok > 0.5, 0.0, NEG))
        return run + incl[TK - 1:TK, :]

    lax.fori_loop(0, nc, bias_body, jnp.zeros((1, Q), f32))


def _merge(att_g, cvg, mg, x, wo_ref, gp):
    o = (_dot(att_g, wo_ref[0:D_ATT, :])
         + _dot(cvg, wo_ref[D_ATT:D_ATT + D_CONV, :])
         + _dot(mg, wo_ref[D_ATT + D_CONV:D_MODEL, :]))
    return x + _rms(o, gp)


def _memkv_kernel(mem_ref, g_ref, w_ref, mk_ref, mv_ref, mkb_ref, mvb_ref):
    kv = _dot(_rms(mem_ref[0], g_ref[...]).astype(bf16), w_ref[...])
    mk = kv[:, :D_MEM]
    mv = kv[:, D_MEM:]
    mk_ref[0] = mk
    mv_ref[0] = mv
    mkb_ref[0] = mk.astype(bf16)
    mvb_ref[0] = mv.astype(bf16)


def _memkv(mem, g, w):
    B, M, _ = mem.shape
    blk = lambda: pl.BlockSpec((1, M, D_MEM), lambda b: (b, 0, 0))
    return pl.pallas_call(
        _memkv_kernel,
        grid=(B,),
        in_specs=[pl.BlockSpec((1, M, D_MODEL), lambda b: (b, 0, 0)),
                  pl.BlockSpec((1, D_MODEL), lambda b: (0, 0)),
                  pl.BlockSpec((D_MODEL, 2 * D_MEM), lambda b: (0, 0))],
        out_specs=[blk(), blk(), blk(), blk()],
        out_shape=[jax.ShapeDtypeStruct((B, M, D_MEM), f32)] * 2
        + [jax.ShapeDtypeStruct((B, M, D_MEM), bf16)] * 2,
        compiler_params=pltpu.CompilerParams(dimension_semantics=("arbitrary",)),
        name="memkv",
    )(mem, g, w)


def _front_common(x_ref, g_ref, w_ref, wt_ref, cos_ref, sa_ref, sb_ref,
                  q_ref, k_ref, v_ref, kb_ref, vt_ref, ga_ref, qi_ref, ki_ref, kk_ref, wit_ref):
    xb = _rms(x_ref[0], g_ref[...]).astype(bf16)
    cos, sa, sb = cos_ref[...], sa_ref[...], sb_ref[...]
    proj = lambda lo, hi: _dot(xb, w_ref[:, lo:hi])
    rope = lambda z: _rope(z, cos, sa, sb)

    for s in range(D_ATT // LANES):
        lo = s * LANES
        q_ref[0, :, lo:lo + LANES] = (rope(proj(P_Q + lo, P_Q + lo + LANES)) * QK_SCALE).astype(bf16)
        qi_ref[0, :, lo:lo + LANES] = (rope(proj(P_QIDX + lo, P_QIDX + lo + LANES)) * IDX_SCALE).astype(bf16)
    zk = rope(proj(P_K, P_V))
    k_ref[0] = zk
    kb_ref[0] = zk.astype(bf16)
    v_ref[0] = proj(P_V, P_GATT)
    ga_ref[0] = _silu(proj(P_GATT, P_QIDX))
    zkk = rope(proj(P_KK, P_U))
    ki_ref[0] = zkk[:, :IDX_DIM]
    kk_ref[0] = zkk.astype(bf16)
    vw_t = _dot_nt(wt_ref[...], xb)
    TK = vt_ref.shape[3]
    for c in range(vt_ref.shape[1]):
        vt_ref[0, c] = vw_t[0:D_KV, c * TK:(c + 1) * TK].astype(bf16)
    wit_ref[0] = vw_t[D_KV:D_KV + N_IDX_HEADS, :] * W_SCALE
    return proj


def _front_prompt_kernel(x_ref, g_ref, w_ref, wt_ref, cos_ref, sa_ref, sb_ref, cw_ref, mkb_ref, mvb_ref,
                         q_ref, k_ref, v_ref, kb_ref, vt_ref, ga_ref, qi_ref, ki_ref, kk_ref, wit_ref,
                         cvg_ref, cst_ref, mg_ref, ubuf):
    TM = x_ref.shape[1]
    proj = _front_common(x_ref, g_ref, w_ref, wt_ref, cos_ref, sa_ref, sb_ref,
                         q_ref, k_ref, v_ref, kb_ref, vt_ref, ga_ref, qi_ref, ki_ref, kk_ref, wit_ref)

    @pl.when(pl.program_id(1) == 0)
    def _():
        ubuf[0:8, :] = jnp.zeros((8, D_CONV), f32)

    u = proj(P_C, P_GCONV) * proj(P_U, P_B)
    ubuf[8:TM + 8, :] = u
    cw = cw_ref[...]
    conv = proj(P_B, P_C) * (cw[0:1] * ubuf[6:TM + 6, :] + cw[1:2] * ubuf[7:TM + 7, :] + cw[2:3] * u)
    cvg_ref[0] = (conv * _silu(proj(P_GCONV, P_QMEM))).astype(bf16)
    tail = u[TM - 2:TM, :]
    ubuf[6:8, :] = tail
    cst_ref[0] = tail

    qm = (proj(P_QMEM, P_GMEM) * QK_SCALE).astype(bf16)
    mkb = mkb_ref[0]
    mvb = mvb_ref[0]
    outs = []
    for h in range(N_MEM_HEADS):
        sl = slice(h * HEAD_DIM, (h + 1) * HEAD_DIM)
        p, l = _softmax_parts(_dot_nt(qm[:, sl], mkb[:, sl]))
        outs.append(_dot(p.astype(bf16), mvb[:, sl]) / l)
    mo = jnp.concatenate(outs, axis=1)
    mg_ref[0] = (mo * _silu(proj(P_GMEM, P_END))).astype(bf16)


def _front_sample_kernel(x_ref, g_ref, w_ref, wt_ref, cos_ref, sa_ref, sb_ref, cw_ref, st0_ref, st1_ref,
                         q_ref, k_ref, v_ref, kb_ref, vt_ref, ga_ref, qi_ref, ki_ref, kk_ref, wit_ref,
                         cvg_ref, u_ref, qm_ref, gm_ref):
    proj = _front_common(x_ref, g_ref, w_ref, wt_ref, cos_ref, sa_ref, sb_ref,
                         q_ref, k_ref, v_ref, kb_ref, vt_ref, ga_ref, qi_ref, ki_ref, kk_ref, wit_ref)
    u = proj(P_C, P_GCONV) * proj(P_U, P_B)
    cw = cw_ref[...]
    conv = proj(P_B, P_C) * (cw[0:1] * st0_ref[...] + cw[1:2] * st1_ref[...] + cw[2:3] * u)
    cvg_ref[0] = (conv * _silu(proj(P_GCONV, P_QMEM))).astype(bf16)
    u_ref[0] = u
    qm_ref[0] = (proj(P_QMEM, P_GMEM) * QK_SCALE).astype(bf16)
    gm_ref[0] = _silu(proj(P_GMEM, P_END))


def _front_out(B, S, TM, TK):
    tok = lambda d: pl.BlockSpec((1, TM, d), lambda b, j: (b, j, 0))
    specs = [tok(D_ATT), tok(D_KV), tok(D_KV), tok(D_KV),
             pl.BlockSpec((1, TM // TK, D_KV, TK), lambda b, j: (b, j, 0, 0)),
             tok(D_ATT), tok(D_QIDX), tok(IDX_DIM), tok(LANES),
             pl.BlockSpec((1, N_IDX_HEADS, TM), lambda b, j: (b, 0, j))]
    sds = lambda d, t: jax.ShapeDtypeStruct((B, S, d), t)
    shapes = [sds(D_ATT, bf16), sds(D_KV, f32), sds(D_KV, f32), sds(D_KV, bf16),
              jax.ShapeDtypeStruct((B, S // TK, D_KV, TK), bf16),
              sds(D_ATT, f32), sds(D_QIDX, bf16), sds(IDX_DIM, f32), sds(LANES, bf16),
              jax.ShapeDtypeStruct((B, N_IDX_HEADS, S), f32)]
    return tok, sds, specs, shapes


def _front_in(TM):
    full = lambda r, c: pl.BlockSpec((r, c), lambda b, j: (0, 0))
    rows = lambda: pl.BlockSpec((TM, LANES), lambda b, j: (j, 0))
    return [pl.BlockSpec((1, TM, D_MODEL), lambda b, j: (b, j, 0)), full(1, D_MODEL), full(D_MODEL, P_END),
            full(D_T, D_MODEL), rows(), rows(), rows(), full(CONV_WIDTH, D_CONV)]


def _front_prompt(x, g, w, wt, cos, sa, sb, cw, mkb, mvb, TM, TK):
    B, S, _ = x.shape
    M = mkb.shape[1]
    tok, sds, specs, shapes = _front_out(B, S, TM, TK)
    mem = lambda: pl.BlockSpec((1, M, D_MEM), lambda b, j: (b, 0, 0))
    return pl.pallas_call(
        _front_prompt_kernel,
        grid=(B, S // TM),
        in_specs=_front_in(TM) + [mem(), mem()],
        out_specs=specs + [tok(D_CONV), pl.BlockSpec((1, CONV_WIDTH - 1, D_CONV), lambda b, j: (b, 0, 0)),
                           tok(D_MEM)],
        out_shape=shapes + [sds(D_CONV, bf16), jax.ShapeDtypeStruct((B, CONV_WIDTH - 1, D_CONV), f32),
                            sds(D_MEM, bf16)],
        scratch_shapes=[pltpu.VMEM((TM + 8, D_CONV), f32)],
        compiler_params=pltpu.CompilerParams(dimension_semantics=("arbitrary", "arbitrary"),
                                             vmem_limit_bytes=VMEM_LIMIT),
        name="front_prompt",
    )(x, g, w, wt, cos, sa, sb, cw, mkb, mvb)


def _front_sample(x, g, w, wt, cos, sa, sb, cw, st0, st1):
    _, T, _ = x.shape
    tok, sds, specs, shapes = _front_out(1, T, T, T)
    st = lambda: pl.BlockSpec((T, D_CONV), lambda b, j: (0, 0))
    return pl.pallas_call(
        _front_sample_kernel,
        grid=(1, 1),
        in_specs=_front_in(T) + [st(), st()],
        out_specs=specs + [tok(D_CONV), tok(D_CONV), tok(D_MEM), tok(D_MEM)],
        out_shape=shapes + [sds(D_CONV, bf16), sds(D_CONV, f32), sds(D_MEM, bf16), sds(D_MEM, f32)],
        compiler_params=pltpu.CompilerParams(dimension_semantics=("arbitrary", "arbitrary"),
                                             vmem_limit_bytes=VMEM_LIMIT),
        name="front_sample",
    )(x, g, w, wt, cos, sa, sb, cw, st0, st1)


def _half_masked(slab, half):
    lane = lax.broadcasted_iota(jnp.int32, slab.shape, 1)
    keep = (lane < HEAD_DIM) if half == 0 else (lane >= HEAD_DIM)
    return jnp.where(keep, slab, jnp.zeros((), slab.dtype))


def _attn_prompt_kernel(q_ref, qi_ref, wit_ref, kb_ref, vt_ref, kk_ref, ga_ref, cvg_ref, mg_ref, x_ref,
                        wo_ref, gp_ref, y_ref, sc_ref, bias_ref, acc_ref):
    TQ = q_ref.shape[1]
    TK = TQ
    S = kb_ref.shape[1]
    i = pl.program_id(1)
    chunk = lambda c: pl.ds(pl.multiple_of(c * TK, TK), TK)
    slab = lambda ref, p: ref[0, :, (p // 2) * LANES:(p // 2 + 1) * LANES]

    blk = lambda a, p: a[:, p * TQ:(p + 1) * TQ]
    qi_all = jnp.concatenate([_half_masked(slab(qi_ref, h), h % 2) for h in range(N_IDX_HEADS)], axis=0)
    wit = wit_ref[0]

    def scores_t(c):
        d = _dot_nt(kk_ref[0, chunk(c), :], qi_all)
        acc = jnp.maximum(blk(d, 0), 0.0) * wit[0:1, :]
        for h in range(1, N_IDX_HEADS):
            acc = acc + jnp.maximum(blk(d, h), 0.0) * wit[h:h + 1, :]
        return acc

    def score_body(c, carry):
        sc_ref[c] = scores_t(c)
        return carry

    lax.fori_loop(0, i, score_body, 0)
    key_pos = lax.broadcasted_iota(jnp.int32, (TK, TQ), 0)
    qry_pos = lax.broadcasted_iota(jnp.int32, (TK, TQ), 1)
    sc_ref[i] = jnp.where(key_pos <= qry_pos, scores_t(i), -jnp.inf)

    t1 = i * TQ + 1 + lax.broadcasted_iota(jnp.int32, (1, TQ), 1)
    k_eff = jnp.minimum(t1, min(TOPK_MAX, S // 4)).astype(f32)
    _select_bias_t(sc_ref, bias_ref, i + 1, k_eff)

    q_all = jnp.concatenate([_half_masked(slab(q_ref, p), p % 2) for p in range(N_ATT_HEADS)], axis=0)
    acc_ref[...] = jnp.zeros(acc_ref.shape, f32)

    def att_body(c, carry):
        ms, ls = carry
        bias = bias_ref[c]
        s_all = _dot_nt(kb_ref[0, chunk(c), :], q_all)
        ms_new, ls_new, alphas, pts = [], [], [], []
        for p in range(N_ATT_HEADS):
            s = blk(s_all, p) + bias
            m_new = jnp.maximum(ms[p], jnp.max(s, axis=0, keepdims=True))
            alpha = jnp.exp(ms[p] - m_new)
            pt = jnp.exp(s - m_new)
            ls_new.append(alpha * ls[p] + jnp.sum(pt, axis=0, keepdims=True))
            ms_new.append(m_new)
            alphas.append(alpha)
            pts.append(pt.astype(bf16))
        pv_all = _dot(vt_ref[0, c], jnp.concatenate(pts, axis=1))
        for p in range(N_ATT_HEADS):
            rows = slice(p * HEAD_DIM, (p + 1) * HEAD_DIM)
            g0 = (p % 2) * HEAD_DIM
            acc_ref[rows, :] = alphas[p] * acc_ref[rows, :] + blk(pv_all, p)[g0:g0 + HEAD_DIM, :]
        return tuple(ms_new), tuple(ls_new)

    init = (tuple(jnp.full((1, TQ), NEG, f32) for _ in range(N_ATT_HEADS)),
            tuple(jnp.zeros((1, TQ), f32) for _ in range(N_ATT_HEADS)))
    _, ls = lax.fori_loop(0, i + 1, att_body, init)
    for p in range(N_ATT_HEADS):
        rows = slice(p * HEAD_DIM, (p + 1) * HEAD_DIM)
        acc_ref[rows, :] = acc_ref[rows, :] / ls[p]
    att_g = (acc_ref[...].T * ga_ref[0]).astype(bf16)
    y_ref[0] = _merge(att_g, cvg_ref[0], mg_ref[0], x_ref[0], wo_ref, gp_ref[...])


def _attn_prompt(q, qi, wit, kb, vt, kk, ga, cvg, mg, x, wo, gp, TQ):
    B, S, _ = x.shape
    NC = S // TQ
    tok = lambda d: pl.BlockSpec((1, TQ, d), lambda b, i: (b, i, 0))
    seq = lambda d: pl.BlockSpec((1, S, d), lambda b, i: (b, 0, 0))
    full = lambda r, c: pl.BlockSpec((r, c), lambda b, i: (0, 0))
    return pl.pallas_call(
        _attn_prompt_kernel,
        grid=(B, NC),
        in_specs=[tok(D_ATT), tok(D_QIDX), pl.BlockSpec((1, N_IDX_HEADS, TQ), lambda b, i: (b, 0, i)),
                  seq(D_KV), pl.BlockSpec((1, NC, D_KV, TQ), lambda b, i: (b, 0, 0, 0)), seq(LANES),
                  tok(D_ATT), tok(D_CONV), tok(D_MEM), tok(D_MODEL), full(D_MODEL, D_MODEL), full(1, D_MODEL)],
        out_specs=tok(D_MODEL),
        out_shape=jax.ShapeDtypeStruct((B, S, D_MODEL), f32),
        scratch_shapes=[pltpu.VMEM((NC, TQ, TQ), f32), pltpu.VMEM((NC, TQ, TQ), f32),
                        pltpu.VMEM((D_ATT, TQ), f32)],
        compiler_params=pltpu.CompilerParams(dimension_semantics=("arbitrary", "arbitrary"),
                                             vmem_limit_bytes=VMEM_LIMIT),
        name="attn_prompt",
    )(q, qi, wit, kb, vt, kk, ga, cvg, mg, x, wo, gp)


def _paged_copies(pt_ref, b, hbm_ref, buf, sem, slot, n_pages, page):
    return [pltpu.make_async_copy(hbm_ref.at[pt_ref[b, p]], buf.at[slot, :, pl.ds(p * page, page)], sem.at[slot])
            for p in range(n_pages)]


def _sidx_kernel(pt_ref, qi_ref, wi_ref, knew_ref, kidx_hbm, sc_ref, dn_ref, buf, sem):
    b = pl.program_id(0)
    nb = pl.num_programs(0)
    n_pages = pt_ref.shape[1]
    page = kidx_hbm.shape[2]
    copies = lambda bb, slot: _paged_copies(pt_ref, bb, kidx_hbm, buf, sem, slot, n_pages, page)
    slot = lax.rem(b, 2)

    @pl.when(b == 0)
    def _():
        for cp in copies(0, 0):
            cp.start()

    @pl.when(b + 1 < nb)
    def _():
        for cp in copies(b + 1, 1 - slot):
            cp.start()

    for cp in copies(b, slot):
        cp.wait()

    qi = qi_ref[0]
    wi = wi_ref[0]
    weigh = lambda d: jnp.sum(jnp.maximum(d, 0.0) * wi, axis=0, keepdims=True)
    sub = 1024
    for j in range(n_pages * page // sub):
        sc_ref[0, :, j * sub:(j + 1) * sub] = weigh(_dot(qi, buf[slot, :, j * sub:(j + 1) * sub].astype(bf16)))
    dn_ref[0] = weigh(_dot_nt(qi, knew_ref[...]))


def _sidx(pt, qi8, wi8, knew, kidx_t_pool):
    nb, n_pages = pt.shape
    page = kidx_t_pool.shape[2]
    past = n_pages * page
    return pl.pallas_call(
        _sidx_kernel,
        grid_spec=pltpu.PrefetchScalarGridSpec(
            num_scalar_prefetch=1,
            grid=(nb,),
            in_specs=[pl.BlockSpec((1, N_IDX_HEADS, IDX_DIM), lambda b, pt: (b, 0, 0)),
                      pl.BlockSpec((1, N_IDX_HEADS, 1), lambda b, pt: (b, 0, 0)),
                      pl.BlockSpec((nb, IDX_DIM), lambda b, pt: (0, 0)),
                      pl.BlockSpec(memory_space=pl.ANY)],
            out_specs=[pl.BlockSpec((1, 1, past), lambda b, pt: (b, 0, 0)),
                       pl.BlockSpec((1, 1, nb), lambda b, pt: (b, 0, 0))],
            scratch_shapes=[pltpu.VMEM((2, IDX_DIM, past), f32), pltpu.SemaphoreType.DMA((2,))]),
        out_shape=[jax.ShapeDtypeStruct((nb, 1, past), f32), jax.ShapeDtypeStruct((nb, 1, nb), f32)],
        compiler_params=pltpu.CompilerParams(dimension_semantics=("arbitrary",), vmem_limit_bytes=VMEM_LIMIT),
        name="sample_indexer",
    )(pt, qi8, wi8, knew, kidx_t_pool)


def _ssel_kernel(sct_ref, dn_ref, out_ref, sc_ref, bias_ref):
    nc, TK, Q = sc_ref.shape
    for c in range(nc - 1):
        sc_ref[c] = sct_ref[c * TK:(c + 1) * TK, :]
    row = lax.broadcasted_iota(jnp.int32, (TK, Q), 0)
    col = lax.broadcasted_iota(jnp.int32, (TK, Q), 1)
    sc_ref[nc - 1] = jnp.where(row == col, dn_ref[...], -jnp.inf)
    n_valid = (nc - 1) * TK + 1
    k_eff = jnp.full((1, Q), float(min(TOPK_MAX, n_valid // 4)), f32)
    _select_bias_t(sc_ref, bias_ref, nc, k_eff)
    for c in range(nc):
        out_ref[c * TK:(c + 1) * TK, :] = bias_ref[c]


def _ssel(sc_t, dn_t):
    past, nb = sc_t.shape
    nc = past // nb + 1
    return pl.pallas_call(
        _ssel_kernel,
        out_shape=jax.ShapeDtypeStruct((past + nb, nb), f32),
        scratch_shapes=[pltpu.VMEM((nc, nb, nb), f32), pltpu.VMEM((nc, nb, nb), f32)],
        compiler_params=pltpu.CompilerParams(vmem_limit_bytes=VMEM_LIMIT),
        name="sample_select",
    )(sc_t, dn_t)


def _sattn_kernel(pt_ref, qbd_ref, bias_ref, knew_ref, vnew_ref, qmbd_ref, cmk_ref, cmv_ref, k_hbm, v_hbm,
                  o_ref, mo_ref, kbuf, vbuf, semk, semv):
    b = pl.program_id(0)
    nb = pl.num_programs(0)
    n_pages = pt_ref.shape[1]
    page = k_hbm.shape[2]
    past = n_pages * page
    slot = lax.rem(b, 2)

    def copies(bb, sl):
        return (_paged_copies(pt_ref, bb, k_hbm, kbuf, semk, sl, n_pages, page)
                + _paged_copies(pt_ref, bb, v_hbm, vbuf, semv, sl, n_pages, page))

    @pl.when(b == 0)
    def _():
        for cp in copies(0, 0):
            cp.start()

    @pl.when(b + 1 < nb)
    def _():
        for cp in copies(b + 1, 1 - slot):
            cp.start()

    qm = qmbd_ref[0]
    pm, lm = _softmax_parts(_dot(qm, cmk_ref[0].astype(bf16)))
    mo = _dot_nt(pm.astype(bf16), cmv_ref[0].astype(bf16)) / lm
    for h in range(N_MEM_HEADS):
        hs = slice(h * HEAD_DIM, (h + 1) * HEAD_DIM)
        mo_ref[0, :, hs] = mo[h:h + 1, hs]

    for cp in copies(b, slot):
        cp.wait()

    qbd = qbd_ref[0]
    s_past = _dot(qbd, kbuf[slot].astype(bf16)) + bias_ref[0, :, 0:past]
    s_new = _dot_nt(qbd, knew_ref[...]) + bias_ref[0, :, past:past + nb]
    m = jnp.maximum(jnp.max(s_past, axis=1, keepdims=True), jnp.max(s_new, axis=1, keepdims=True))
    p_past = jnp.exp(s_past - m)
    p_new = jnp.exp(s_new - m)
    l = jnp.sum(p_past, axis=1, keepdims=True) + jnp.sum(p_new, axis=1, keepdims=True)
    o = _dot_nt(p_past.astype(bf16), vbuf[slot].astype(bf16)) + _dot(p_new.astype(bf16), vnew_ref[...])
    o = o / l
    for p in range(N_ATT_HEADS):
        g0 = (p % 2) * HEAD_DIM
        o_ref[0, :, p * HEAD_DIM:(p + 1) * HEAD_DIM] = o[p:p + 1, g0:g0 + HEAD_DIM]


def _sattn(pt, qbd, bias, knew, vnew, qmbd, cmk_t, cmv_t, k_t_pool, v_t_pool):
    nb, n_pages = pt.shape
    page = k_t_pool.shape[2]
    past = n_pages * page
    M = cmk_t.shape[2]
    per = lambda r, c: pl.BlockSpec((1, r, c), lambda b, pt: (b, 0, 0))
    full = lambda r, c: pl.BlockSpec((r, c), lambda b, pt: (0, 0))
    return pl.pallas_call(
        _sattn_kernel,
        grid_spec=pltpu.PrefetchScalarGridSpec(
            num_scalar_prefetch=1,
            grid=(nb,),
            in_specs=[per(N_ATT_HEADS, D_KV), per(1, past + nb), full(nb, D_KV), full(nb, D_KV),
                      per(N_MEM_HEADS, D_MEM), per(D_MEM, M), per(D_MEM, M),
                      pl.BlockSpec(memory_space=pl.ANY), pl.BlockSpec(memory_space=pl.ANY)],
            out_specs=[per(1, D_ATT), per(1, D_MEM)],
            scratch_shapes=[pltpu.VMEM((2, D_KV, past), f32), pltpu.VMEM((2, D_KV, past), f32),
                            pltpu.SemaphoreType.DMA((2,)), pltpu.SemaphoreType.DMA((2,))]),
        out_shape=[jax.ShapeDtypeStruct((nb, 1, D_ATT), f32), jax.ShapeDtypeStruct((nb, 1, D_MEM), f32)],
        compiler_params=pltpu.CompilerParams(dimension_semantics=("arbitrary",), vmem_limit_bytes=VMEM_LIMIT),
        name="sample_attention",
    )(pt, qbd, bias, knew, vnew, qmbd, cmk_t, cmv_t, k_t_pool, v_t_pool)


def _smerge_kernel(att_ref, ga_ref, cvg_ref, mo_ref, gm_ref, x_ref, wo_ref, gp_ref, y_ref):
    att_g = (att_ref[...] * ga_ref[...]).astype(bf16)
    mg = (mo_ref[...] * gm_ref[...]).astype(bf16)
    y_ref[...] = _merge(att_g, cvg_ref[...], mg, x_ref[...], wo_ref, gp_ref[...])


def _smerge(att, ga, cvg, mo, gm, x, wo, gp):
    return pl.pallas_call(
        _smerge_kernel,
        out_shape=jax.ShapeDtypeStruct(x.shape, f32),
        compiler_params=pltpu.CompilerParams(vmem_limit_bytes=VMEM_LIMIT),
        name="sample_merge",
    )(att, ga, cvg, mo, gm, x, wo, gp)


def _rope_tables(pos):
    half = HEAD_DIM // 2
    inv = ROPE_THETA ** (-jnp.arange(half, dtype=f32) / half)
    ang = pos.astype(f32)[:, None] * inv[None, :]
    cos, sin = jnp.cos(ang), jnp.sin(ang)
    zero = jnp.zeros_like(sin)
    rep = LANES // HEAD_DIM
    return (jnp.tile(jnp.concatenate([cos, cos], axis=1), (1, rep)),
            jnp.tile(jnp.concatenate([-sin, zero], axis=1), (1, rep)),
            jnp.tile(jnp.concatenate([zero, sin], axis=1), (1, rep)))


def _pack_w_in(w):
    head = lambda base: jnp.concatenate([w[:, base + h * HEAD_DIM:base + (h + 1) * HEAD_DIM] for h in ATT_ORDER], 1)
    kidx = w[:, R_KIDX:R_WIDX]
    packed = jnp.concatenate([head(R_Q), w[:, R_K:R_GATT], head(R_GATT), w[:, R_QIDX:R_KIDX], kidx, kidx,
                              w[:, R_U:]], axis=1)
    w_t = jnp.concatenate([w[:, R_V:R_GATT], w[:, R_WIDX:R_U], jnp.zeros((D_MODEL, D_T - D_KV - N_IDX_HEADS), w.dtype)],
                          axis=1).T
    return packed.astype(bf16), w_t.astype(bf16)


def _pack_w_out(w):
    att = jnp.concatenate([w[h * HEAD_DIM:(h + 1) * HEAD_DIM] for h in ATT_ORDER], axis=0)
    return jnp.concatenate([att, w[D_ATT:]], axis=0).astype(bf16)


def _slab_rows(x, n_rows):
    T = x.shape[0]
    slabs = jnp.repeat(x.reshape(T, n_rows // 2, 1, 2, HEAD_DIM), 2, axis=2)
    keep = jnp.arange(2)[:, None] == jnp.arange(2)[None, :]
    return jnp.where(keep[None, None, :, :, None], slabs, jnp.zeros((), x.dtype)).reshape(T, n_rows, LANES)


def _diag_rows(x, n_heads):
    T = x.shape[0]
    keep = jnp.arange(n_heads)[:, None] == jnp.arange(n_heads)[None, :]
    xh = x.reshape(T, 1, n_heads, HEAD_DIM)
    return jnp.where(keep[None, :, :, None], xh, jnp.zeros((), x.dtype)).reshape(T, n_heads, n_heads * HEAD_DIM)


def _layer(l, xp, xs, mem_prompt, cache_k, cache_v, cache_kidx, cache_mem_k, cache_mem_v, state_conv,
           page_table, g_pre, w_in, conv_w, g_memnorm, w_mem_kv, w_out, g_post):
    B, S, _ = xp.shape
    nb, T, _ = xs.shape
    assert T == 1
    n_pages = page_table.shape[1]
    n_pool, page = cache_k.shape[1], cache_k.shape[2]
    past = n_pages * page
    M = mem_prompt.shape[1]
    assert page == nb and past % 1024 == 0

    row = lambda a: a[l][None, :]
    w_pack, w_t = _pack_w_in(w_in[l])
    wo = _pack_w_out(w_out[l])
    gp = row(g_post)

    mk, mv, mkb, mvb = _memkv(mem_prompt, row(g_memnorm), w_mem_kv[l].astype(bf16))
    cos, sa, sb = _rope_tables(jnp.arange(S, dtype=jnp.int32))
    TQ = min(256, S)
    TM = min(512, S)
    (q, k, v, kb, vt, ga, qi, ki, kk, wit, cvg, cst, mg) = _front_prompt(
        xp, row(g_pre), w_pack, w_t, cos, sa, sb, conv_w[l], mkb, mvb, TM, TQ)
    yp = _attn_prompt(q, qi, wit, kb, vt, kk, ga, cvg, mg, xp, wo, gp, TQ)

    cos, sa, sb = _rope_tables(jnp.full((nb,), past, jnp.int32))
    st = state_conv[l]
    (q_s, k_s, v_s, kb_s, vt_s, ga_s, qi_s, ki_s, kk_s, wit_s, cvg_s, u_s, qm_s, gm_s) = _front_sample(
        xs.reshape(1, nb, D_MODEL), row(g_pre), w_pack, w_t, cos, sa, sb, conv_w[l], st[:, 0, :], st[:, 1, :])
    kidx_t = jnp.swapaxes(cache_kidx[l], 1, 2)
    k_t = jnp.transpose(cache_k[l], (0, 2, 3, 1)).reshape(n_pool, D_KV, page)
    v_t = jnp.transpose(cache_v[l], (0, 2, 3, 1)).reshape(n_pool, D_KV, page)
    cmk_t = jnp.transpose(cache_mem_k[l], (0, 2, 3, 1)).reshape(nb, D_MEM, M)
    cmv_t = jnp.transpose(cache_mem_v[l], (0, 2, 3, 1)).reshape(nb, D_MEM, M)

    sc, dn = _sidx(page_table, qi_s.reshape(nb, N_IDX_HEADS, IDX_DIM), wit_s[0].T.reshape(nb, N_IDX_HEADS, 1),
                   kk_s[0, :, :IDX_DIM], kidx_t)
    bias_t = _ssel(sc.reshape(nb, past).T, dn.reshape(nb, nb).T)
    att_s, mo_s = _sattn(page_table, _slab_rows(q_s[0], N_ATT_HEADS), bias_t.T.reshape(nb, 1, past + nb),
                         kb_s[0], vt_s[0, 0].T, _diag_rows(qm_s[0], N_MEM_HEADS), cmk_t, cmv_t, k_t, v_t)
    ys = _smerge(att_s.reshape(nb, D_ATT), ga_s[0], cvg_s[0], mo_s.reshape(nb, D_MEM), gm_s[0],
                 xs.reshape(nb, D_MODEL), wo, gp)

    outs_p = (k.reshape(B, S, N_KV_HEADS, HEAD_DIM), v.reshape(B, S, N_KV_HEADS, HEAD_DIM), ki, cst,
              mk.reshape(B, M, N_MEM_HEADS, HEAD_DIM), mv.reshape(B, M, N_MEM_HEADS, HEAD_DIM))
    outs_s = (k_s.reshape(nb, 1, N_KV_HEADS, HEAD_DIM), v_s.reshape(nb, 1, N_KV_HEADS, HEAD_DIM),
              ki_s.reshape(nb, 1, IDX_DIM), jnp.stack([st[:, 1, :], u_s[0]], axis=1))
    return yp, ys.reshape(nb, 1, D_MODEL), outs_p, outs_s


def kernel(x_prompt, x_sample, mem_prompt, cache_k, cache_v, cache_kidx, cache_mem_k, cache_mem_v, state_conv,
           page_table, g_pre, w_in, conv_w, g_memnorm, w_mem_kv, w_out, g_post):
    depth = w_in.shape[0]
    xp, xs = x_prompt, x_sample
    per_p, per_s = [], []
    for l in range(depth):
        xp, xs, op, os_ = _layer(l, xp, xs, mem_prompt, cache_k, cache_v, cache_kidx, cache_mem_k, cache_mem_v,
                                 state_conv, page_table, g_pre, w_in, conv_w, g_memnorm, w_mem_kv, w_out, g_post)
        per_p.append(op)
        per_s.append(os_)
    stack = lambda outs, i: jnp.stack([o[i] for o in outs])
    return ((xp, xs) + tuple(stack(per_p, i) for i in range(6)) + tuple(stack(per_s, i) for i in range(4)))
```

```python
import jax
import jax.numpy as jnp
from jax import lax
from jax.experimental import pallas as pl
from jax.experimental.pallas import tpu as pltpu

D_MODEL = 1024
HEAD_DIM = 64
N_MEM_HEADS = 4
D_MEM = N_MEM_HEADS * HEAD_DIM
D_ATT = 512
N_ATT_HEADS = 8
N_KV_HEADS = 2
Q_PER_KV = N_ATT_HEADS // N_KV_HEADS
D_KV = N_KV_HEADS * HEAD_DIM
D_CONV = 256
N_IDX_HEADS = 8
IDX_DIM = 64
D_QIDX = N_IDX_HEADS * IDX_DIM
TOPK_MAX = 256
CONV_WIDTH = 3
ROPE_THETA = 10000.0
RMS_EPS = 1e-6
LANES = 128
SUBLANES = 8

R_Q = 0
R_K = R_Q + D_ATT
R_V = R_K + D_KV
R_GATT = R_V + D_KV
R_QIDX = R_GATT + D_ATT
R_KIDX = R_QIDX + D_QIDX
R_WIDX = R_KIDX + IDX_DIM
R_U = R_WIDX + N_IDX_HEADS

ATT_ORDER = (0, 4, 1, 5, 2, 6, 3, 7)
P_Q = 0
P_K = P_Q + D_ATT
P_V = P_K + D_KV
P_GATT = P_V + D_KV
P_QIDX = P_GATT + D_ATT
P_KK = P_QIDX + D_QIDX
P_U = P_KK + LANES
P_B = P_U + D_CONV
P_C = P_B + D_CONV
P_GCONV = P_C + D_CONV
P_QMEM = P_GCONV + D_CONV
P_GMEM = P_QMEM + D_MEM
P_END = P_GMEM + D_MEM

QK_SCALE = HEAD_DIM ** -0.5
LOG2E = 1.4426950408889634
IDX_SCALE = IDX_DIM ** -0.5
W_SCALE = N_IDX_HEADS ** -0.5

INT_MIN = -2 ** 31
NEG = -1e30
VMEM_LIMIT = 56 * 1024 * 1024
ACC_ROWS = 32
D_T = D_KV + 16

bf16 = jnp.bfloat16
f32 = jnp.float32
NT = (((1,), (1,)), ((), ()))


def _dot(a, b):
    return jnp.dot(a, b, preferred_element_type=f32)


def _dot_nt(a, b):
    return lax.dot_general(a, b, NT, preferred_element_type=f32)


def _rms(x, g):
    return x * lax.rsqrt(jnp.mean(x * x, axis=-1, keepdims=True) + RMS_EPS) * g


def _silu(x):
    return x / (1.0 + jnp.exp(-x))


def _rope(z, cos, sa, sb):
    return z * cos + pltpu.roll(z, 96, 1) * sa + pltpu.roll(z, 32, 1) * sb


def _key_to_f32(k):
    return lax.bitcast_convert_type(k ^ ((k >> 31) & jnp.int32(0x7FFFFFFF)), f32)


def _softmax_parts(s):
    m = jnp.max(s, axis=-1, keepdims=True)
    p = jnp.exp(s - m)
    return p, jnp.sum(p, axis=-1, keepdims=True)


def _select_bias_t(sc_ref, bias_ref, nc, k_eff):
    _, TK, Q = sc_ref.shape

    def count(cmp, thr):
        thr_b = jnp.broadcast_to(thr, (ACC_ROWS, Q))

        def body(c, acc):
            for j in range(TK // ACC_ROWS):
                acc = acc + jnp.where(cmp(sc_ref[c, j * ACC_ROWS:(j + 1) * ACC_ROWS, :], thr_b), 1.0, 0.0)
            return acc

        acc = lax.fori_loop(0, nc, body, jnp.zeros((ACC_ROWS, Q), f32))
        return jnp.sum(acc, axis=0, keepdims=True)

    def bit_body(it, carry):
        u, cnt_u = carry
        cand = u | jnp.left_shift(jnp.int32(1), 31 - it)
        cnt = count(lambda a, b: a >= b, _key_to_f32(cand ^ jnp.int32(INT_MIN)))
        ok = cnt >= k_eff
        return jnp.where(ok, cand, u), jnp.where(ok, cnt, cnt_u)

    u, cnt_u = lax.fori_loop(0, 32, bit_body, (jnp.zeros((1, Q), jnp.int32), jnp.full((1, Q), -1.0, f32)))
    thr = _key_to_f32(u ^ jnp.int32(INT_MIN))
    has_tie = jnp.max(jnp.where(cnt_u != k_eff, 1.0, 0.0)) > 0.5

    @pl.when(jnp.logical_not(has_tie))
    def _():
        def bias_body(c, carry):
            bias_ref[c] = jnp.where(sc_ref[c] >= thr, 0.0, NEG)
            return carry

        lax.fori_loop(0, nc, bias_body, 0)

    @pl.when(has_tie)
    def _():
        need = k_eff - count(lambda a, b: a > b, thr)
        r_i = lax.broadcasted_iota(jnp.int32, (TK, TK), 0)
        c_i = lax.broadcasted_iota(jnp.int32, (TK, TK), 1)
        tri = jnp.where(c_i <= r_i, 1.0, 0.0).astype(bf16)

        def bias_body(c, run):
            x = sc_ref[c]
            eqf = jnp.where(x == thr, 1.0, 0.0)
            incl = _dot(tri, eqf.astype(bf16))
            tie_ok = jnp.where(run + incl - eqf < need, eqf, 0.0)
            bias_ref[c] = jnp.where(x > thr, 0.0, jnp.where(tie_ok > 0.5, 0.0, NEG))
            return run + incl[TK - 1:TK, :]

        lax.fori_loop(0, nc, bias_body, jnp.zeros((1, Q), f32))


def _merge(att_g, cvg, mg, x, wo_ref, gp):
    o = (_dot(att_g, wo_ref[0:D_ATT, :])
         + _dot(cvg, wo_ref[D_ATT:D_ATT + D_CONV, :])
         + _dot(mg, wo_ref[D_ATT + D_CONV:D_MODEL, :]))
    return x + _rms(o, gp)


def _memkv_kernel(mem_ref, g_ref, w_ref, mk_ref, mv_ref, mkb_ref, mvb_ref):
    kv = _dot(_rms(mem_ref[0], g_ref[...]).astype(bf16), w_ref[...])
    mk = kv[:, :D_MEM]
    mv = kv[:, D_MEM:]
    mk_ref[0] = mk
    mv_ref[0] = mv
    mkb_ref[0] = mk.astype(bf16)
    mvb_ref[0] = mv.astype(bf16)


def _memkv(mem, g, w):
    B, M, _ = mem.shape
    blk = lambda: pl.BlockSpec((1, M, D_MEM), lambda b: (b, 0, 0))
    return pl.pallas_call(
        _memkv_kernel,
        grid=(B,),
        in_specs=[pl.BlockSpec((1, M, D_MODEL), lambda b: (b, 0, 0)),
                  pl.BlockSpec((1, D_MODEL), lambda b: (0, 0)),
                  pl.BlockSpec((D_MODEL, 2 * D_MEM), lambda b: (0, 0))],
        out_specs=[blk(), blk(), blk(), blk()],
        out_shape=[jax.ShapeDtypeStruct((B, M, D_MEM), f32)] * 2
        + [jax.ShapeDtypeStruct((B, M, D_MEM), bf16)] * 2,
        compiler_params=pltpu.CompilerParams(dimension_semantics=("arbitrary",)),
        name="memkv",
    )(mem, g, w)


def _front_common(x_ref, g_ref, w_ref, wt_ref, cos_ref, sa_ref, sb_ref,
                  q_ref, k_ref, v_ref, kb_ref, vt_ref, ga_ref, qi_ref, ki_ref, kk_ref, wit_ref):
    xb = _rms(x_ref[0], g_ref[...]).astype(bf16)
    cos, sa, sb = cos_ref[...], sa_ref[...], sb_ref[...]
    proj = lambda lo, hi: _dot(xb, w_ref[:, lo:hi])
    rope = lambda z: _rope(z, cos, sa, sb)

    zq = proj(P_Q, P_K)
    zqi = proj(P_QIDX, P_KK)
    for s in range(D_ATT // LANES):
        sl = slice(s * LANES, (s + 1) * LANES)
        q_ref[0, :, sl] = (rope(zq[:, sl]) * (QK_SCALE * LOG2E)).astype(bf16)
        qi_ref[0, :, sl] = (rope(zqi[:, sl]) * IDX_SCALE).astype(bf16)
    zkv = proj(P_K, P_GATT)
    zk = rope(zkv[:, :D_KV])
    k_ref[0] = zk
    kb_ref[0] = zk.astype(bf16)
    v_ref[0] = zkv[:, D_KV:]
    ga_ref[0] = _silu(proj(P_GATT, P_QIDX))
    zkk = rope(proj(P_KK, P_U))
    ki_ref[0] = zkk[:, :IDX_DIM]
    kk_ref[0] = zkk.astype(bf16)
    vw_t = _dot_nt(wt_ref[...], xb)
    TK = vt_ref.shape[3]
    for c in range(vt_ref.shape[1]):
        vt_ref[0, c] = vw_t[0:D_KV, c * TK:(c + 1) * TK].astype(bf16)
    wit_ref[0] = vw_t[D_KV:D_KV + N_IDX_HEADS, :] * W_SCALE
    return proj


def _front_prompt_kernel(x_ref, g_ref, w_ref, wt_ref, cos_ref, sa_ref, sb_ref, cw_ref, mkb_ref, mvb_ref,
                         q_ref, k_ref, v_ref, kb_ref, vt_ref, ga_ref, qi_ref, ki_ref, kk_ref, wit_ref,
                         cvg_ref, cst_ref, mg_ref, ubuf):
    TM = x_ref.shape[1]
    proj = _front_common(x_ref, g_ref, w_ref, wt_ref, cos_ref, sa_ref, sb_ref,
                         q_ref, k_ref, v_ref, kb_ref, vt_ref, ga_ref, qi_ref, ki_ref, kk_ref, wit_ref)

    @pl.when(pl.program_id(1) == 0)
    def _():
        ubuf[0:8, :] = jnp.zeros((8, D_CONV), f32)

    u = proj(P_C, P_GCONV) * proj(P_U, P_B)
    ubuf[8:TM + 8, :] = u
    cw = cw_ref[...]
    conv = proj(P_B, P_C) * (cw[0:1] * ubuf[6:TM + 6, :] + cw[1:2] * ubuf[7:TM + 7, :] + cw[2:3] * u)
    cvg_ref[0] = (conv * _silu(proj(P_GCONV, P_QMEM))).astype(bf16)
    tail = u[TM - 2:TM, :]
    ubuf[6:8, :] = tail
    cst_ref[0] = tail

    qm = (proj(P_QMEM, P_GMEM) * QK_SCALE).astype(bf16)
    mkb = mkb_ref[0]
    mvb = mvb_ref[0]
    outs = []
    for h in range(N_MEM_HEADS):
        sl = slice(h * HEAD_DIM, (h + 1) * HEAD_DIM)
        p, l = _softmax_parts(_dot_nt(qm[:, sl], mkb[:, sl]))
        outs.append(_dot(p.astype(bf16), mvb[:, sl]) / l)
    mo = jnp.concatenate(outs, axis=1)
    mg_ref[0] = (mo * _silu(proj(P_GMEM, P_END))).astype(bf16)


def _front_sample_kernel(x_ref, g_ref, w_ref, wt_ref, cos_ref, sa_ref, sb_ref, cw_ref, st0_ref, st1_ref,
                         q_ref, k_ref, v_ref, kb_ref, vt_ref, ga_ref, qi_ref, ki_ref, kk_ref, wit_ref,
                         cvg_ref, u_ref, qm_ref, gm_ref):
    proj = _front_common(x_ref, g_ref, w_ref, wt_ref, cos_ref, sa_ref, sb_ref,
                         q_ref, k_ref, v_ref, kb_ref, vt_ref, ga_ref, qi_ref, ki_ref, kk_ref, wit_ref)
    u = proj(P_C, P_GCONV) * proj(P_U, P_B)
    cw = cw_ref[...]
    conv = proj(P_B, P_C) * (cw[0:1] * st0_ref[...] + cw[1:2] * st1_ref[...] + cw[2:3] * u)
    cvg_ref[0] = (conv * _silu(proj(P_GCONV, P_QMEM))).astype(bf16)
    u_ref[0] = u
    qm_ref[0] = (proj(P_QMEM, P_GMEM) * QK_SCALE).astype(bf16)
    gm_ref[0] = _silu(proj(P_GMEM, P_END))


def _front_out(B, S, TM, TK):
    tok = lambda d: pl.BlockSpec((1, TM, d), lambda b, j: (b, j, 0))
    specs = [tok(D_ATT), tok(D_KV), tok(D_KV), tok(D_KV),
             pl.BlockSpec((1, TM // TK, D_KV, TK), lambda b, j: (b, j, 0, 0)),
             tok(D_ATT), tok(D_QIDX), tok(IDX_DIM), tok(LANES),
             pl.BlockSpec((1, N_IDX_HEADS, TM), lambda b, j: (b, 0, j))]
    sds = lambda d, t: jax.ShapeDtypeStruct((B, S, d), t)
    shapes = [sds(D_ATT, bf16), sds(D_KV, f32), sds(D_KV, f32), sds(D_KV, bf16),
              jax.ShapeDtypeStruct((B, S // TK, D_KV, TK), bf16),
              sds(D_ATT, f32), sds(D_QIDX, bf16), sds(IDX_DIM, f32), sds(LANES, bf16),
              jax.ShapeDtypeStruct((B, N_IDX_HEADS, S), f32)]
    return tok, sds, specs, shapes


def _front_in(TM):
    full = lambda r, c: pl.BlockSpec((r, c), lambda b, j: (0, 0))
    rows = lambda: pl.BlockSpec((TM, LANES), lambda b, j: (j, 0))
    return [pl.BlockSpec((1, TM, D_MODEL), lambda b, j: (b, j, 0)), full(1, D_MODEL), full(D_MODEL, P_END),
            full(D_T, D_MODEL), rows(), rows(), rows(), full(CONV_WIDTH, D_CONV)]


def _front_prompt(x, g, w, wt, cos, sa, sb, cw, mkb, mvb, TM, TK):
    B, S, _ = x.shape
    M = mkb.shape[1]
    tok, sds, specs, shapes = _front_out(B, S, TM, TK)
    mem = lambda: pl.BlockSpec((1, M, D_MEM), lambda b, j: (b, 0, 0))
    return pl.pallas_call(
        _front_prompt_kernel,
        grid=(B, S // TM),
        in_specs=_front_in(TM) + [mem(), mem()],
        out_specs=specs + [tok(D_CONV), pl.BlockSpec((1, CONV_WIDTH - 1, D_CONV), lambda b, j: (b, 0, 0)),
                           tok(D_MEM)],
        out_shape=shapes + [sds(D_CONV, bf16), jax.ShapeDtypeStruct((B, CONV_WIDTH - 1, D_CONV), f32),
                            sds(D_MEM, bf16)],
        scratch_shapes=[pltpu.VMEM((TM + 8, D_CONV), f32)],
        compiler_params=pltpu.CompilerParams(dimension_semantics=("arbitrary", "arbitrary"),
                                             vmem_limit_bytes=VMEM_LIMIT),
        name="front_prompt",
    )(x, g, w, wt, cos, sa, sb, cw, mkb, mvb)


def _front_sample(x, g, w, wt, cos, sa, sb, cw, st0, st1):
    _, T, _ = x.shape
    tok, sds, specs, shapes = _front_out(1, T, T, T)
    st = lambda: pl.BlockSpec((T, D_CONV), lambda b, j: (0, 0))
    return pl.pallas_call(
        _front_sample_kernel,
        grid=(1, 1),
        in_specs=_front_in(T) + [st(), st()],
        out_specs=specs + [tok(D_CONV), tok(D_CONV), tok(D_MEM), tok(D_MEM)],
        out_shape=shapes + [sds(D_CONV, bf16), sds(D_CONV, f32), sds(D_MEM, bf16), sds(D_MEM, f32)],
        compiler_params=pltpu.CompilerParams(dimension_semantics=("arbitrary", "arbitrary"),
                                             vmem_limit_bytes=VMEM_LIMIT),
        name="front_sample",
    )(x, g, w, wt, cos, sa, sb, cw, st0, st1)


def _half_masked(slab, half):
    lane = lax.broadcasted_iota(jnp.int32, slab.shape, 1)
    keep = (lane < HEAD_DIM) if half == 0 else (lane >= HEAD_DIM)
    return jnp.where(keep, slab, jnp.zeros((), slab.dtype))


def _attn_prompt_kernel(q_ref, qi_ref, wit_ref, kb_ref, vt_ref, kk_ref, ga_ref, cvg_ref, mg_ref, x_ref,
                        wo_ref, gp_ref, y_ref, sc_ref, bias_ref, acc_ref):
    TQ = q_ref.shape[1]
    TK = TQ
    S = kb_ref.shape[1]
    i = pl.program_id(1)
    chunk = lambda c: pl.ds(pl.multiple_of(c * TK, TK), TK)
    slab = lambda ref, p: ref[0, :, (p // 2) * LANES:(p // 2 + 1) * LANES]

    blk = lambda a, p: a[:, p * TQ:(p + 1) * TQ]
    qi_all = jnp.concatenate([_half_masked(slab(qi_ref, h), h % 2) for h in range(N_IDX_HEADS)], axis=0)
    wit = wit_ref[0]

    def scores_t(c):
        d = _dot_nt(kk_ref[0, chunk(c), :], qi_all)
        acc = jnp.maximum(blk(d, 0), 0.0) * wit[0:1, :]
        for h in range(1, N_IDX_HEADS):
            acc = acc + jnp.maximum(blk(d, h), 0.0) * wit[h:h + 1, :]
        return acc

    def score_body(j, carry):
        for r in range(2):
            c = jnp.minimum(2 * j + r, i - 1)
            sc_ref[c] = scores_t(c)
        return carry

    lax.fori_loop(0, lax.shift_right_logical(i + 1, 1), score_body, 0)
    key_pos = lax.broadcasted_iota(jnp.int32, (TK, TQ), 0)
    qry_pos = lax.broadcasted_iota(jnp.int32, (TK, TQ), 1)
    sc_ref[i] = jnp.where(key_pos <= qry_pos, scores_t(i), -jnp.inf)

    t1 = i * TQ + 1 + lax.broadcasted_iota(jnp.int32, (1, TQ), 1)
    k_eff = jnp.minimum(t1, min(TOPK_MAX, S // 4)).astype(f32)
    _select_bias_t(sc_ref, bias_ref, i + 1, k_eff)

    q_all = jnp.concatenate([_half_masked(slab(q_ref, p), p % 2) for p in range(N_ATT_HEADS)], axis=0)
    acc_ref[...] = jnp.zeros(acc_ref.shape, f32)

    def att_chunk(c_kv, c_bias, ms, ls):
        bias = bias_ref[c_bias]
        s_all = _dot_nt(kb_ref[0, chunk(c_kv), :], q_all)
        ms_new, ls_new, alphas, pts = [], [], [], []
        for p in range(N_ATT_HEADS):
            s = blk(s_all, p) + bias
            m_new = jnp.maximum(ms[p], jnp.max(s, axis=0, keepdims=True))
            alpha = jnp.exp2(ms[p] - m_new)
            pt = jnp.exp2(s - m_new)
            ls_new.append(alpha * ls[p] + jnp.sum(pt, axis=0, keepdims=True))
            ms_new.append(m_new)
            alphas.append(alpha)
            pts.append(pt.astype(bf16))
        pv_all = _dot(vt_ref[0, c_kv], jnp.concatenate(pts, axis=1))
        for p in range(N_ATT_HEADS):
            rows = slice(p * HEAD_DIM, (p + 1) * HEAD_DIM)
            g0 = (p % 2) * HEAD_DIM
            acc_ref[rows, :] = alphas[p] * acc_ref[rows, :] + blk(pv_all, p)[g0:g0 + HEAD_DIM, :]
        return tuple(ms_new), tuple(ls_new)

    def att_body(j, carry):
        return att_chunk(2 * j + 1, 2 * j + 1, *att_chunk(2 * j, 2 * j, *carry))

    init = (tuple(jnp.full((1, TQ), NEG, f32) for _ in range(N_ATT_HEADS)),
            tuple(jnp.zeros((1, TQ), f32) for _ in range(N_ATT_HEADS)))
    carry = lax.fori_loop(0, lax.shift_right_logical(i + 1, 1), att_body, init)
    _, ls = lax.cond((i + 1) % 2 == 1, lambda ms, ls: att_chunk(i, i, ms, ls), lambda ms, ls: (ms, ls), *carry)
    for p in range(N_ATT_HEADS):
        rows = slice(p * HEAD_DIM, (p + 1) * HEAD_DIM)
        acc_ref[rows, :] = acc_ref[rows, :] / ls[p]
    att_g = (acc_ref[...].T * ga_ref[0]).astype(bf16)
    y_ref[0] = _merge(att_g, cvg_ref[0], mg_ref[0], x_ref[0], wo_ref, gp_ref[...])


def _attn_prompt(q, qi, wit, kb, vt, kk, ga, cvg, mg, x, wo, gp, TQ):
    B, S, _ = x.shape
    NC = S // TQ
    tok = lambda d: pl.BlockSpec((1, TQ, d), lambda b, i: (b, i, 0))
    seq = lambda d: pl.BlockSpec((1, S, d), lambda b, i: (b, 0, 0))
    full = lambda r, c: pl.BlockSpec((r, c), lambda b, i: (0, 0))
    return pl.pallas_call(
        _attn_prompt_kernel,
        grid=(B, NC),
        in_specs=[tok(D_ATT), tok(D_QIDX), pl.BlockSpec((1, N_IDX_HEADS, TQ), lambda b, i: (b, 0, i)),
                  seq(D_KV), pl.BlockSpec((1, NC, D_KV, TQ), lambda b, i: (b, 0, 0, 0)), seq(LANES),
                  tok(D_ATT), tok(D_CONV), tok(D_MEM), tok(D_MODEL), full(D_MODEL, D_MODEL), full(1, D_MODEL)],
        out_specs=tok(D_MODEL),
        out_shape=jax.ShapeDtypeStruct((B, S, D_MODEL), f32),
        scratch_shapes=[pltpu.VMEM((NC, TQ, TQ), f32), pltpu.VMEM((NC, TQ, TQ), f32),
                        pltpu.VMEM((D_ATT, TQ), f32)],
        compiler_params=pltpu.CompilerParams(dimension_semantics=("arbitrary", "arbitrary"),
                                             vmem_limit_bytes=VMEM_LIMIT),
        name="attn_prompt",
    )(q, qi, wit, kb, vt, kk, ga, cvg, mg, x, wo, gp)


def _paged_copies(pt_ref, b, hbm_ref, buf, sem, slot, n_pages, page):
    return [pltpu.make_async_copy(hbm_ref.at[pt_ref[b, p]], buf.at[slot, :, pl.ds(p * page, page)], sem.at[slot])
            for p in range(n_pages)]


def _sidx_kernel(pt_ref, qi_ref, wi_ref, knew_ref, kidx_hbm, sc_ref, dn_ref, buf, sem):
    b = pl.program_id(0)
    nb = pl.num_programs(0)
    n_pages = pt_ref.shape[1]
    page = kidx_hbm.shape[2]
    copies = lambda bb, slot: _paged_copies(pt_ref, bb, kidx_hbm, buf, sem, slot, n_pages, page)
    slot = lax.rem(b, 2)

    @pl.when(b == 0)
    def _():
        for cp in copies(0, 0):
            cp.start()

    @pl.when(b + 1 < nb)
    def _():
        for cp in copies(b + 1, 1 - slot):
            cp.start()

    for cp in copies(b, slot):
        cp.wait()

    qi = qi_ref[0]
    wi = wi_ref[0]
    weigh = lambda d: jnp.sum(jnp.maximum(d, 0.0) * wi, axis=0, keepdims=True)
    sub = 1024
    for j in range(n_pages * page // sub):
        sc_ref[0, :, j * sub:(j + 1) * sub] = weigh(_dot(qi, buf[slot, :, j * sub:(j + 1) * sub].astype(bf16)))
    dn_ref[0] = weigh(_dot_nt(qi, knew_ref[...]))


def _sidx(pt, qi8, wi8, knew, kidx_t_pool):
    nb, n_pages = pt.shape
    page = kidx_t_pool.shape[2]
    past = n_pages * page
    return pl.pallas_call(
        _sidx_kernel,
        grid_spec=pltpu.PrefetchScalarGridSpec(
            num_scalar_prefetch=1,
            grid=(nb,),
            in_specs=[pl.BlockSpec((1, N_IDX_HEADS, IDX_DIM), lambda b, pt: (b, 0, 0)),
                      pl.BlockSpec((1, N_IDX_HEADS, 1), lambda b, pt: (b, 0, 0)),
                      pl.BlockSpec((nb, IDX_DIM), lambda b, pt: (0, 0)),
                      pl.BlockSpec(memory_space=pl.ANY)],
            out_specs=[pl.BlockSpec((1, 1, past), lambda b, pt: (b, 0, 0)),
                       pl.BlockSpec((1, 1, nb), lambda b, pt: (b, 0, 0))],
            scratch_shapes=[pltpu.VMEM((2, IDX_DIM, past), f32), pltpu.SemaphoreType.DMA((2,))]),
        out_shape=[jax.ShapeDtypeStruct((nb, 1, past), f32), jax.ShapeDtypeStruct((nb, 1, nb), f32)],
        compiler_params=pltpu.CompilerParams(dimension_semantics=("arbitrary",), vmem_limit_bytes=VMEM_LIMIT),
        name="sample_indexer",
    )(pt, qi8, wi8, knew, kidx_t_pool)


def _ssel_kernel(sct_ref, dn_ref, out_ref, sc_ref, bias_ref):
    nc, TK, Q = sc_ref.shape
    for c in range(nc - 1):
        sc_ref[c] = sct_ref[c * TK:(c + 1) * TK, :]
    row = lax.broadcasted_iota(jnp.int32, (TK, Q), 0)
    col = lax.broadcasted_iota(jnp.int32, (TK, Q), 1)
    sc_ref[nc - 1] = jnp.where(row == col, dn_ref[...], -jnp.inf)
    n_valid = (nc - 1) * TK + 1
    k_eff = jnp.full((1, Q), float(min(TOPK_MAX, n_valid // 4)), f32)
    _select_bias_t(sc_ref, bias_ref, nc, k_eff)
    for c in range(nc):
        out_ref[c * TK:(c + 1) * TK, :] = bias_ref[c]


def _ssel(sc_t, dn_t):
    past, nb = sc_t.shape
    nc = past // nb + 1
    return pl.pallas_call(
        _ssel_kernel,
        out_shape=jax.ShapeDtypeStruct((past + nb, nb), f32),
        scratch_shapes=[pltpu.VMEM((nc, nb, nb), f32), pltpu.VMEM((nc, nb, nb), f32)],
        compiler_params=pltpu.CompilerParams(vmem_limit_bytes=VMEM_LIMIT),
        name="sample_select",
    )(sc_t, dn_t)


def _sattn_kernel(pt_ref, qbd_ref, bias_ref, knew_ref, vnew_ref, qmbd_ref, cmk_ref, cmv_ref, k_hbm, v_hbm,
                  o_ref, mo_ref, kbuf, vbuf, semk, semv):
    b = pl.program_id(0)
    nb = pl.num_programs(0)
    n_pages = pt_ref.shape[1]
    page = k_hbm.shape[2]
    past = n_pages * page
    slot = lax.rem(b, 2)

    def copies(bb, sl):
        return (_paged_copies(pt_ref, bb, k_hbm, kbuf, semk, sl, n_pages, page)
                + _paged_copies(pt_ref, bb, v_hbm, vbuf, semv, sl, n_pages, page))

    @pl.when(b == 0)
    def _():
        for cp in copies(0, 0):
            cp.start()

    @pl.when(b + 1 < nb)
    def _():
        for cp in copies(b + 1, 1 - slot):
            cp.start()

    qm = qmbd_ref[0]
    pm, lm = _softmax_parts(_dot(qm, cmk_ref[0].astype(bf16)))
    mo = _dot_nt(pm.astype(bf16), cmv_ref[0].astype(bf16)) / lm
    for h in range(N_MEM_HEADS):
        hs = slice(h * HEAD_DIM, (h + 1) * HEAD_DIM)
        mo_ref[0, :, hs] = mo[h:h + 1, hs]

    for cp in copies(b, slot):
        cp.wait()

    qbd = qbd_ref[0]
    s_past = _dot(qbd, kbuf[slot].astype(bf16)) + bias_ref[0, :, 0:past]
    s_new = _dot_nt(qbd, knew_ref[...]) + bias_ref[0, :, past:past + nb]
    m = jnp.maximum(jnp.max(s_past, axis=1, keepdims=True), jnp.max(s_new, axis=1, keepdims=True))
    p_past = jnp.exp2(s_past - m)
    p_new = jnp.exp2(s_new - m)
    l = jnp.sum(p_past, axis=1, keepdims=True) + jnp.sum(p_new, axis=1, keepdims=True)
    o = _dot_nt(p_past.astype(bf16), vbuf[slot].astype(bf16)) + _dot(p_new.astype(bf16), vnew_ref[...])
    o = o / l
    for p in range(N_ATT_HEADS):
        g0 = (p % 2) * HEAD_DIM
        o_ref[0, :, p * HEAD_DIM:(p + 1) * HEAD_DIM] = o[p:p + 1, g0:g0 + HEAD_DIM]


def _sattn(pt, qbd, bias, knew, vnew, qmbd, cmk_t, cmv_t, k_t_pool, v_t_pool):
    nb, n_pages = pt.shape
    page = k_t_pool.shape[2]
    past = n_pages * page
    M = cmk_t.shape[2]
    per = lambda r, c: pl.BlockSpec((1, r, c), lambda b, pt: (b, 0, 0))
    full = lambda r, c: pl.BlockSpec((r, c), lambda b, pt: (0, 0))
    return pl.pallas_call(
        _sattn_kernel,
        grid_spec=pltpu.PrefetchScalarGridSpec(
            num_scalar_prefetch=1,
            grid=(nb,),
            in_specs=[per(N_ATT_HEADS, D_KV), per(1, past + nb), full(nb, D_KV), full(nb, D_KV),
                      per(N_MEM_HEADS, D_MEM), per(D_MEM, M), per(D_MEM, M),
                      pl.BlockSpec(memory_space=pl.ANY), pl.BlockSpec(memory_space=pl.ANY)],
            out_specs=[per(1, D_ATT), per(1, D_MEM)],
            scratch_shapes=[pltpu.VMEM((2, D_KV, past), f32), pltpu.VMEM((2, D_KV, past), f32),
                            pltpu.SemaphoreType.DMA((2,)), pltpu.SemaphoreType.DMA((2,))]),
        out_shape=[jax.ShapeDtypeStruct((nb, 1, D_ATT), f32), jax.ShapeDtypeStruct((nb, 1, D_MEM), f32)],
        compiler_params=pltpu.CompilerParams(dimension_semantics=("arbitrary",), vmem_limit_bytes=VMEM_LIMIT),
        name="sample_attention",
    )(pt, qbd, bias, knew, vnew, qmbd, cmk_t, cmv_t, k_t_pool, v_t_pool)


def _smerge_kernel(att_ref, ga_ref, cvg_ref, mo_ref, gm_ref, x_ref, wo_ref, gp_ref, y_ref):
    att_g = (att_ref[...] * ga_ref[...]).astype(bf16)
    mg = (mo_ref[...] * gm_ref[...]).astype(bf16)
    y_ref[...] = _merge(att_g, cvg_ref[...], mg, x_ref[...], wo_ref, gp_ref[...])


def _smerge(att, ga, cvg, mo, gm, x, wo, gp):
    return pl.pallas_call(
        _smerge_kernel,
        out_shape=jax.ShapeDtypeStruct(x.shape, f32),
        compiler_params=pltpu.CompilerParams(vmem_limit_bytes=VMEM_LIMIT),
        name="sample_merge",
    )(att, ga, cvg, mo, gm, x, wo, gp)


def _rope_tables(pos):
    half = HEAD_DIM // 2
    inv = ROPE_THETA ** (-jnp.arange(half, dtype=f32) / half)
    ang = pos.astype(f32)[:, None] * inv[None, :]
    cos, sin = jnp.cos(ang), jnp.sin(ang)
    zero = jnp.zeros_like(sin)
    rep = LANES // HEAD_DIM
    return (jnp.tile(jnp.concatenate([cos, cos], axis=1), (1, rep)),
            jnp.tile(jnp.concatenate([-sin, zero], axis=1), (1, rep)),
            jnp.tile(jnp.concatenate([zero, sin], axis=1), (1, rep)))


def _pack_w_in(w):
    head = lambda base: jnp.concatenate([w[:, base + h * HEAD_DIM:base + (h + 1) * HEAD_DIM] for h in ATT_ORDER], 1)
    kidx = w[:, R_KIDX:R_WIDX]
    packed = jnp.concatenate([head(R_Q), w[:, R_K:R_GATT], head(R_GATT), w[:, R_QIDX:R_KIDX], kidx, kidx,
                              w[:, R_U:]], axis=1)
    w_t = jnp.concatenate([w[:, R_V:R_GATT], w[:, R_WIDX:R_U], jnp.zeros((D_MODEL, D_T - D_KV - N_IDX_HEADS), w.dtype)],
                          axis=1).T
    return packed.astype(bf16), w_t.astype(bf16)


def _pack_w_out(w):
    att = jnp.concatenate([w[h * HEAD_DIM:(h + 1) * HEAD_DIM] for h in ATT_ORDER], axis=0)
    return jnp.concatenate([att, w[D_ATT:]], axis=0).astype(bf16)


def _slab_rows(x, n_rows):
    T = x.shape[0]
    slabs = jnp.repeat(x.reshape(T, n_rows // 2, 1, 2, HEAD_DIM), 2, axis=2)
    keep = jnp.arange(2)[:, None] == jnp.arange(2)[None, :]
    return jnp.where(keep[None, None, :, :, None], slabs, jnp.zeros((), x.dtype)).reshape(T, n_rows, LANES)


def _diag_rows(x, n_heads):
    T = x.shape[0]
    keep = jnp.arange(n_heads)[:, None] == jnp.arange(n_heads)[None, :]
    xh = x.reshape(T, 1, n_heads, HEAD_DIM)
    return jnp.where(keep[None, :, :, None], xh, jnp.zeros((), x.dtype)).reshape(T, n_heads, n_heads * HEAD_DIM)


def _layer(l, xp, xs, mem_prompt, cache_k, cache_v, cache_kidx, cache_mem_k, cache_mem_v, state_conv,
           page_table, g_pre, w_in, conv_w, g_memnorm, w_mem_kv, w_out, g_post):
    B, S, _ = xp.shape
    nb, T, _ = xs.shape
    assert T == 1
    n_pages = page_table.shape[1]
    n_pool, page = cache_k.shape[1], cache_k.shape[2]
    past = n_pages * page
    M = mem_prompt.shape[1]
    assert page == nb and past % 1024 == 0

    row = lambda a: a[l][None, :]
    w_pack, w_t = _pack_w_in(w_in[l])
    wo = _pack_w_out(w_out[l])
    gp = row(g_post)

    mk, mv, mkb, mvb = _memkv(mem_prompt, row(g_memnorm), w_mem_kv[l].astype(bf16))
    cos, sa, sb = _rope_tables(jnp.arange(S, dtype=jnp.int32))
    TQ = min(256, S)
    TM = min(512, S)
    (q, k, v, kb, vt, ga, qi, ki, kk, wit, cvg, cst, mg) = _front_prompt(
        xp, row(g_pre), w_pack, w_t, cos, sa, sb, conv_w[l], mkb, mvb, TM, TQ)
    yp = _attn_prompt(q, qi, wit, kb, vt, kk, ga, cvg, mg, xp, wo, gp, TQ)

    cos, sa, sb = _rope_tables(jnp.full((nb,), past, jnp.int32))
    st = state_conv[l]
    (q_s, k_s, v_s, kb_s, vt_s, ga_s, qi_s, ki_s, kk_s, wit_s, cvg_s, u_s, qm_s, gm_s) = _front_sample(
        xs.reshape(1, nb, D_MODEL), row(g_pre), w_pack, w_t, cos, sa, sb, conv_w[l], st[:, 0, :], st[:, 1, :])
    kidx_t = jnp.swapaxes(cache_kidx[l], 1, 2)
    k_t = jnp.transpose(cache_k[l], (0, 2, 3, 1)).reshape(n_pool, D_KV, page)
    v_t = jnp.transpose(cache_v[l], (0, 2, 3, 1)).reshape(n_pool, D_KV, page)
    cmk_t = jnp.transpose(cache_mem_k[l], (0, 2, 3, 1)).reshape(nb, D_MEM, M)
    cmv_t = jnp.transpose(cache_mem_v[l], (0, 2, 3, 1)).reshape(nb, D_MEM, M)

    sc, dn = _sidx(page_table, qi_s.reshape(nb, N_IDX_HEADS, IDX_DIM), wit_s[0].T.reshape(nb, N_IDX_HEADS, 1),
                   kk_s[0, :, :IDX_DIM], kidx_t)
    bias_t = _ssel(sc.reshape(nb, past).T, dn.reshape(nb, nb).T)
    att_s, mo_s = _sattn(page_table, _slab_rows(q_s[0], N_ATT_HEADS), bias_t.T.reshape(nb, 1, past + nb),
                         kb_s[0], vt_s[0, 0].T, _diag_rows(qm_s[0], N_MEM_HEADS), cmk_t, cmv_t, k_t, v_t)
    ys = _smerge(att_s.reshape(nb, D_ATT), ga_s[0], cvg_s[0], mo_s.reshape(nb, D_MEM), gm_s[0],
                 xs.reshape(nb, D_MODEL), wo, gp)

    outs_p = (k.reshape(B, S, N_KV_HEADS, HEAD_DIM), v.reshape(B, S, N_KV_HEADS, HEAD_DIM), ki, cst,
              mk.reshape(B, M, N_MEM_HEADS, HEAD_DIM), mv.reshape(B, M, N_MEM_HEADS, HEAD_DIM))
    outs_s = (k_s.reshape(nb, 1, N_KV_HEADS, HEAD_DIM), v_s.reshape(nb, 1, N_KV_HEADS, HEAD_DIM),
              ki_s.reshape(nb, 1, IDX_DIM), jnp.stack([st[:, 1, :], u_s[0]], axis=1))
    return yp, ys.reshape(nb, 1, D_MODEL), outs_p, outs_s


def kernel(x_prompt, x_sample, mem_prompt, cache_k, cache_v, cache_kidx, cache_mem_k, cache_mem_v, state_conv,
           page_table, g_pre, w_in, conv_w, g_memnorm, w_mem_kv, w_out, g_post):
    depth = w_in.shape[0]
    xp, xs = x_prompt, x_sample
    per_p, per_s = [], []
    for l in range(depth):
        xp, xs, op, os_ = _layer(l, xp, xs, mem_prompt, cache_k, cache_v, cache_kidx, cache_mem_k, cache_mem_v,
                                 state_conv, page_table, g_pre, w_in, conv_w, g_memnorm, w_mem_kv, w_out, g_post)
        per_p.append(op)
        per_s.append(os_)
    stack = lambda outs, i: jnp.stack([o[i] for o in outs])
    return ((xp, xs) + tuple(stack(per_p, i) for i in range(6)) + tuple(stack(per_s, i) for i in range(4)))
```

```python
import functools

import jax
import jax.numpy as jnp
from jax import lax
from jax.experimental import pallas as pl
from jax.experimental.pallas import tpu as pltpu

D_MODEL = 1024
HEAD_DIM = 64
N_MEM_HEADS = 4
D_MEM = N_MEM_HEADS * HEAD_DIM
D_ATT = 512
N_ATT_HEADS = 8
N_KV_HEADS = 2
Q_PER_KV = N_ATT_HEADS // N_KV_HEADS
D_KV = N_KV_HEADS * HEAD_DIM
D_CONV = 256
N_IDX_HEADS = 8
IDX_DIM = 64
D_QIDX = N_IDX_HEADS * IDX_DIM
TOPK_MAX = 256
CONV_WIDTH = 3
ROPE_THETA = 10000.0
RMS_EPS = 1e-6
LANES = 128
SUBLANES = 8

R_Q = 0
R_K = R_Q + D_ATT
R_V = R_K + D_KV
R_GATT = R_V + D_KV
R_QIDX = R_GATT + D_ATT
R_KIDX = R_QIDX + D_QIDX
R_WIDX = R_KIDX + IDX_DIM
R_U = R_WIDX + N_IDX_HEADS

ATT_ORDER = (0, 4, 1, 5, 2, 6, 3, 7)
P_Q = 0
P_K = P_Q + D_ATT
P_V = P_K + D_KV
P_GATT = P_V + D_KV
P_QIDX = P_GATT + D_ATT
P_KK = P_QIDX + D_QIDX
P_U = P_KK + LANES
P_B = P_U + D_CONV
P_C = P_B + D_CONV
P_GCONV = P_C + D_CONV
P_QMEM = P_GCONV + D_CONV
P_GMEM = P_QMEM + D_MEM
P_END = P_GMEM + D_MEM

QK_SCALE = HEAD_DIM ** -0.5
LOG2E = 1.4426950408889634
IDX_SCALE = IDX_DIM ** -0.5
W_SCALE = N_IDX_HEADS ** -0.5

F32_MAX = 3.4028234663852886e38
FIXED_PASSES = 16
KEY_PASS_EVERY = 8
MAX_PASSES = 8 * 34
NEG = -1e30
VMEM_LIMIT = 56 * 1024 * 1024
ACC_ROWS = 32
D_T = D_KV + 16

bf16 = jnp.bfloat16
f32 = jnp.float32
NT = (((1,), (1,)), ((), ()))


def _dot(a, b):
    return jnp.dot(a, b, preferred_element_type=f32)


def _dot_nt(a, b):
    return lax.dot_general(a, b, NT, preferred_element_type=f32)


def _rms(x, g):
    return x * lax.rsqrt(jnp.mean(x * x, axis=-1, keepdims=True) + RMS_EPS) * g


def _silu(x):
    return x / (1.0 + jnp.exp(-x))


def _rope(z, cos, sa, sb):
    return z * cos + pltpu.roll(z, 96, 1) * sa + pltpu.roll(z, 32, 1) * sb


def _key_to_f32(k):
    return lax.bitcast_convert_type(k ^ ((k >> 31) & jnp.int32(0x7FFFFFFF)), f32)


def _softmax_parts(s):
    m = jnp.max(s, axis=-1, keepdims=True)
    p = jnp.exp(s - m)
    return p, jnp.sum(p, axis=-1, keepdims=True)


def _f32_to_key(x):
    bits = lax.bitcast_convert_type(x, jnp.int32)
    return bits ^ ((bits >> 31) & jnp.int32(0x7FFFFFFF))


def _select_bias_t(sc_ref, bias_ref, nc, k_eff, n_valid):
    _, TK, Q = sc_ref.shape
    groups = -(-TOPK_MAX // TK)

    def count(cmp, thr):
        thr_b = jnp.broadcast_to(thr, (ACC_ROWS, Q))

        def body(c, acc):
            for j in range(TK // ACC_ROWS):
                acc = acc + jnp.where(cmp(sc_ref[c, j * ACC_ROWS:(j + 1) * ACC_ROWS, :], thr_b), 1.0, 0.0)
            return acc

        acc = lax.fori_loop(0, nc, body, jnp.zeros((ACC_ROWS, Q), f32))
        return jnp.sum(acc, axis=0, keepdims=True)

    ge = lambda a, b: a >= b

    if groups == 1:
        bias_ref[0] = sc_ref[0]

        def gmax_body(c, carry):
            bias_ref[0] = jnp.maximum(bias_ref[0], sc_ref[c])
            return carry

        lax.fori_loop(1, nc, gmax_body, 0)
    else:
        for c in range(nc):
            bias_ref[c % groups] = sc_ref[c] if c < groups else jnp.maximum(bias_ref[c % groups], sc_ref[c])
    gmaxs = [bias_ref[g] for g in range(groups)]
    low = jnp.min(functools.reduce(jnp.minimum, gmaxs), axis=0, keepdims=True)
    top = jnp.max(functools.reduce(jnp.maximum, gmaxs), axis=0, keepdims=True)

    low_b = jnp.broadcast_to(low, (ACC_ROWS, Q))

    def start_body(c, accs):
        a_low, a_ge0, a_gt0 = accs
        for j in range(TK // ACC_ROWS):
            x = sc_ref[c, j * ACC_ROWS:(j + 1) * ACC_ROWS, :]
            a_low = a_low + jnp.where(x >= low_b, 1.0, 0.0)
            a_ge0 = a_ge0 + jnp.where(x >= 0.0, 1.0, 0.0)
            a_gt0 = a_gt0 + jnp.where(x > 0.0, 1.0, 0.0)
        return a_low, a_ge0, a_gt0

    zeros = jnp.zeros((ACC_ROWS, Q), f32)
    c_low, c_ge0, c_gt0 = [jnp.sum(a, axis=0, keepdims=True)
                           for a in lax.fori_loop(0, nc, start_body, (zeros, zeros, zeros))]
    all_in = n_valid == k_eff
    at_zero = (c_gt0 < k_eff) & (c_ge0 >= k_eff)
    above = c_gt0 >= k_eff
    low_ok = low > -F32_MAX
    lo = jnp.where(all_in, -F32_MAX, jnp.where(
        at_zero, 0.0, jnp.where(above, jnp.maximum(low, 0.0), jnp.where(low_ok, low, -F32_MAX))))
    c_lo = jnp.where(all_in, n_valid, jnp.where(
        at_zero, c_ge0, jnp.where(above, jnp.where(low > 0.0, c_low, c_ge0), jnp.where(low_ok, c_low, n_valid))))
    hi = jnp.where(at_zero, _key_to_f32(jnp.ones((1, Q), jnp.int32)),
                   jnp.where(above, _key_to_f32(_f32_to_key(top) + 1), 0.0))
    c_hi = jnp.where(at_zero | above, 0.0, c_ge0)

    settled = all_in | at_zero
    halfway = lambda lo, hi: lo + (hi - lo) * 0.5

    def still_open(lo, hi, c_lo):
        mid = halfway(lo, hi)
        return jnp.where(jnp.logical_not(settled) & (c_lo != k_eff) & (mid > lo) & (mid < hi), 1.0, 0.0)

    def one_pass(it, state):
        lo, hi, c_lo, c_hi, w_lo, w_hi, moved = state
        active = still_open(lo, hi, c_lo) > 0.5
        k_lo, k_hi = _f32_to_key(lo), _f32_to_key(hi)
        a = (c_lo - k_eff + 0.5) * w_lo
        b = (k_eff - 0.5 - c_hi) * w_hi
        by_count = lo + (hi - lo) * (a / (a + b))
        by_key = _key_to_f32(k_lo + lax.shift_right_logical(k_hi - k_lo, 1))
        pivot = jnp.where(it % KEY_PASS_EVERY == KEY_PASS_EVERY - 1, by_key, by_count)
        pivot = jnp.where((pivot > lo) & (pivot < hi), pivot, halfway(lo, hi))
        pivot = jnp.where(active, pivot, lo)
        cnt = count(ge, pivot)
        up = active & (cnt >= k_eff)
        down = active & (cnt < k_eff)
        w_hi = jnp.where(up, jnp.where(moved > 0.5, w_hi * 0.5, 1.0), jnp.where(down, 1.0, w_hi))
        w_lo = jnp.where(down, jnp.where(moved < -0.5, w_lo * 0.5, 1.0), jnp.where(up, 1.0, w_lo))
        moved = jnp.where(up, 1.0, jnp.where(down, -1.0, moved))
        lo, c_lo = jnp.where(up, pivot, lo), jnp.where(up, cnt, c_lo)
        hi, c_hi = jnp.where(down, pivot, hi), jnp.where(down, cnt, c_hi)
        return lo, hi, c_lo, c_hi, w_lo, w_hi, moved

    ones = jnp.ones((1, Q), f32)
    state = lax.fori_loop(0, FIXED_PASSES, one_pass, (lo, hi, c_lo, c_hi, ones, ones, jnp.zeros((1, Q), f32)))
    any_open = lambda st: jnp.max(still_open(st[0], st[1], st[2]))

    def more_passes(carry):
        it, state, _ = carry
        state = one_pass(it + 1, one_pass(it, state))
        return it + 2, state, any_open(state)

    _, state, _ = lax.while_loop(lambda carry: (carry[2] > 0.5) & (carry[0] < MAX_PASSES), more_passes,
                                 (jnp.int32(FIXED_PASSES), state, any_open(state)))
    thr, cnt_u = state[0], state[2]
    has_tie = jnp.max(jnp.where(cnt_u != k_eff, 1.0, 0.0)) > 0.5

    @pl.when(jnp.logical_not(has_tie))
    def _():
        def bias_body(c, carry):
            bias_ref[c] = jnp.where(sc_ref[c] >= thr, 0.0, NEG)
            return carry

        lax.fori_loop(0, nc, bias_body, 0)

    @pl.when(has_tie)
    def _():
        need = k_eff - count(lambda a, b: a > b, thr)
        r_i = lax.broadcasted_iota(jnp.int32, (TK, TK), 0)
        c_i = lax.broadcasted_iota(jnp.int32, (TK, TK), 1)
        tri = jnp.where(c_i <= r_i, 1.0, 0.0).astype(bf16)

        def bias_body(c, run):
            x = sc_ref[c]
            eqf = jnp.where(x == thr, 1.0, 0.0)
            incl = _dot(tri, eqf.astype(bf16))
            tie_ok = jnp.where(run + incl - eqf < need, eqf, 0.0)
            bias_ref[c] = jnp.where(x > thr, 0.0, jnp.where(tie_ok > 0.5, 0.0, NEG))
            return run + incl[TK - 1:TK, :]

        lax.fori_loop(0, nc, bias_body, jnp.zeros((1, Q), f32))


def _merge(att_g, cvg, mg, x, wo_ref, gp):
    o = (_dot(att_g, wo_ref[0:D_ATT, :])
         + _dot(cvg, wo_ref[D_ATT:D_ATT + D_CONV, :])
         + _dot(mg, wo_ref[D_ATT + D_CONV:D_MODEL, :]))
    return x + _rms(o, gp)


def _memkv_kernel(mem_ref, g_ref, w_ref, mk_ref, mv_ref, mkb_ref, mvb_ref):
    kv = _dot(_rms(mem_ref[0], g_ref[...]).astype(bf16), w_ref[...])
    mk = kv[:, :D_MEM]
    mv = kv[:, D_MEM:]
    mk_ref[0] = mk
    mv_ref[0] = mv
    mkb_ref[0] = mk.astype(bf16)
    mvb_ref[0] = mv.astype(bf16)


def _memkv(mem, g, w):
    B, M, _ = mem.shape
    blk = lambda: pl.BlockSpec((1, M, D_MEM), lambda b: (b, 0, 0))
    return pl.pallas_call(
        _memkv_kernel,
        grid=(B,),
        in_specs=[pl.BlockSpec((1, M, D_MODEL), lambda b: (b, 0, 0)),
                  pl.BlockSpec((1, D_MODEL), lambda b: (0, 0)),
                  pl.BlockSpec((D_MODEL, 2 * D_MEM), lambda b: (0, 0))],
        out_specs=[blk(), blk(), blk(), blk()],
        out_shape=[jax.ShapeDtypeStruct((B, M, D_MEM), f32)] * 2
        + [jax.ShapeDtypeStruct((B, M, D_MEM), bf16)] * 2,
        compiler_params=pltpu.CompilerParams(dimension_semantics=("arbitrary",)),
        name="memkv",
    )(mem, g, w)


def _front_common(x_ref, g_ref, w_ref, wt_ref, cos_ref, sa_ref, sb_ref,
                  q_ref, k_ref, v_ref, kb_ref, vt_ref, ga_ref, qi_ref, ki_ref, kk_ref, wit_ref):
    xb = _rms(x_ref[0], g_ref[...]).astype(bf16)
    cos, sa, sb = cos_ref[...], sa_ref[...], sb_ref[...]
    proj = lambda lo, hi: _dot(xb, w_ref[:, lo:hi])
    rope = lambda z: _rope(z, cos, sa, sb)

    zq = proj(P_Q, P_K)
    zqi = proj(P_QIDX, P_KK)
    for s in range(D_ATT // LANES):
        sl = slice(s * LANES, (s + 1) * LANES)
        q_ref[0, :, sl] = (rope(zq[:, sl]) * (QK_SCALE * LOG2E)).astype(bf16)
        qi_ref[0, :, sl] = (rope(zqi[:, sl]) * IDX_SCALE).astype(bf16)
    zkv = proj(P_K, P_GATT)
    zk = rope(zkv[:, :D_KV])
    k_ref[0] = zk
    kb_ref[0] = zk.astype(bf16)
    v_ref[0] = zkv[:, D_KV:]
    ga_ref[0] = _silu(proj(P_GATT, P_QIDX))
    zkk = rope(proj(P_KK, P_U))
    ki_ref[0] = zkk[:, :IDX_DIM]
    kk_ref[0] = zkk.astype(bf16)
    vw_t = _dot_nt(wt_ref[...], xb)
    TK = vt_ref.shape[3]
    for c in range(vt_ref.shape[1]):
        vt_ref[0, c] = vw_t[0:D_KV, c * TK:(c + 1) * TK].astype(bf16)
    wit_ref[0] = vw_t[D_KV:D_KV + N_IDX_HEADS, :] * W_SCALE
    return proj


def _front_prompt_kernel(x_ref, g_ref, w_ref, wt_ref, cos_ref, sa_ref, sb_ref, cw_ref, mkb_ref, mvb_ref,
                         q_ref, k_ref, v_ref, kb_ref, vt_ref, ga_ref, qi_ref, ki_ref, kk_ref, wit_ref,
                         cvg_ref, cst_ref, mg_ref, ubuf):
    TM = x_ref.shape[1]
    proj = _front_common(x_ref, g_ref, w_ref, wt_ref, cos_ref, sa_ref, sb_ref,
                         q_ref, k_ref, v_ref, kb_ref, vt_ref, ga_ref, qi_ref, ki_ref, kk_ref, wit_ref)

    @pl.when(pl.program_id(1) == 0)
    def _():
        ubuf[0:8, :] = jnp.zeros((8, D_CONV), f32)

    u = proj(P_C, P_GCONV) * proj(P_U, P_B)
    ubuf[8:TM + 8, :] = u
    cw = cw_ref[...]
    conv = proj(P_B, P_C) * (cw[0:1] * ubuf[6:TM + 6, :] + cw[1:2] * ubuf[7:TM + 7, :] + cw[2:3] * u)
    cvg_ref[0] = (conv * _silu(proj(P_GCONV, P_QMEM))).astype(bf16)
    tail = u[TM - 2:TM, :]
    ubuf[6:8, :] = tail
    cst_ref[0] = tail

    qm = (proj(P_QMEM, P_GMEM) * QK_SCALE).astype(bf16)
    mkb = mkb_ref[0]
    mvb = mvb_ref[0]
    outs = []
    for h in range(N_MEM_HEADS):
        sl = slice(h * HEAD_DIM, (h + 1) * HEAD_DIM)
        p, l = _softmax_parts(_dot_nt(qm[:, sl], mkb[:, sl]))
        outs.append(_dot(p.astype(bf16), mvb[:, sl]) / l)
    mo = jnp.concatenate(outs, axis=1)
    mg_ref[0] = (mo * _silu(proj(P_GMEM, P_END))).astype(bf16)


def _front_sample_kernel(x_ref, g_ref, w_ref, wt_ref, cos_ref, sa_ref, sb_ref, cw_ref, st0_ref, st1_ref,
                         q_ref, k_ref, v_ref, kb_ref, vt_ref, ga_ref, qi_ref, ki_ref, kk_ref, wit_ref,
                         cvg_ref, u_ref, qm_ref, gm_ref):
    proj = _front_common(x_ref, g_ref, w_ref, wt_ref, cos_ref, sa_ref, sb_ref,
                         q_ref, k_ref, v_ref, kb_ref, vt_ref, ga_ref, qi_ref, ki_ref, kk_ref, wit_ref)
    u = proj(P_C, P_GCONV) * proj(P_U, P_B)
    cw = cw_ref[...]
    conv = proj(P_B, P_C) * (cw[0:1] * st0_ref[...] + cw[1:2] * st1_ref[...] + cw[2:3] * u)
    cvg_ref[0] = (conv * _silu(proj(P_GCONV, P_QMEM))).astype(bf16)
    u_ref[0] = u
    qm_ref[0] = (proj(P_QMEM, P_GMEM) * QK_SCALE).astype(bf16)
    gm_ref[0] = _silu(proj(P_GMEM, P_END))


def _front_out(B, S, TM, TK):
    tok = lambda d: pl.BlockSpec((1, TM, d), lambda b, j: (b, j, 0))
    specs = [tok(D_ATT), tok(D_KV), tok(D_KV), tok(D_KV),
             pl.BlockSpec((1, TM // TK, D_KV, TK), lambda b, j: (b, j, 0, 0)),
             tok(D_ATT), tok(D_QIDX), tok(IDX_DIM), tok(LANES),
             pl.BlockSpec((1, N_IDX_HEADS, TM), lambda b, j: (b, 0, j))]
    sds = lambda d, t: jax.ShapeDtypeStruct((B, S, d), t)
    shapes = [sds(D_ATT, bf16), sds(D_KV, f32), sds(D_KV, f32), sds(D_KV, bf16),
              jax.ShapeDtypeStruct((B, S // TK, D_KV, TK), bf16),
              sds(D_ATT, f32), sds(D_QIDX, bf16), sds(IDX_DIM, f32), sds(LANES, bf16),
              jax.ShapeDtypeStruct((B, N_IDX_HEADS, S), f32)]
    return tok, sds, specs, shapes


def _front_in(TM):
    full = lambda r, c: pl.BlockSpec((r, c), lambda b, j: (0, 0))
    rows = lambda: pl.BlockSpec((TM, LANES), lambda b, j: (j, 0))
    return [pl.BlockSpec((1, TM, D_MODEL), lambda b, j: (b, j, 0)), full(1, D_MODEL), full(D_MODEL, P_END),
            full(D_T, D_MODEL), rows(), rows(), rows(), full(CONV_WIDTH, D_CONV)]


def _front_prompt(x, g, w, wt, cos, sa, sb, cw, mkb, mvb, TM, TK):
    B, S, _ = x.shape
    M = mkb.shape[1]
    tok, sds, specs, shapes = _front_out(B, S, TM, TK)
    mem = lambda: pl.BlockSpec((1, M, D_MEM), lambda b, j: (b, 0, 0))
    return pl.pallas_call(
        _front_prompt_kernel,
        grid=(B, S // TM),
        in_specs=_front_in(TM) + [mem(), mem()],
        out_specs=specs + [tok(D_CONV), pl.BlockSpec((1, CONV_WIDTH - 1, D_CONV), lambda b, j: (b, 0, 0)),
                           tok(D_MEM)],
        out_shape=shapes + [sds(D_CONV, bf16), jax.ShapeDtypeStruct((B, CONV_WIDTH - 1, D_CONV), f32),
                            sds(D_MEM, bf16)],
        scratch_shapes=[pltpu.VMEM((TM + 8, D_CONV), f32)],
        compiler_params=pltpu.CompilerParams(dimension_semantics=("arbitrary", "arbitrary"),
                                             vmem_limit_bytes=VMEM_LIMIT),
        name="front_prompt",
    )(x, g, w, wt, cos, sa, sb, cw, mkb, mvb)


def _front_sample(x, g, w, wt, cos, sa, sb, cw, st0, st1):
    _, T, _ = x.shape
    tok, sds, specs, shapes = _front_out(1, T, T, T)
    st = lambda: pl.BlockSpec((T, D_CONV), lambda b, j: (0, 0))
    return pl.pallas_call(
        _front_sample_kernel,
        grid=(1, 1),
        in_specs=_front_in(T) + [st(), st()],
        out_specs=specs + [tok(D_CONV), tok(D_CONV), tok(D_MEM), tok(D_MEM)],
        out_shape=shapes + [sds(D_CONV, bf16), sds(D_CONV, f32), sds(D_MEM, bf16), sds(D_MEM, f32)],
        compiler_params=pltpu.CompilerParams(dimension_semantics=("arbitrary", "arbitrary"),
                                             vmem_limit_bytes=VMEM_LIMIT),
        name="front_sample",
    )(x, g, w, wt, cos, sa, sb, cw, st0, st1)


def _half_masked(slab, half):
    lane = lax.broadcasted_iota(jnp.int32, slab.shape, 1)
    keep = (lane < HEAD_DIM) if half == 0 else (lane >= HEAD_DIM)
    return jnp.where(keep, slab, jnp.zeros((), slab.dtype))


def _attn_prompt_kernel(q_ref, qi_ref, wit_ref, kb_ref, vt_ref, kk_ref, ga_ref, cvg_ref, mg_ref, x_ref,
                        wo_ref, gp_ref, y_ref, sc_ref, bias_ref, acc_ref):
    TQ = q_ref.shape[1]
    TK = TQ
    S = kb_ref.shape[1]
    i = pl.program_id(1)
    chunk = lambda c: pl.ds(pl.multiple_of(c * TK, TK), TK)
    slab = lambda ref, p: ref[0, :, (p // 2) * LANES:(p // 2 + 1) * LANES]

    blk = lambda a, p: a[:, p * TQ:(p + 1) * TQ]
    qi_all = jnp.concatenate([_half_masked(slab(qi_ref, h), h % 2) for h in range(N_IDX_HEADS)], axis=0)
    wit = wit_ref[0]

    def scores_t(c):
        d = _dot_nt(kk_ref[0, chunk(c), :], qi_all)
        acc = jnp.maximum(blk(d, 0), 0.0) * wit[0:1, :]
        for h in range(1, N_IDX_HEADS):
            acc = acc + jnp.maximum(blk(d, h), 0.0) * wit[h:h + 1, :]
        return acc

    def score_body(j, carry):
        for r in range(2):
            c = jnp.minimum(2 * j + r, i - 1)
            sc_ref[c] = scores_t(c)
        return carry

    lax.fori_loop(0, lax.shift_right_logical(i + 1, 1), score_body, 0)
    key_pos = lax.broadcasted_iota(jnp.int32, (TK, TQ), 0)
    qry_pos = lax.broadcasted_iota(jnp.int32, (TK, TQ), 1)
    sc_ref[i] = jnp.where(key_pos <= qry_pos, scores_t(i), -jnp.inf)

    t1 = i * TQ + 1 + lax.broadcasted_iota(jnp.int32, (1, TQ), 1)
    k_eff = jnp.minimum(t1, min(TOPK_MAX, S // 4)).astype(f32)
    _select_bias_t(sc_ref, bias_ref, i + 1, k_eff, t1.astype(f32))

    q_all = jnp.concatenate([_half_masked(slab(q_ref, p), p % 2) for p in range(N_ATT_HEADS)], axis=0)
    acc_ref[...] = jnp.zeros(acc_ref.shape, f32)

    def att_chunk(c_kv, c_bias, ms, ls):
        bias = bias_ref[c_bias]
        s_all = _dot_nt(kb_ref[0, chunk(c_kv), :], q_all)
        ms_new, ls_new, alphas, pts = [], [], [], []
        for p in range(N_ATT_HEADS):
            s = blk(s_all, p) + bias
            m_new = jnp.maximum(ms[p], jnp.max(s, axis=0, keepdims=True))
            alpha = jnp.exp2(ms[p] - m_new)
            pt = jnp.exp2(s - m_new)
            ls_new.append(alpha * ls[p] + jnp.sum(pt, axis=0, keepdims=True))
            ms_new.append(m_new)
            alphas.append(alpha)
            pts.append(pt.astype(bf16))
        pv_all = _dot(vt_ref[0, c_kv], jnp.concatenate(pts, axis=1))
        for p in range(N_ATT_HEADS):
            rows = slice(p * HEAD_DIM, (p + 1) * HEAD_DIM)
            g0 = (p % 2) * HEAD_DIM
            acc_ref[rows, :] = alphas[p] * acc_ref[rows, :] + blk(pv_all, p)[g0:g0 + HEAD_DIM, :]
        return tuple(ms_new), tuple(ls_new)

    def att_body(j, carry):
        return att_chunk(2 * j + 1, 2 * j + 1, *att_chunk(2 * j, 2 * j, *carry))

    init = (tuple(jnp.full((1, TQ), NEG, f32) for _ in range(N_ATT_HEADS)),
            tuple(jnp.zeros((1, TQ), f32) for _ in range(N_ATT_HEADS)))
    carry = lax.fori_loop(0, lax.shift_right_logical(i + 1, 1), att_body, init)
    _, ls = lax.cond((i + 1) % 2 == 1, lambda ms, ls: att_chunk(i, i, ms, ls), lambda ms, ls: (ms, ls), *carry)
    for p in range(N_ATT_HEADS):
        rows = slice(p * HEAD_DIM, (p + 1) * HEAD_DIM)
        acc_ref[rows, :] = acc_ref[rows, :] / ls[p]
    att_g = (acc_ref[...].T * ga_ref[0]).astype(bf16)
    y_ref[0] = _merge(att_g, cvg_ref[0], mg_ref[0], x_ref[0], wo_ref, gp_ref[...])


def _attn_prompt(q, qi, wit, kb, vt, kk, ga, cvg, mg, x, wo, gp, TQ):
    B, S, _ = x.shape
    NC = S // TQ
    tok = lambda d: pl.BlockSpec((1, TQ, d), lambda b, i: (b, i, 0))
    seq = lambda d: pl.BlockSpec((1, S, d), lambda b, i: (b, 0, 0))
    full = lambda r, c: pl.BlockSpec((r, c), lambda b, i: (0, 0))
    return pl.pallas_call(
        _attn_prompt_kernel,
        grid=(B, NC),
        in_specs=[tok(D_ATT), tok(D_QIDX), pl.BlockSpec((1, N_IDX_HEADS, TQ), lambda b, i: (b, 0, i)),
                  seq(D_KV), pl.BlockSpec((1, NC, D_KV, TQ), lambda b, i: (b, 0, 0, 0)), seq(LANES),
                  tok(D_ATT), tok(D_CONV), tok(D_MEM), tok(D_MODEL), full(D_MODEL, D_MODEL), full(1, D_MODEL)],
        out_specs=tok(D_MODEL),
        out_shape=jax.ShapeDtypeStruct((B, S, D_MODEL), f32),
        scratch_shapes=[pltpu.VMEM((NC, TQ, TQ), f32), pltpu.VMEM((NC, TQ, TQ), f32),
                        pltpu.VMEM((D_ATT, TQ), f32)],
        compiler_params=pltpu.CompilerParams(dimension_semantics=("arbitrary", "arbitrary"),
                                             vmem_limit_bytes=VMEM_LIMIT),
        name="attn_prompt",
    )(q, qi, wit, kb, vt, kk, ga, cvg, mg, x, wo, gp)


def _paged_copies(pt_ref, b, hbm_ref, buf, sem, slot, n_pages, page):
    return [pltpu.make_async_copy(hbm_ref.at[pt_ref[b, p]], buf.at[slot, :, pl.ds(p * page, page)], sem.at[slot])
            for p in range(n_pages)]


def _sidx_kernel(pt_ref, qi_ref, wi_ref, knew_ref, kidx_hbm, sc_ref, dn_ref, buf, sem):
    b = pl.program_id(0)
    nb = pl.num_programs(0)
    n_pages = pt_ref.shape[1]
    page = kidx_hbm.shape[2]
    copies = lambda bb, slot: _paged_copies(pt_ref, bb, kidx_hbm, buf, sem, slot, n_pages, page)
    slot = lax.rem(b, 2)

    @pl.when(b == 0)
    def _():
        for cp in copies(0, 0):
            cp.start()

    @pl.when(b + 1 < nb)
    def _():
        for cp in copies(b + 1, 1 - slot):
            cp.start()

    for cp in copies(b, slot):
        cp.wait()

    qi = qi_ref[0]
    wi = wi_ref[0]
    weigh = lambda d: jnp.sum(jnp.maximum(d, 0.0) * wi, axis=0, keepdims=True)
    sub = 1024
    for j in range(n_pages * page // sub):
        sc_ref[0, :, j * sub:(j + 1) * sub] = weigh(_dot(qi, buf[slot, :, j * sub:(j + 1) * sub].astype(bf16)))
    dn_ref[0] = weigh(_dot_nt(qi, knew_ref[...]))


def _sidx(pt, qi8, wi8, knew, kidx_t_pool):
    nb, n_pages = pt.shape
    page = kidx_t_pool.shape[2]
    past = n_pages * page
    return pl.pallas_call(
        _sidx_kernel,
        grid_spec=pltpu.PrefetchScalarGridSpec(
            num_scalar_prefetch=1,
            grid=(nb,),
            in_specs=[pl.BlockSpec((1, N_IDX_HEADS, IDX_DIM), lambda b, pt: (b, 0, 0)),
                      pl.BlockSpec((1, N_IDX_HEADS, 1), lambda b, pt: (b, 0, 0)),
                      pl.BlockSpec((nb, IDX_DIM), lambda b, pt: (0, 0)),
                      pl.BlockSpec(memory_space=pl.ANY)],
            out_specs=[pl.BlockSpec((1, 1, past), lambda b, pt: (b, 0, 0)),
                       pl.BlockSpec((1, 1, nb), lambda b, pt: (b, 0, 0))],
            scratch_shapes=[pltpu.VMEM((2, IDX_DIM, past), f32), pltpu.SemaphoreType.DMA((2,))]),
        out_shape=[jax.ShapeDtypeStruct((nb, 1, past), f32), jax.ShapeDtypeStruct((nb, 1, nb), f32)],
        compiler_params=pltpu.CompilerParams(dimension_semantics=("arbitrary",), vmem_limit_bytes=VMEM_LIMIT),
        name="sample_indexer",
    )(pt, qi8, wi8, knew, kidx_t_pool)


def _ssel_kernel(sct_ref, dn_ref, out_ref, sc_ref, bias_ref):
    nc, TK, Q = sc_ref.shape
    for c in range(nc - 1):
        sc_ref[c] = sct_ref[c * TK:(c + 1) * TK, :]
    row = lax.broadcasted_iota(jnp.int32, (TK, Q), 0)
    col = lax.broadcasted_iota(jnp.int32, (TK, Q), 1)
    sc_ref[nc - 1] = jnp.where(row == col, dn_ref[...], -jnp.inf)
    n_valid = (nc - 1) * TK + 1
    k_eff = jnp.full((1, Q), float(min(TOPK_MAX, n_valid // 4)), f32)
    _select_bias_t(sc_ref, bias_ref, nc, k_eff, jnp.full((1, Q), float(n_valid), f32))
    for c in range(nc):
        out_ref[c * TK:(c + 1) * TK, :] = bias_ref[c]


def _ssel(sc_t, dn_t):
    past, nb = sc_t.shape
    nc = past // nb + 1
    return pl.pallas_call(
        _ssel_kernel,
        out_shape=jax.ShapeDtypeStruct((past + nb, nb), f32),
        scratch_shapes=[pltpu.VMEM((nc, nb, nb), f32), pltpu.VMEM((nc, nb, nb), f32)],
        compiler_params=pltpu.CompilerParams(vmem_limit_bytes=VMEM_LIMIT),
        name="sample_select",
    )(sc_t, dn_t)


def _sattn_kernel(pt_ref, qbd_ref, bias_ref, knew_ref, vnew_ref, qmbd_ref, cmk_ref, cmv_ref, k_hbm, v_hbm,
                  o_ref, mo_ref, kbuf, vbuf, semk, semv):
    b = pl.program_id(0)
    nb = pl.num_programs(0)
    n_pages = pt_ref.shape[1]
    page = k_hbm.shape[2]
    past = n_pages * page
    slot = lax.rem(b, 2)

    def copies(bb, sl):
        return (_paged_copies(pt_ref, bb, k_hbm, kbuf, semk, sl, n_pages, page)
                + _paged_copies(pt_ref, bb, v_hbm, vbuf, semv, sl, n_pages, page))

    @pl.when(b == 0)
    def _():
        for cp in copies(0, 0):
            cp.start()

    @pl.when(b + 1 < nb)
    def _():
        for cp in copies(b + 1, 1 - slot):
            cp.start()

    qm = qmbd_ref[0]
    pm, lm = _softmax_parts(_dot(qm, cmk_ref[0].astype(bf16)))
    mo = _dot_nt(pm.astype(bf16), cmv_ref[0].astype(bf16)) / lm
    for h in range(N_MEM_HEADS):
        hs = slice(h * HEAD_DIM, (h + 1) * HEAD_DIM)
        mo_ref[0, :, hs] = mo[h:h + 1, hs]

    for cp in copies(b, slot):
        cp.wait()

    qbd = qbd_ref[0]
    s_past = _dot(qbd, kbuf[slot].astype(bf16)) + bias_ref[0, :, 0:past]
    s_new = _dot_nt(qbd, knew_ref[...]) + bias_ref[0, :, past:past + nb]
    m = jnp.maximum(jnp.max(s_past, axis=1, keepdims=True), jnp.max(s_new, axis=1, keepdims=True))
    p_past = jnp.exp2(s_past - m)
    p_new = jnp.exp2(s_new - m)
    l = jnp.sum(p_past, axis=1, keepdims=True) + jnp.sum(p_new, axis=1, keepdims=True)
    o = _dot_nt(p_past.astype(bf16), vbuf[slot].astype(bf16)) + _dot(p_new.astype(bf16), vnew_ref[...])
    o = o / l
    for p in range(N_ATT_HEADS):
        g0 = (p % 2) * HEAD_DIM
        o_ref[0, :, p * HEAD_DIM:(p + 1) * HEAD_DIM] = o[p:p + 1, g0:g0 + HEAD_DIM]


def _sattn(pt, qbd, bias, knew, vnew, qmbd, cmk_t, cmv_t, k_t_pool, v_t_pool):
    nb, n_pages = pt.shape
    page = k_t_pool.shape[2]
    past = n_pages * page
    M = cmk_t.shape[2]
    per = lambda r, c: pl.BlockSpec((1, r, c), lambda b, pt: (b, 0, 0))
    full = lambda r, c: pl.BlockSpec((r, c), lambda b, pt: (0, 0))
    return pl.pallas_call(
        _sattn_kernel,
        grid_spec=pltpu.PrefetchScalarGridSpec(
            num_scalar_prefetch=1,
            grid=(nb,),
            in_specs=[per(N_ATT_HEADS, D_KV), per(1, past + nb), full(nb, D_KV), full(nb, D_KV),
                      per(N_MEM_HEADS, D_MEM), per(D_MEM, M), per(D_MEM, M),
                      pl.BlockSpec(memory_space=pl.ANY), pl.BlockSpec(memory_space=pl.ANY)],
            out_specs=[per(1, D_ATT), per(1, D_MEM)],
            scratch_shapes=[pltpu.VMEM((2, D_KV, past), f32), pltpu.VMEM((2, D_KV, past), f32),
                            pltpu.SemaphoreType.DMA((2,)), pltpu.SemaphoreType.DMA((2,))]),
        out_shape=[jax.ShapeDtypeStruct((nb, 1, D_ATT), f32), jax.ShapeDtypeStruct((nb, 1, D_MEM), f32)],
        compiler_params=pltpu.CompilerParams(dimension_semantics=("arbitrary",), vmem_limit_bytes=VMEM_LIMIT),
        name="sample_attention",
    )(pt, qbd, bias, knew, vnew, qmbd, cmk_t, cmv_t, k_t_pool, v_t_pool)


def _smerge_kernel(att_ref, ga_ref, cvg_ref, mo_ref, gm_ref, x_ref, wo_ref, gp_ref, y_ref):
    att_g = (att_ref[...] * ga_ref[...]).astype(bf16)
    mg = (mo_ref[...] * gm_ref[...]).astype(bf16)
    y_ref[...] = _merge(att_g, cvg_ref[...], mg, x_ref[...], wo_ref, gp_ref[...])


def _smerge(att, ga, cvg, mo, gm, x, wo, gp):
    return pl.pallas_call(
        _smerge_kernel,
        out_shape=jax.ShapeDtypeStruct(x.shape, f32),
        compiler_params=pltpu.CompilerParams(vmem_limit_bytes=VMEM_LIMIT),
        name="sample_merge",
    )(att, ga, cvg, mo, gm, x, wo, gp)


def _rope_tables(pos):
    half = HEAD_DIM // 2
    inv = ROPE_THETA ** (-jnp.arange(half, dtype=f32) / half)
    ang = pos.astype(f32)[:, None] * inv[None, :]
    cos, sin = jnp.cos(ang), jnp.sin(ang)
    zero = jnp.zeros_like(sin)
    rep = LANES // HEAD_DIM
    return (jnp.tile(jnp.concatenate([cos, cos], axis=1), (1, rep)),
            jnp.tile(jnp.concatenate([-sin, zero], axis=1), (1, rep)),
            jnp.tile(jnp.concatenate([zero, sin], axis=1), (1, rep)))


def _pack_w_in(w):
    head = lambda base: jnp.concatenate([w[:, base + h * HEAD_DIM:base + (h + 1) * HEAD_DIM] for h in ATT_ORDER], 1)
    kidx = w[:, R_KIDX:R_WIDX]
    packed = jnp.concatenate([head(R_Q), w[:, R_K:R_GATT], head(R_GATT), w[:, R_QIDX:R_KIDX], kidx, kidx,
                              w[:, R_U:]], axis=1)
    w_t = jnp.concatenate([w[:, R_V:R_GATT], w[:, R_WIDX:R_U], jnp.zeros((D_MODEL, D_T - D_KV - N_IDX_HEADS), w.dtype)],
                          axis=1).T
    return packed.astype(bf16), w_t.astype(bf16)


def _pack_w_out(w):
    att = jnp.concatenate([w[h * HEAD_DIM:(h + 1) * HEAD_DIM] for h in ATT_ORDER], axis=0)
    return jnp.concatenate([att, w[D_ATT:]], axis=0).astype(bf16)


def _slab_rows(x, n_rows):
    T = x.shape[0]
    slabs = jnp.repeat(x.reshape(T, n_rows // 2, 1, 2, HEAD_DIM), 2, axis=2)
    keep = jnp.arange(2)[:, None] == jnp.arange(2)[None, :]
    return jnp.where(keep[None, None, :, :, None], slabs, jnp.zeros((), x.dtype)).reshape(T, n_rows, LANES)


def _diag_rows(x, n_heads):
    T = x.shape[0]
    keep = jnp.arange(n_heads)[:, None] == jnp.arange(n_heads)[None, :]
    xh = x.reshape(T, 1, n_heads, HEAD_DIM)
    return jnp.where(keep[None, :, :, None], xh, jnp.zeros((), x.dtype)).reshape(T, n_heads, n_heads * HEAD_DIM)


def _layer(l, xp, xs, mem_prompt, cache_k, cache_v, cache_kidx, cache_mem_k, cache_mem_v, state_conv,
           page_table, g_pre, w_in, conv_w, g_memnorm, w_mem_kv, w_out, g_post):
    B, S, _ = xp.shape
    nb, T, _ = xs.shape
    assert T == 1
    n_pages = page_table.shape[1]
    n_pool, page = cache_k.shape[1], cache_k.shape[2]
    past = n_pages * page
    M = mem_prompt.shape[1]
    assert page == nb and past % 1024 == 0

    row = lambda a: a[l][None, :]
    w_pack, w_t = _pack_w_in(w_in[l])
    wo = _pack_w_out(w_out[l])
    gp = row(g_post)

    mk, mv, mkb, mvb = _memkv(mem_prompt, row(g_memnorm), w_mem_kv[l].astype(bf16))
    cos, sa, sb = _rope_tables(jnp.arange(S, dtype=jnp.int32))
    TQ = min(256, S)
    TM = min(512, S)
    (q, k, v, kb, vt, ga, qi, ki, kk, wit, cvg, cst, mg) = _front_prompt(
        xp, row(g_pre), w_pack, w_t, cos, sa, sb, conv_w[l], mkb, mvb, TM, TQ)
    yp = _attn_prompt(q, qi, wit, kb, vt, kk, ga, cvg, mg, xp, wo, gp, TQ)

    cos, sa, sb = _rope_tables(jnp.full((nb,), past, jnp.int32))
    st = state_conv[l]
    (q_s, k_s, v_s, kb_s, vt_s, ga_s, qi_s, ki_s, kk_s, wit_s, cvg_s, u_s, qm_s, gm_s) = _front_sample(
        xs.reshape(1, nb, D_MODEL), row(g_pre), w_pack, w_t, cos, sa, sb, conv_w[l], st[:, 0, :], st[:, 1, :])
    kidx_t = jnp.swapaxes(cache_kidx[l], 1, 2)
    k_t = jnp.transpose(cache_k[l], (0, 2, 3, 1)).reshape(n_pool, D_KV, page)
    v_t = jnp.transpose(cache_v[l], (0, 2, 3, 1)).reshape(n_pool, D_KV, page)
    cmk_t = jnp.transpose(cache_mem_k[l], (0, 2, 3, 1)).reshape(nb, D_MEM, M)
    cmv_t = jnp.transpose(cache_mem_v[l], (0, 2, 3, 1)).reshape(nb, D_MEM, M)

    sc, dn = _sidx(page_table, qi_s.reshape(nb, N_IDX_HEADS, IDX_DIM), wit_s[0].T.reshape(nb, N_IDX_HEADS, 1),
                   kk_s[0, :, :IDX_DIM], kidx_t)
    bias_t = _ssel(sc.reshape(nb, past).T, dn.reshape(nb, nb).T)
    att_s, mo_s = _sattn(page_table, _slab_rows(q_s[0], N_ATT_HEADS), bias_t.T.reshape(nb, 1, past + nb),
                         kb_s[0], vt_s[0, 0].T, _diag_rows(qm_s[0], N_MEM_HEADS), cmk_t, cmv_t, k_t, v_t)
    ys = _smerge(att_s.reshape(nb, D_ATT), ga_s[0], cvg_s[0], mo_s.reshape(nb, D_MEM), gm_s[0],
                 xs.reshape(nb, D_MODEL), wo, gp)

    outs_p = (k.reshape(B, S, N_KV_HEADS, HEAD_DIM), v.reshape(B, S, N_KV_HEADS, HEAD_DIM), ki, cst,
              mk.reshape(B, M, N_MEM_HEADS, HEAD_DIM), mv.reshape(B, M, N_MEM_HEADS, HEAD_DIM))
    outs_s = (k_s.reshape(nb, 1, N_KV_HEADS, HEAD_DIM), v_s.reshape(nb, 1, N_KV_HEADS, HEAD_DIM),
              ki_s.reshape(nb, 1, IDX_DIM), jnp.stack([st[:, 1, :], u_s[0]], axis=1))
    return yp, ys.reshape(nb, 1, D_MODEL), outs_p, outs_s


def kernel(x_prompt, x_sample, mem_prompt, cache_k, cache_v, cache_kidx, cache_mem_k, cache_mem_v, state_conv,
           page_table, g_pre, w_in, conv_w, g_memnorm, w_mem_kv, w_out, g_post):
    depth = w_in.shape[0]
    xp, xs = x_prompt, x_sample
    per_p, per_s = [], []
    for l in range(depth):
        xp, xs, op, os_ = _layer(l, xp, xs, mem_prompt, cache_k, cache_v, cache_kidx, cache_mem_k, cache_mem_v,
                                 state_conv, page_table, g_pre, w_in, conv_w, g_memnorm, w_mem_kv, w_out, g_post)
        per_p.append(op)
        per_s.append(os_)
    stack = lambda outs, i: jnp.stack([o[i] for o in outs])
    return ((xp, xs) + tuple(stack(per_p, i) for i in range(6)) + tuple(stack(per_s, i) for i in range(4)))
```

```python
import functools

import jax
import jax.numpy as jnp
from jax import lax
from jax.experimental import pallas as pl
from jax.experimental.pallas import tpu as pltpu

D_MODEL = 1024
HEAD_DIM = 64
N_MEM_HEADS = 4
D_MEM = N_MEM_HEADS * HEAD_DIM
D_ATT = 512
N_ATT_HEADS = 8
N_KV_HEADS = 2
Q_PER_KV = N_ATT_HEADS // N_KV_HEADS
D_KV = N_KV_HEADS * HEAD_DIM
D_CONV = 256
N_IDX_HEADS = 8
IDX_DIM = 64
D_QIDX = N_IDX_HEADS * IDX_DIM
TOPK_MAX = 256
CONV_WIDTH = 3
ROPE_THETA = 10000.0
RMS_EPS = 1e-6
LANES = 128
SUBLANES = 8

R_Q = 0
R_K = R_Q + D_ATT
R_V = R_K + D_KV
R_GATT = R_V + D_KV
R_QIDX = R_GATT + D_ATT
R_KIDX = R_QIDX + D_QIDX
R_WIDX = R_KIDX + IDX_DIM
R_U = R_WIDX + N_IDX_HEADS

ATT_ORDER = (0, 4, 1, 5, 2, 6, 3, 7)
P_Q = 0
P_K = P_Q + D_ATT
P_V = P_K + D_KV
P_GATT = P_V + D_KV
P_QIDX = P_GATT + D_ATT
P_KK = P_QIDX + D_QIDX
P_U = P_KK + LANES
P_B = P_U + D_CONV
P_C = P_B + D_CONV
P_GCONV = P_C + D_CONV
P_QMEM = P_GCONV + D_CONV
P_GMEM = P_QMEM + D_MEM
P_END = P_GMEM + D_MEM

QK_SCALE = HEAD_DIM ** -0.5
LOG2E = 1.4426950408889634
IDX_SCALE = IDX_DIM ** -0.5
W_SCALE = N_IDX_HEADS ** -0.5

F32_MAX = 3.4028234663852886e38
SAMPLE_SEG = 2048
FIXED_PASSES = 16
KEY_PASS_EVERY = 8
MAX_PASSES = 8 * 34
NEG = -1e30
VMEM_LIMIT = 56 * 1024 * 1024
ACC_ROWS = 32
D_T = D_KV + 16

bf16 = jnp.bfloat16
f32 = jnp.float32
NT = (((1,), (1,)), ((), ()))


def _dot(a, b):
    return jnp.dot(a, b, preferred_element_type=f32)


def _dot_nt(a, b):
    return lax.dot_general(a, b, NT, preferred_element_type=f32)


def _rms(x, g):
    return x * lax.rsqrt(jnp.mean(x * x, axis=-1, keepdims=True) + RMS_EPS) * g


def _silu(x):
    return x / (1.0 + jnp.exp(-x))


def _rope(z, cos, sa, sb):
    return z * cos + pltpu.roll(z, 96, 1) * sa + pltpu.roll(z, 32, 1) * sb


def _key_to_f32(k):
    return lax.bitcast_convert_type(k ^ ((k >> 31) & jnp.int32(0x7FFFFFFF)), f32)


def _softmax_parts(s):
    m = jnp.max(s, axis=-1, keepdims=True)
    p = jnp.exp(s - m)
    return p, jnp.sum(p, axis=-1, keepdims=True)


def _f32_to_key(x):
    bits = lax.bitcast_convert_type(x, jnp.int32)
    return bits ^ ((bits >> 31) & jnp.int32(0x7FFFFFFF))


def _select_bias_t(sc_ref, bias_ref, nc, k_eff, n_valid):
    _, TK, Q = sc_ref.shape
    groups = -(-TOPK_MAX // TK)

    def count(cmp, thr):
        thr_b = jnp.broadcast_to(thr, (ACC_ROWS, Q))

        def body(c, acc):
            for j in range(TK // ACC_ROWS):
                acc = acc + jnp.where(cmp(sc_ref[c, j * ACC_ROWS:(j + 1) * ACC_ROWS, :], thr_b), 1.0, 0.0)
            return acc

        acc = lax.fori_loop(0, nc, body, jnp.zeros((ACC_ROWS, Q), f32))
        return jnp.sum(acc, axis=0, keepdims=True)

    ge = lambda a, b: a >= b

    if groups == 1:
        bias_ref[0] = sc_ref[0]

        def gmax_body(c, carry):
            bias_ref[0] = jnp.maximum(bias_ref[0], sc_ref[c])
            return carry

        lax.fori_loop(1, nc, gmax_body, 0)
    else:
        for c in range(nc):
            bias_ref[c % groups] = sc_ref[c] if c < groups else jnp.maximum(bias_ref[c % groups], sc_ref[c])
    gmaxs = [bias_ref[g] for g in range(groups)]
    low = jnp.min(functools.reduce(jnp.minimum, gmaxs), axis=0, keepdims=True)
    top = jnp.max(functools.reduce(jnp.maximum, gmaxs), axis=0, keepdims=True)

    low_b = jnp.broadcast_to(low, (ACC_ROWS, Q))

    def start_body(c, accs):
        a_low, a_ge0, a_gt0 = accs
        for j in range(TK // ACC_ROWS):
            x = sc_ref[c, j * ACC_ROWS:(j + 1) * ACC_ROWS, :]
            a_low = a_low + jnp.where(x >= low_b, 1.0, 0.0)
            a_ge0 = a_ge0 + jnp.where(x >= 0.0, 1.0, 0.0)
            a_gt0 = a_gt0 + jnp.where(x > 0.0, 1.0, 0.0)
        return a_low, a_ge0, a_gt0

    zeros = jnp.zeros((ACC_ROWS, Q), f32)
    c_low, c_ge0, c_gt0 = [jnp.sum(a, axis=0, keepdims=True)
                           for a in lax.fori_loop(0, nc, start_body, (zeros, zeros, zeros))]
    all_in = n_valid == k_eff
    at_zero = (c_gt0 < k_eff) & (c_ge0 >= k_eff)
    above = c_gt0 >= k_eff
    low_ok = low > -F32_MAX
    lo = jnp.where(all_in, -F32_MAX, jnp.where(
        at_zero, 0.0, jnp.where(above, jnp.maximum(low, 0.0), jnp.where(low_ok, low, -F32_MAX))))
    c_lo = jnp.where(all_in, n_valid, jnp.where(
        at_zero, c_ge0, jnp.where(above, jnp.where(low > 0.0, c_low, c_ge0), jnp.where(low_ok, c_low, n_valid))))
    hi = jnp.where(at_zero, _key_to_f32(jnp.ones((1, Q), jnp.int32)),
                   jnp.where(above, _key_to_f32(_f32_to_key(top) + 1), 0.0))
    c_hi = jnp.where(at_zero | above, 0.0, c_ge0)

    settled = all_in | at_zero
    halfway = lambda lo, hi: lo + (hi - lo) * 0.5

    def still_open(lo, hi, c_lo):
        mid = halfway(lo, hi)
        return jnp.where(jnp.logical_not(settled) & (c_lo != k_eff) & (mid > lo) & (mid < hi), 1.0, 0.0)

    def one_pass(it, state):
        lo, hi, c_lo, c_hi, w_lo, w_hi, moved = state
        active = still_open(lo, hi, c_lo) > 0.5
        k_lo, k_hi = _f32_to_key(lo), _f32_to_key(hi)
        a = (c_lo - k_eff + 0.5) * w_lo
        b = (k_eff - 0.5 - c_hi) * w_hi
        by_count = lo + (hi - lo) * (a / (a + b))
        by_key = _key_to_f32(k_lo + lax.shift_right_logical(k_hi - k_lo, 1))
        pivot = jnp.where(it % KEY_PASS_EVERY == KEY_PASS_EVERY - 1, by_key, by_count)
        pivot = jnp.where((pivot > lo) & (pivot < hi), pivot, halfway(lo, hi))
        pivot = jnp.where(active, pivot, lo)
        cnt = count(ge, pivot)
        up = active & (cnt >= k_eff)
        down = active & (cnt < k_eff)
        w_hi = jnp.where(up, jnp.where(moved > 0.5, w_hi * 0.5, 1.0), jnp.where(down, 1.0, w_hi))
        w_lo = jnp.where(down, jnp.where(moved < -0.5, w_lo * 0.5, 1.0), jnp.where(up, 1.0, w_lo))
        moved = jnp.where(up, 1.0, jnp.where(down, -1.0, moved))
        lo, c_lo = jnp.where(up, pivot, lo), jnp.where(up, cnt, c_lo)
        hi, c_hi = jnp.where(down, pivot, hi), jnp.where(down, cnt, c_hi)
        return lo, hi, c_lo, c_hi, w_lo, w_hi, moved

    ones = jnp.ones((1, Q), f32)
    state = lax.fori_loop(0, FIXED_PASSES, one_pass, (lo, hi, c_lo, c_hi, ones, ones, jnp.zeros((1, Q), f32)))
    any_open = lambda st: jnp.max(still_open(st[0], st[1], st[2]))

    def more_passes(carry):
        it, state, _ = carry
        state = one_pass(it + 1, one_pass(it, state))
        return it + 2, state, any_open(state)

    _, state, _ = lax.while_loop(lambda carry: (carry[2] > 0.5) & (carry[0] < MAX_PASSES), more_passes,
                                 (jnp.int32(FIXED_PASSES), state, any_open(state)))
    thr, cnt_u = state[0], state[2]
    has_tie = jnp.max(jnp.where(cnt_u != k_eff, 1.0, 0.0)) > 0.5

    @pl.when(jnp.logical_not(has_tie))
    def _():
        def bias_body(c, carry):
            bias_ref[c] = jnp.where(sc_ref[c] >= thr, 0.0, NEG)
            return carry

        lax.fori_loop(0, nc, bias_body, 0)

    @pl.when(has_tie)
    def _():
        need = k_eff - count(lambda a, b: a > b, thr)
        r_i = lax.broadcasted_iota(jnp.int32, (TK, TK), 0)
        c_i = lax.broadcasted_iota(jnp.int32, (TK, TK), 1)
        tri = jnp.where(c_i <= r_i, 1.0, 0.0).astype(bf16)

        def bias_body(c, run):
            x = sc_ref[c]
            eqf = jnp.where(x == thr, 1.0, 0.0)
            incl = _dot(tri, eqf.astype(bf16))
            tie_ok = jnp.where(run + incl - eqf < need, eqf, 0.0)
            bias_ref[c] = jnp.where(x > thr, 0.0, jnp.where(tie_ok > 0.5, 0.0, NEG))
            return run + incl[TK - 1:TK, :]

        lax.fori_loop(0, nc, bias_body, jnp.zeros((1, Q), f32))


def _merge(att_g, cvg, mg, x, wo_ref, gp):
    o = (_dot(att_g, wo_ref[0:D_ATT, :])
         + _dot(cvg, wo_ref[D_ATT:D_ATT + D_CONV, :])
         + _dot(mg, wo_ref[D_ATT + D_CONV:D_MODEL, :]))
    return x + _rms(o, gp)


def _memkv_kernel(mem_ref, g_ref, w_ref, mk_ref, mv_ref, mkb_ref, mvb_ref):
    kv = _dot(_rms(mem_ref[0], g_ref[...]).astype(bf16), w_ref[...])
    mk = kv[:, :D_MEM]
    mv = kv[:, D_MEM:]
    mk_ref[0] = mk
    mv_ref[0] = mv
    mkb_ref[0] = mk.astype(bf16)
    mvb_ref[0] = mv.astype(bf16)


def _memkv(mem, g, w):
    B, M, _ = mem.shape
    blk = lambda: pl.BlockSpec((1, M, D_MEM), lambda b: (b, 0, 0))
    return pl.pallas_call(
        _memkv_kernel,
        grid=(B,),
        in_specs=[pl.BlockSpec((1, M, D_MODEL), lambda b: (b, 0, 0)),
                  pl.BlockSpec((1, D_MODEL), lambda b: (0, 0)),
                  pl.BlockSpec((D_MODEL, 2 * D_MEM), lambda b: (0, 0))],
        out_specs=[blk(), blk(), blk(), blk()],
        out_shape=[jax.ShapeDtypeStruct((B, M, D_MEM), f32)] * 2
        + [jax.ShapeDtypeStruct((B, M, D_MEM), bf16)] * 2,
        compiler_params=pltpu.CompilerParams(dimension_semantics=("arbitrary",)),
        name="memkv",
    )(mem, g, w)


def _front_common(x_ref, g_ref, w_ref, wt_ref, cos_ref, sa_ref, sb_ref,
                  q_ref, k_ref, v_ref, kb_ref, vt_ref, ga_ref, qi_ref, ki_ref, kk_ref, wit_ref):
    xb = _rms(x_ref[0], g_ref[...]).astype(bf16)
    cos, sa, sb = cos_ref[...], sa_ref[...], sb_ref[...]
    proj = lambda lo, hi: _dot(xb, w_ref[:, lo:hi])
    rope = lambda z: _rope(z, cos, sa, sb)

    zq = proj(P_Q, P_K)
    zqi = proj(P_QIDX, P_KK)
    for s in range(D_ATT // LANES):
        sl = slice(s * LANES, (s + 1) * LANES)
        q_ref[0, :, sl] = (rope(zq[:, sl]) * (QK_SCALE * LOG2E)).astype(bf16)
        qi_ref[0, :, sl] = (rope(zqi[:, sl]) * IDX_SCALE).astype(bf16)
    zkv = proj(P_K, P_GATT)
    zk = rope(zkv[:, :D_KV])
    k_ref[0] = zk
    kb_ref[0] = zk.astype(bf16)
    v_ref[0] = zkv[:, D_KV:]
    ga_ref[0] = _silu(proj(P_GATT, P_QIDX))
    zkk = rope(proj(P_KK, P_U))
    ki_ref[0] = zkk[:, :IDX_DIM]
    kk_ref[0] = zkk.astype(bf16)
    vw_t = _dot_nt(wt_ref[...], xb)
    TK = vt_ref.shape[3]
    for c in range(vt_ref.shape[1]):
        vt_ref[0, c] = vw_t[0:D_KV, c * TK:(c + 1) * TK].astype(bf16)
    wit_ref[0] = vw_t[D_KV:D_KV + N_IDX_HEADS, :] * W_SCALE
    return proj


def _front_prompt_kernel(x_ref, g_ref, w_ref, wt_ref, cos_ref, sa_ref, sb_ref, cw_ref, mkb_ref, mvb_ref,
                         q_ref, k_ref, v_ref, kb_ref, vt_ref, ga_ref, qi_ref, ki_ref, kk_ref, wit_ref,
                         cvg_ref, cst_ref, mg_ref, ubuf):
    TM = x_ref.shape[1]
    proj = _front_common(x_ref, g_ref, w_ref, wt_ref, cos_ref, sa_ref, sb_ref,
                         q_ref, k_ref, v_ref, kb_ref, vt_ref, ga_ref, qi_ref, ki_ref, kk_ref, wit_ref)

    @pl.when(pl.program_id(1) == 0)
    def _():
        ubuf[0:8, :] = jnp.zeros((8, D_CONV), f32)

    u = proj(P_C, P_GCONV) * proj(P_U, P_B)
    ubuf[8:TM + 8, :] = u
    cw = cw_ref[...]
    conv = proj(P_B, P_C) * (cw[0:1] * ubuf[6:TM + 6, :] + cw[1:2] * ubuf[7:TM + 7, :] + cw[2:3] * u)
    cvg_ref[0] = (conv * _silu(proj(P_GCONV, P_QMEM))).astype(bf16)
    tail = u[TM - 2:TM, :]
    ubuf[6:8, :] = tail
    cst_ref[0] = tail

    qm = (proj(P_QMEM, P_GMEM) * QK_SCALE).astype(bf16)
    mkb = mkb_ref[0]
    mvb = mvb_ref[0]
    outs = []
    for h in range(N_MEM_HEADS):
        sl = slice(h * HEAD_DIM, (h + 1) * HEAD_DIM)
        p, l = _softmax_parts(_dot_nt(qm[:, sl], mkb[:, sl]))
        outs.append(_dot(p.astype(bf16), mvb[:, sl]) / l)
    mo = jnp.concatenate(outs, axis=1)
    mg_ref[0] = (mo * _silu(proj(P_GMEM, P_END))).astype(bf16)


def _front_sample_kernel(x_ref, g_ref, w_ref, wt_ref, cos_ref, sa_ref, sb_ref, cw_ref, st0_ref, st1_ref,
                         q_ref, k_ref, v_ref, kb_ref, vt_ref, ga_ref, qi_ref, ki_ref, kk_ref, wit_ref,
                         cvg_ref, u_ref, qm_ref, gm_ref):
    proj = _front_common(x_ref, g_ref, w_ref, wt_ref, cos_ref, sa_ref, sb_ref,
                         q_ref, k_ref, v_ref, kb_ref, vt_ref, ga_ref, qi_ref, ki_ref, kk_ref, wit_ref)
    u = proj(P_C, P_GCONV) * proj(P_U, P_B)
    cw = cw_ref[...]
    conv = proj(P_B, P_C) * (cw[0:1] * st0_ref[...] + cw[1:2] * st1_ref[...] + cw[2:3] * u)
    cvg_ref[0] = (conv * _silu(proj(P_GCONV, P_QMEM))).astype(bf16)
    u_ref[0] = u
    qm_ref[0] = (proj(P_QMEM, P_GMEM) * QK_SCALE).astype(bf16)
    gm_ref[0] = _silu(proj(P_GMEM, P_END))


def _front_out(B, S, TM, TK):
    tok = lambda d: pl.BlockSpec((1, TM, d), lambda b, j: (b, j, 0))
    specs = [tok(D_ATT), tok(D_KV), tok(D_KV), tok(D_KV),
             pl.BlockSpec((1, TM // TK, D_KV, TK), lambda b, j: (b, j, 0, 0)),
             tok(D_ATT), tok(D_QIDX), tok(IDX_DIM), tok(LANES),
             pl.BlockSpec((1, N_IDX_HEADS, TM), lambda b, j: (b, 0, j))]
    sds = lambda d, t: jax.ShapeDtypeStruct((B, S, d), t)
    shapes = [sds(D_ATT, bf16), sds(D_KV, f32), sds(D_KV, f32), sds(D_KV, bf16),
              jax.ShapeDtypeStruct((B, S // TK, D_KV, TK), bf16),
              sds(D_ATT, f32), sds(D_QIDX, bf16), sds(IDX_DIM, f32), sds(LANES, bf16),
              jax.ShapeDtypeStruct((B, N_IDX_HEADS, S), f32)]
    return tok, sds, specs, shapes


def _front_in(TM):
    full = lambda r, c: pl.BlockSpec((r, c), lambda b, j: (0, 0))
    rows = lambda: pl.BlockSpec((TM, LANES), lambda b, j: (j, 0))
    return [pl.BlockSpec((1, TM, D_MODEL), lambda b, j: (b, j, 0)), full(1, D_MODEL), full(D_MODEL, P_END),
            full(D_T, D_MODEL), rows(), rows(), rows(), full(CONV_WIDTH, D_CONV)]


def _front_prompt(x, g, w, wt, cos, sa, sb, cw, mkb, mvb, TM, TK):
    B, S, _ = x.shape
    M = mkb.shape[1]
    tok, sds, specs, shapes = _front_out(B, S, TM, TK)
    mem = lambda: pl.BlockSpec((1, M, D_MEM), lambda b, j: (b, 0, 0))
    return pl.pallas_call(
        _front_prompt_kernel,
        grid=(B, S // TM),
        in_specs=_front_in(TM) + [mem(), mem()],
        out_specs=specs + [tok(D_CONV), pl.BlockSpec((1, CONV_WIDTH - 1, D_CONV), lambda b, j: (b, 0, 0)),
                           tok(D_MEM)],
        out_shape=shapes + [sds(D_CONV, bf16), jax.ShapeDtypeStruct((B, CONV_WIDTH - 1, D_CONV), f32),
                            sds(D_MEM, bf16)],
        scratch_shapes=[pltpu.VMEM((TM + 8, D_CONV), f32)],
        compiler_params=pltpu.CompilerParams(dimension_semantics=("arbitrary", "arbitrary"),
                                             vmem_limit_bytes=VMEM_LIMIT),
        name="front_prompt",
    )(x, g, w, wt, cos, sa, sb, cw, mkb, mvb)


def _front_sample(x, g, w, wt, cos, sa, sb, cw, st0, st1):
    _, T, _ = x.shape
    tok, sds, specs, shapes = _front_out(1, T, T, T)
    st = lambda: pl.BlockSpec((T, D_CONV), lambda b, j: (0, 0))
    return pl.pallas_call(
        _front_sample_kernel,
        grid=(1, 1),
        in_specs=_front_in(T) + [st(), st()],
        out_specs=specs + [tok(D_CONV), tok(D_CONV), tok(D_MEM), tok(D_MEM)],
        out_shape=shapes + [sds(D_CONV, bf16), sds(D_CONV, f32), sds(D_MEM, bf16), sds(D_MEM, f32)],
        compiler_params=pltpu.CompilerParams(dimension_semantics=("arbitrary", "arbitrary"),
                                             vmem_limit_bytes=VMEM_LIMIT),
        name="front_sample",
    )(x, g, w, wt, cos, sa, sb, cw, st0, st1)


def _half_masked(slab, half):
    lane = lax.broadcasted_iota(jnp.int32, slab.shape, 1)
    keep = (lane < HEAD_DIM) if half == 0 else (lane >= HEAD_DIM)
    return jnp.where(keep, slab, jnp.zeros((), slab.dtype))


def _attn_prompt_body(q_ref, qi_ref, wit_ref, kb_ref, vt_ref, kk_ref, ga_ref, cvg_ref, mg_ref, x_ref,
                      wo_ref, gp_ref, y_ref, sc_ref, bias_ref, acc_ref):
    TQ = q_ref.shape[1]
    TK = TQ
    S = kb_ref.shape[1]
    i = pl.program_id(1)
    chunk = lambda c: pl.ds(pl.multiple_of(c * TK, TK), TK)
    slab = lambda ref, p: ref[0, :, (p // 2) * LANES:(p // 2 + 1) * LANES]

    blk = lambda a, p: a[:, p * TQ:(p + 1) * TQ]
    qi_all = jnp.concatenate([_half_masked(slab(qi_ref, h), h % 2) for h in range(N_IDX_HEADS)], axis=0)
    wit = wit_ref[0]

    def scores_t(c):
        d = _dot_nt(kk_ref[0, chunk(c), :], qi_all)
        acc = jnp.maximum(blk(d, 0), 0.0) * wit[0:1, :]
        for h in range(1, N_IDX_HEADS):
            acc = acc + jnp.maximum(blk(d, h), 0.0) * wit[h:h + 1, :]
        return acc

    def score_body(j, carry):
        for r in range(2):
            c = jnp.minimum(2 * j + r, i - 1)
            sc_ref[c] = scores_t(c)
        return carry

    lax.fori_loop(0, lax.shift_right_logical(i + 1, 1), score_body, 0)
    key_pos = lax.broadcasted_iota(jnp.int32, (TK, TQ), 0)
    qry_pos = lax.broadcasted_iota(jnp.int32, (TK, TQ), 1)
    sc_ref[i] = jnp.where(key_pos <= qry_pos, scores_t(i), -jnp.inf)

    t1 = i * TQ + 1 + lax.broadcasted_iota(jnp.int32, (1, TQ), 1)
    k_eff = jnp.minimum(t1, min(TOPK_MAX, S // 4)).astype(f32)
    _select_bias_t(sc_ref, bias_ref, i + 1, k_eff, t1.astype(f32))

    q_all = jnp.concatenate([_half_masked(slab(q_ref, p), p % 2) for p in range(N_ATT_HEADS)], axis=0)
    acc_ref[...] = jnp.zeros(acc_ref.shape, f32)

    def att_chunk(c_kv, c_bias, ms, ls):
        bias = bias_ref[c_bias]
        s_all = _dot_nt(kb_ref[0, chunk(c_kv), :], q_all)
        ms_new, ls_new, alphas, pts = [], [], [], []
        for p in range(N_ATT_HEADS):
            s = blk(s_all, p) + bias
            m_new = jnp.maximum(ms[p], jnp.max(s, axis=0, keepdims=True))
            alpha = jnp.exp2(ms[p] - m_new)
            pt = jnp.exp2(s - m_new)
            ls_new.append(alpha * ls[p] + jnp.sum(pt, axis=0, keepdims=True))
            ms_new.append(m_new)
            alphas.append(alpha)
            pts.append(pt.astype(bf16))
        pv_all = _dot(vt_ref[0, c_kv], jnp.concatenate(pts, axis=1))
        for p in range(N_ATT_HEADS):
            rows = slice(p * HEAD_DIM, (p + 1) * HEAD_DIM)
            g0 = (p % 2) * HEAD_DIM
            acc_ref[rows, :] = alphas[p] * acc_ref[rows, :] + blk(pv_all, p)[g0:g0 + HEAD_DIM, :]
        return tuple(ms_new), tuple(ls_new)

    def att_body(j, carry):
        return att_chunk(2 * j + 1, 2 * j + 1, *att_chunk(2 * j, 2 * j, *carry))

    init = (tuple(jnp.full((1, TQ), NEG, f32) for _ in range(N_ATT_HEADS)),
            tuple(jnp.zeros((1, TQ), f32) for _ in range(N_ATT_HEADS)))
    carry = lax.fori_loop(0, lax.shift_right_logical(i + 1, 1), att_body, init)
    _, ls = lax.cond((i + 1) % 2 == 1, lambda ms, ls: att_chunk(i, i, ms, ls), lambda ms, ls: (ms, ls), *carry)
    for p in range(N_ATT_HEADS):
        rows = slice(p * HEAD_DIM, (p + 1) * HEAD_DIM)
        acc_ref[rows, :] = acc_ref[rows, :] / ls[p]
    att_g = (acc_ref[...].T * ga_ref[0]).astype(bf16)
    y_ref[0] = _merge(att_g, cvg_ref[0], mg_ref[0], x_ref[0], wo_ref, gp_ref[...])


N_PROMPT_IN = 12


def _attn_prompt_kernel(*refs):
    _attn_prompt_body(*refs)


def _attn_fused_kernel(pt_ref, *refs):
    prompt_in, refs = refs[:N_PROMPT_IN], refs[N_PROMPT_IN:]
    qbd_ref, sbias_ref, knew_ref, vnew_ref, qmbd_ref, cmk_ref, cmv_ref, k_hbm, v_hbm = refs[:9]
    y_ref, o_ref, mo_ref, sc_ref, bias_ref, acc_ref, kbuf, vbuf, semk, semv = refs[9:]
    s = pl.program_id(0) * pl.num_programs(1) + pl.program_id(1)
    ns = pl.num_programs(0) * pl.num_programs(1)
    _sample_prefetch(pt_ref, s, ns, k_hbm, v_hbm, kbuf, vbuf, semk, semv)
    _attn_prompt_body(*prompt_in, y_ref, sc_ref, bias_ref, acc_ref)
    _sample_attend(pt_ref, s, qbd_ref, sbias_ref, knew_ref, vnew_ref, qmbd_ref, cmk_ref, cmv_ref, k_hbm, v_hbm,
                   o_ref, mo_ref, kbuf, vbuf, semk, semv)


def _attn_prompt(q, qi, wit, kb, vt, kk, ga, cvg, mg, x, wo, gp, TQ, sample=None):
    B, S, _ = x.shape
    NC = S // TQ
    imap = lambda f: (lambda b, i, *_: f(b, i))
    tok = lambda d: pl.BlockSpec((1, TQ, d), imap(lambda b, i: (b, i, 0)))
    seq = lambda d: pl.BlockSpec((1, S, d), imap(lambda b, i: (b, 0, 0)))
    full = lambda r, c: pl.BlockSpec((r, c), imap(lambda b, i: (0, 0)))
    in_specs = [tok(D_ATT), tok(D_QIDX), pl.BlockSpec((1, N_IDX_HEADS, TQ), imap(lambda b, i: (b, 0, i))),
                seq(D_KV), pl.BlockSpec((1, NC, D_KV, TQ), imap(lambda b, i: (b, 0, 0, 0))), seq(LANES),
                tok(D_ATT), tok(D_CONV), tok(D_MEM), tok(D_MODEL), full(D_MODEL, D_MODEL), full(1, D_MODEL)]
    assert len(in_specs) == N_PROMPT_IN
    scratch = [pltpu.VMEM((NC, TQ, TQ), f32), pltpu.VMEM((NC, TQ, TQ), f32), pltpu.VMEM((D_ATT, TQ), f32)]
    params = pltpu.CompilerParams(dimension_semantics=("arbitrary", "arbitrary"), vmem_limit_bytes=VMEM_LIMIT)
    y_shape = jax.ShapeDtypeStruct((B, S, D_MODEL), f32)
    if sample is None:
        return pl.pallas_call(
            _attn_prompt_kernel, grid=(B, NC), in_specs=in_specs, out_specs=tok(D_MODEL), out_shape=y_shape,
            scratch_shapes=scratch, compiler_params=params, name="attn_prompt",
        )(q, qi, wit, kb, vt, kk, ga, cvg, mg, x, wo, gp)
    pt, *sample_ops = sample
    s_in, s_out, s_shapes, s_scratch = _sattn_specs(pt, *sample_ops, index=lambda b, i: b * NC + i)
    return pl.pallas_call(
        _attn_fused_kernel,
        grid_spec=pltpu.PrefetchScalarGridSpec(
            num_scalar_prefetch=1, grid=(B, NC), in_specs=in_specs + s_in, out_specs=[tok(D_MODEL)] + s_out,
            scratch_shapes=scratch + s_scratch),
        out_shape=[y_shape] + s_shapes, compiler_params=params, name="attn_prompt",
    )(pt, q, qi, wit, kb, vt, kk, ga, cvg, mg, x, wo, gp, *sample_ops)


def _paged_copies(pt_ref, b, hbm_ref, buf, sem, slot, n_pages, page):
    return [pltpu.make_async_copy(hbm_ref.at[pt_ref[b, p]], buf.at[slot, :, pl.ds(p * page, page)], sem.at[slot])
            for p in range(n_pages)]


def _sidx_kernel(pt_ref, qi_ref, wi_ref, knew_ref, kidx_hbm, sc_ref, dn_ref, buf, sem):
    b = pl.program_id(0)
    nb = pl.num_programs(0)
    n_pages = pt_ref.shape[1]
    page = kidx_hbm.shape[2]
    copies = lambda bb, slot: _paged_copies(pt_ref, bb, kidx_hbm, buf, sem, slot, n_pages, page)
    slot = lax.rem(b, 2)

    @pl.when(b == 0)
    def _():
        for p, cp in enumerate(copies(0, 0)):
            cp.start(priority=p % 2)

    @pl.when(b + 1 < nb)
    def _():
        for p, cp in enumerate(copies(b + 1, 1 - slot)):
            cp.start(priority=p % 2)

    for cp in copies(b, slot):
        cp.wait()

    qi = qi_ref[0]
    wi = wi_ref[0]
    weigh = lambda d: jnp.sum(jnp.maximum(d, 0.0) * wi, axis=0, keepdims=True)
    sub = 1024
    for j in range(n_pages * page // sub):
        sc_ref[0, :, j * sub:(j + 1) * sub] = weigh(_dot(qi, buf[slot, :, j * sub:(j + 1) * sub].astype(bf16)))
    dn_ref[0] = weigh(_dot_nt(qi, knew_ref[...]))


def _sidx(pt, qi8, wi8, knew, kidx_t_pool):
    nb, n_pages = pt.shape
    page = kidx_t_pool.shape[2]
    past = n_pages * page
    return pl.pallas_call(
        _sidx_kernel,
        grid_spec=pltpu.PrefetchScalarGridSpec(
            num_scalar_prefetch=1,
            grid=(nb,),
            in_specs=[pl.BlockSpec((1, N_IDX_HEADS, IDX_DIM), lambda b, pt: (b, 0, 0)),
                      pl.BlockSpec((1, N_IDX_HEADS, 1), lambda b, pt: (b, 0, 0)),
                      pl.BlockSpec((nb, IDX_DIM), lambda b, pt: (0, 0)),
                      pl.BlockSpec(memory_space=pl.ANY)],
            out_specs=[pl.BlockSpec((1, 1, past), lambda b, pt: (b, 0, 0)),
                       pl.BlockSpec((1, 1, nb), lambda b, pt: (b, 0, 0))],
            scratch_shapes=[pltpu.VMEM((2, IDX_DIM, past), f32), pltpu.SemaphoreType.DMA((2,))]),
        out_shape=[jax.ShapeDtypeStruct((nb, 1, past), f32), jax.ShapeDtypeStruct((nb, 1, nb), f32)],
        compiler_params=pltpu.CompilerParams(dimension_semantics=("arbitrary",), vmem_limit_bytes=VMEM_LIMIT),
        name="sample_indexer",
    )(pt, qi8, wi8, knew, kidx_t_pool)


def _ssel_kernel(sct_ref, dn_ref, out_ref, sc_ref, bias_ref):
    nc, TK, Q = sc_ref.shape
    for c in range(nc - 1):
        sc_ref[c] = sct_ref[c * TK:(c + 1) * TK, :]
    row = lax.broadcasted_iota(jnp.int32, (TK, Q), 0)
    col = lax.broadcasted_iota(jnp.int32, (TK, Q), 1)
    sc_ref[nc - 1] = jnp.where(row == col, dn_ref[...], -jnp.inf)
    n_valid = (nc - 1) * TK + 1
    k_eff = jnp.full((1, Q), float(min(TOPK_MAX, n_valid // 4)), f32)
    _select_bias_t(sc_ref, bias_ref, nc, k_eff, jnp.full((1, Q), float(n_valid), f32))
    for c in range(nc):
        out_ref[c * TK:(c + 1) * TK, :] = bias_ref[c]


def _ssel(sc_t, dn_t):
    past, nb = sc_t.shape
    nc = past // nb + 1
    return pl.pallas_call(
        _ssel_kernel,
        out_shape=jax.ShapeDtypeStruct((past + nb, nb), f32),
        scratch_shapes=[pltpu.VMEM((nc, nb, nb), f32), pltpu.VMEM((nc, nb, nb), f32)],
        compiler_params=pltpu.CompilerParams(vmem_limit_bytes=VMEM_LIMIT),
        name="sample_select",
    )(sc_t, dn_t)


def _sample_copies(pt_ref, s, slot, k_hbm, v_hbm, kbuf, vbuf, semk, semv):
    n_pages, page = pt_ref.shape[1], k_hbm.shape[2]
    return (_paged_copies(pt_ref, s, k_hbm, kbuf, semk, slot, n_pages, page)
            + _paged_copies(pt_ref, s, v_hbm, vbuf, semv, slot, n_pages, page))


def _sample_prefetch(pt_ref, s, ns, k_hbm, v_hbm, kbuf, vbuf, semk, semv):
    slot = lax.rem(s, 2)

    @pl.when(s == 0)
    def _():
        for cp in _sample_copies(pt_ref, 0, 0, k_hbm, v_hbm, kbuf, vbuf, semk, semv):
            cp.start()

    @pl.when(s + 1 < ns)
    def _():
        for cp in _sample_copies(pt_ref, s + 1, 1 - slot, k_hbm, v_hbm, kbuf, vbuf, semk, semv):
            cp.start()


def _sample_attend(pt_ref, s, qbd_ref, bias_ref, knew_ref, vnew_ref, qmbd_ref, cmk_ref, cmv_ref, k_hbm, v_hbm,
                   o_ref, mo_ref, kbuf, vbuf, semk, semv):
    past = kbuf.shape[2]
    nb = knew_ref.shape[0]
    slot = lax.rem(s, 2)

    qm = qmbd_ref[0]
    pm, lm = _softmax_parts(_dot(qm, cmk_ref[0].astype(bf16)))
    mo = _dot_nt(pm.astype(bf16), cmv_ref[0].astype(bf16)) / lm
    for h in range(N_MEM_HEADS):
        hs = slice(h * HEAD_DIM, (h + 1) * HEAD_DIM)
        mo_ref[0, :, hs] = mo[h:h + 1, hs]

    for cp in _sample_copies(pt_ref, s, slot, k_hbm, v_hbm, kbuf, vbuf, semk, semv):
        cp.wait()

    qbd = qbd_ref[0]

    def softmax_part(logits, value_product):
        m_j = jnp.max(logits, axis=1, keepdims=True)
        p_j = jnp.exp2(logits - m_j)
        return m_j, jnp.sum(p_j, axis=1, keepdims=True), value_product(p_j.astype(bf16))

    seg = min(SAMPLE_SEG, past)
    parts = []
    for j in range(past // seg):
        ks = slice(j * seg, (j + 1) * seg)
        parts.append(softmax_part(_dot(qbd, kbuf[slot, :, ks].astype(bf16)) + bias_ref[0, :, ks],
                                  lambda p_j, ks=ks: _dot_nt(p_j, vbuf[slot, :, ks].astype(bf16))))
    parts.append(softmax_part(_dot_nt(qbd, knew_ref[...]) + bias_ref[0, :, past:past + nb],
                              lambda p_j: _dot(p_j, vnew_ref[...])))
    m = functools.reduce(jnp.maximum, [m_j for m_j, _, _ in parts])
    scales = [jnp.exp2(m_j - m) for m_j, _, _ in parts]
    l = sum(l_j * c for (_, l_j, _), c in zip(parts, scales))
    o = sum(o_j * c for (_, _, o_j), c in zip(parts, scales)) / l
    for p in range(N_ATT_HEADS):
        g0 = (p % 2) * HEAD_DIM
        o_ref[0, :, p * HEAD_DIM:(p + 1) * HEAD_DIM] = o[p:p + 1, g0:g0 + HEAD_DIM]


def _sattn_kernel(pt_ref, qbd_ref, bias_ref, knew_ref, vnew_ref, qmbd_ref, cmk_ref, cmv_ref, k_hbm, v_hbm,
                  o_ref, mo_ref, kbuf, vbuf, semk, semv):
    s = pl.program_id(0)
    _sample_prefetch(pt_ref, s, pl.num_programs(0), k_hbm, v_hbm, kbuf, vbuf, semk, semv)
    _sample_attend(pt_ref, s, qbd_ref, bias_ref, knew_ref, vnew_ref, qmbd_ref, cmk_ref, cmv_ref, k_hbm, v_hbm,
                   o_ref, mo_ref, kbuf, vbuf, semk, semv)


def _sattn_specs(pt, qbd, bias, knew, vnew, qmbd, cmk_t, cmv_t, k_t_pool, v_t_pool, index):
    nb, n_pages = pt.shape
    past = n_pages * k_t_pool.shape[2]
    M = cmk_t.shape[2]
    per = lambda r, c: pl.BlockSpec((1, r, c), lambda *g: (index(*g[:-1]), 0, 0))
    full = lambda r, c: pl.BlockSpec((r, c), lambda *g: (0, 0))
    in_specs = [per(N_ATT_HEADS, D_KV), per(1, past + nb), full(nb, D_KV), full(nb, D_KV),
                per(N_MEM_HEADS, D_MEM), per(D_MEM, M), per(D_MEM, M),
                pl.BlockSpec(memory_space=pl.ANY), pl.BlockSpec(memory_space=pl.ANY)]
    out_specs = [per(1, D_ATT), per(1, D_MEM)]
    out_shapes = [jax.ShapeDtypeStruct((nb, 1, D_ATT), f32), jax.ShapeDtypeStruct((nb, 1, D_MEM), f32)]
    scratch = [pltpu.VMEM((2, D_KV, past), f32), pltpu.VMEM((2, D_KV, past), f32),
               pltpu.SemaphoreType.DMA((2,)), pltpu.SemaphoreType.DMA((2,))]
    return in_specs, out_specs, out_shapes, scratch


def _sattn(pt, *ops):
    in_specs, out_specs, out_shapes, scratch = _sattn_specs(pt, *ops, index=lambda b: b)
    return pl.pallas_call(
        _sattn_kernel,
        grid_spec=pltpu.PrefetchScalarGridSpec(num_scalar_prefetch=1, grid=(pt.shape[0],), in_specs=in_specs,
                                               out_specs=out_specs, scratch_shapes=scratch),
        out_shape=out_shapes,
        compiler_params=pltpu.CompilerParams(dimension_semantics=("arbitrary",), vmem_limit_bytes=VMEM_LIMIT),
        name="sample_attention",
    )(pt, *ops)


def _smerge_kernel(att_ref, ga_ref, cvg_ref, mo_ref, gm_ref, x_ref, wo_ref, gp_ref, y_ref):
    att_g = (att_ref[...] * ga_ref[...]).astype(bf16)
    mg = (mo_ref[...] * gm_ref[...]).astype(bf16)
    y_ref[...] = _merge(att_g, cvg_ref[...], mg, x_ref[...], wo_ref, gp_ref[...])


def _smerge(att, ga, cvg, mo, gm, x, wo, gp):
    return pl.pallas_call(
        _smerge_kernel,
        out_shape=jax.ShapeDtypeStruct(x.shape, f32),
        compiler_params=pltpu.CompilerParams(vmem_limit_bytes=VMEM_LIMIT),
        name="sample_merge",
    )(att, ga, cvg, mo, gm, x, wo, gp)


def _rope_tables(pos):
    half = HEAD_DIM // 2
    inv = ROPE_THETA ** (-jnp.arange(half, dtype=f32) / half)
    ang = pos.astype(f32)[:, None] * inv[None, :]
    cos, sin = jnp.cos(ang), jnp.sin(ang)
    zero = jnp.zeros_like(sin)
    rep = LANES // HEAD_DIM
    return (jnp.tile(jnp.concatenate([cos, cos], axis=1), (1, rep)),
            jnp.tile(jnp.concatenate([-sin, zero], axis=1), (1, rep)),
            jnp.tile(jnp.concatenate([zero, sin], axis=1), (1, rep)))


def _pack_w_in(w):
    head = lambda base: jnp.concatenate([w[:, base + h * HEAD_DIM:base + (h + 1) * HEAD_DIM] for h in ATT_ORDER], 1)
    kidx = w[:, R_KIDX:R_WIDX]
    packed = jnp.concatenate([head(R_Q), w[:, R_K:R_GATT], head(R_GATT), w[:, R_QIDX:R_KIDX], kidx, kidx,
                              w[:, R_U:]], axis=1)
    w_t = jnp.concatenate([w[:, R_V:R_GATT], w[:, R_WIDX:R_U], jnp.zeros((D_MODEL, D_T - D_KV - N_IDX_HEADS), w.dtype)],
                          axis=1).T
    return packed.astype(bf16), w_t.astype(bf16)


def _pack_w_out(w):
    att = jnp.concatenate([w[h * HEAD_DIM:(h + 1) * HEAD_DIM] for h in ATT_ORDER], axis=0)
    return jnp.concatenate([att, w[D_ATT:]], axis=0).astype(bf16)


def _slab_rows(x, n_rows):
    T = x.shape[0]
    slabs = jnp.repeat(x.reshape(T, n_rows // 2, 1, 2, HEAD_DIM), 2, axis=2)
    keep = jnp.arange(2)[:, None] == jnp.arange(2)[None, :]
    return jnp.where(keep[None, None, :, :, None], slabs, jnp.zeros((), x.dtype)).reshape(T, n_rows, LANES)


def _diag_rows(x, n_heads):
    T = x.shape[0]
    keep = jnp.arange(n_heads)[:, None] == jnp.arange(n_heads)[None, :]
    xh = x.reshape(T, 1, n_heads, HEAD_DIM)
    return jnp.where(keep[None, :, :, None], xh, jnp.zeros((), x.dtype)).reshape(T, n_heads, n_heads * HEAD_DIM)


def _layer(l, xp, xs, mem_prompt, cache_k, cache_v, cache_kidx, cache_mem_k, cache_mem_v, state_conv,
           page_table, g_pre, w_in, conv_w, g_memnorm, w_mem_kv, w_out, g_post):
    B, S, _ = xp.shape
    nb, T, _ = xs.shape
    assert T == 1
    n_pages = page_table.shape[1]
    n_pool, page = cache_k.shape[1], cache_k.shape[2]
    past = n_pages * page
    M = mem_prompt.shape[1]
    assert page == nb and past % 1024 == 0 and past % min(SAMPLE_SEG, past) == 0

    row = lambda a: a[l][None, :]
    w_pack, w_t = _pack_w_in(w_in[l])
    wo = _pack_w_out(w_out[l])
    gp = row(g_post)

    mk, mv, mkb, mvb = _memkv(mem_prompt, row(g_memnorm), w_mem_kv[l].astype(bf16))
    cos, sa, sb = _rope_tables(jnp.arange(S, dtype=jnp.int32))
    TQ = min(256, S)
    TM = min(512, S)
    (q, k, v, kb, vt, ga, qi, ki, kk, wit, cvg, cst, mg) = _front_prompt(
        xp, row(g_pre), w_pack, w_t, cos, sa, sb, conv_w[l], mkb, mvb, TM, TQ)

    cos, sa, sb = _rope_tables(jnp.full((nb,), past, jnp.int32))
    st = state_conv[l]
    (q_s, k_s, v_s, kb_s, vt_s, ga_s, qi_s, ki_s, kk_s, wit_s, cvg_s, u_s, qm_s, gm_s) = _front_sample(
        xs.reshape(1, nb, D_MODEL), row(g_pre), w_pack, w_t, cos, sa, sb, conv_w[l], st[:, 0, :], st[:, 1, :])
    kidx_t = jnp.swapaxes(cache_kidx[l], 1, 2)
    k_t = jnp.transpose(cache_k[l], (0, 2, 3, 1)).reshape(n_pool, D_KV, page)
    v_t = jnp.transpose(cache_v[l], (0, 2, 3, 1)).reshape(n_pool, D_KV, page)
    cmk_t = jnp.transpose(cache_mem_k[l], (0, 2, 3, 1)).reshape(nb, D_MEM, M)
    cmv_t = jnp.transpose(cache_mem_v[l], (0, 2, 3, 1)).reshape(nb, D_MEM, M)

    sc, dn = _sidx(page_table, qi_s.reshape(nb, N_IDX_HEADS, IDX_DIM), wit_s[0].T.reshape(nb, N_IDX_HEADS, 1),
                   kk_s[0, :, :IDX_DIM], kidx_t)
    bias_t = _ssel(sc.reshape(nb, past).T, dn.reshape(nb, nb).T)
    sample_ops = (page_table, _slab_rows(q_s[0], N_ATT_HEADS), bias_t.T.reshape(nb, 1, past + nb),
                  kb_s[0], vt_s[0, 0].T, _diag_rows(qm_s[0], N_MEM_HEADS), cmk_t, cmv_t, k_t, v_t)

    prompt_ops = (q, qi, wit, kb, vt, kk, ga, cvg, mg, xp, wo, gp, TQ)
    if nb == B * (S // TQ):
        yp, att_s, mo_s = _attn_prompt(*prompt_ops, sample=sample_ops)
    else:
        yp = _attn_prompt(*prompt_ops)
        att_s, mo_s = _sattn(*sample_ops)
    ys = _smerge(att_s.reshape(nb, D_ATT), ga_s[0], cvg_s[0], mo_s.reshape(nb, D_MEM), gm_s[0],
                 xs.reshape(nb, D_MODEL), wo, gp)

    outs_p = (k.reshape(B, S, N_KV_HEADS, HEAD_DIM), v.reshape(B, S, N_KV_HEADS, HEAD_DIM), ki, cst,
              mk.reshape(B, M, N_MEM_HEADS, HEAD_DIM), mv.reshape(B, M, N_MEM_HEADS, HEAD_DIM))
    outs_s = (k_s.reshape(nb, 1, N_KV_HEADS, HEAD_DIM), v_s.reshape(nb, 1, N_KV_HEADS, HEAD_DIM),
              ki_s.reshape(nb, 1, IDX_DIM), jnp.stack([st[:, 1, :], u_s[0]], axis=1))
    return yp, ys.reshape(nb, 1, D_MODEL), outs_p, outs_s


def kernel(x_prompt, x_sample, mem_prompt, cache_k, cache_v, cache_kidx, cache_mem_k, cache_mem_v, state_conv,
           page_table, g_pre, w_in, conv_w, g_memnorm, w_mem_kv, w_out, g_post):
    depth = w_in.shape[0]
    xp, xs = x_prompt, x_sample
    per_p, per_s = [], []
    for l in range(depth):
        xp, xs, op, os_ = _layer(l, xp, xs, mem_prompt, cache_k, cache_v, cache_kidx, cache_mem_k, cache_mem_v,
                                 state_conv, page_table, g_pre, w_in, conv_w, g_memnorm, w_mem_kv, w_out, g_post)
        per_p.append(op)
        per_s.append(os_)
    stack = lambda outs, i: jnp.stack([o[i] for o in outs])
    return ((xp, xs) + tuple(stack(per_p, i) for i in range(6)) + tuple(stack(per_s, i) for i in range(4)))
```

```python
import functools

import jax
import jax.numpy as jnp
from jax import lax
from jax.experimental import pallas as pl
from jax.experimental.pallas import tpu as pltpu

D_MODEL = 1024
HEAD_DIM = 64
N_MEM_HEADS = 4
D_MEM = N_MEM_HEADS * HEAD_DIM
D_ATT = 512
N_ATT_HEADS = 8
N_KV_HEADS = 2
Q_PER_KV = N_ATT_HEADS // N_KV_HEADS
D_KV = N_KV_HEADS * HEAD_DIM
D_CONV = 256
N_IDX_HEADS = 8
IDX_DIM = 64
D_QIDX = N_IDX_HEADS * IDX_DIM
TOPK_MAX = 256
CONV_WIDTH = 3
ROPE_THETA = 10000.0
RMS_EPS = 1e-6
LANES = 128
SUBLANES = 8

R_Q = 0
R_K = R_Q + D_ATT
R_V = R_K + D_KV
R_GATT = R_V + D_KV
R_QIDX = R_GATT + D_ATT
R_KIDX = R_QIDX + D_QIDX
R_WIDX = R_KIDX + IDX_DIM
R_U = R_WIDX + N_IDX_HEADS

ATT_ORDER = (0, 4, 1, 5, 2, 6, 3, 7)
P_Q = 0
P_K = P_Q + D_ATT
P_V = P_K + D_KV
P_GATT = P_V + D_KV
P_QIDX = P_GATT + D_ATT
P_KK = P_QIDX + D_QIDX
P_U = P_KK + LANES
P_B = P_U + D_CONV
P_C = P_B + D_CONV
P_GCONV = P_C + D_CONV
P_QMEM = P_GCONV + D_CONV
P_GMEM = P_QMEM + D_MEM
P_END = P_GMEM + D_MEM

QK_SCALE = HEAD_DIM ** -0.5
LOG2E = 1.4426950408889634
IDX_SCALE = IDX_DIM ** -0.5
W_SCALE = N_IDX_HEADS ** -0.5

F32_MAX = 3.4028234663852886e38
SAMPLE_SEG = 4096
FIXED_PASSES = 16
KEY_PASS_EVERY = 8
MAX_PASSES = 8 * 34
NEG = -1e30
VMEM_LIMIT = 56 * 1024 * 1024
ACC_ROWS = 32
D_T = D_KV + 16

bf16 = jnp.bfloat16
f32 = jnp.float32
NT = (((1,), (1,)), ((), ()))


def _dot(a, b):
    return jnp.dot(a, b, preferred_element_type=f32)


def _dot_nt(a, b):
    return lax.dot_general(a, b, NT, preferred_element_type=f32)


def _rms(x, g):
    return x * lax.rsqrt(jnp.mean(x * x, axis=-1, keepdims=True) + RMS_EPS) * g


def _silu(x):
    return x / (1.0 + jnp.exp(-x))


def _rope(z, cos, sa, sb):
    return z * cos + pltpu.roll(z, 96, 1) * sa + pltpu.roll(z, 32, 1) * sb


def _key_to_f32(k):
    return lax.bitcast_convert_type(k ^ ((k >> 31) & jnp.int32(0x7FFFFFFF)), f32)


def _softmax_parts(s):
    m = jnp.max(s, axis=-1, keepdims=True)
    p = jnp.exp(s - m)
    return p, jnp.sum(p, axis=-1, keepdims=True)


def _f32_to_key(x):
    bits = lax.bitcast_convert_type(x, jnp.int32)
    return bits ^ ((bits >> 31) & jnp.int32(0x7FFFFFFF))


def _select_bias_t(sc_ref, bias_ref, nc, k_eff, n_valid):
    _, TK, Q = sc_ref.shape
    groups = -(-TOPK_MAX // TK)

    def count(cmp, thr):
        thr_b = jnp.broadcast_to(thr, (ACC_ROWS, Q))

        def body(c, acc):
            for j in range(TK // ACC_ROWS):
                acc = acc + jnp.where(cmp(sc_ref[c, j * ACC_ROWS:(j + 1) * ACC_ROWS, :], thr_b), 1.0, 0.0)
            return acc

        acc = lax.fori_loop(0, nc, body, jnp.zeros((ACC_ROWS, Q), f32))
        return jnp.sum(acc, axis=0, keepdims=True)

    ge = lambda a, b: a >= b

    if groups == 1:
        bias_ref[0] = sc_ref[0]

        def gmax_body(c, carry):
            bias_ref[0] = jnp.maximum(bias_ref[0], sc_ref[c])
            return carry

        lax.fori_loop(1, nc, gmax_body, 0)
    else:
        for c in range(nc):
            bias_ref[c % groups] = sc_ref[c] if c < groups else jnp.maximum(bias_ref[c % groups], sc_ref[c])
    gmaxs = [bias_ref[g] for g in range(groups)]
    low = jnp.min(functools.reduce(jnp.minimum, gmaxs), axis=0, keepdims=True)
    top = jnp.max(functools.reduce(jnp.maximum, gmaxs), axis=0, keepdims=True)

    low_b = jnp.broadcast_to(low, (ACC_ROWS, Q))

    def start_body(c, accs):
        a_low, a_ge0, a_gt0 = accs
        for j in range(TK // ACC_ROWS):
            x = sc_ref[c, j * ACC_ROWS:(j + 1) * ACC_ROWS, :]
            a_low = a_low + jnp.where(x >= low_b, 1.0, 0.0)
            a_ge0 = a_ge0 + jnp.where(x >= 0.0, 1.0, 0.0)
            a_gt0 = a_gt0 + jnp.where(x > 0.0, 1.0, 0.0)
        return a_low, a_ge0, a_gt0

    zeros = jnp.zeros((ACC_ROWS, Q), f32)
    c_low, c_ge0, c_gt0 = [jnp.sum(a, axis=0, keepdims=True)
                           for a in lax.fori_loop(0, nc, start_body, (zeros, zeros, zeros))]
    all_in = n_valid == k_eff
    at_zero = (c_gt0 < k_eff) & (c_ge0 >= k_eff)
    above = c_gt0 >= k_eff
    low_ok = low > -F32_MAX
    lo = jnp.where(all_in, -F32_MAX, jnp.where(
        at_zero, 0.0, jnp.where(above, jnp.maximum(low, 0.0), jnp.where(low_ok, low, -F32_MAX))))
    c_lo = jnp.where(all_in, n_valid, jnp.where(
        at_zero, c_ge0, jnp.where(above, jnp.where(low > 0.0, c_low, c_ge0), jnp.where(low_ok, c_low, n_valid))))
    hi = jnp.where(above, _key_to_f32(_f32_to_key(top) + 1), 0.0)
    c_hi = jnp.where(at_zero, c_gt0, jnp.where(above, 0.0, c_ge0))

    settled = all_in | at_zero
    halfway = lambda lo, hi: lo + (hi - lo) * 0.5

    def still_open(lo, hi, c_lo):
        mid = halfway(lo, hi)
        return jnp.where(jnp.logical_not(settled) & (c_lo != k_eff) & (mid > lo) & (mid < hi), 1.0, 0.0)

    def one_pass(it, state):
        lo, hi, c_lo, c_hi, w_lo, w_hi, moved = state
        active = still_open(lo, hi, c_lo) > 0.5
        k_lo, k_hi = _f32_to_key(lo), _f32_to_key(hi)
        a = (c_lo - k_eff + 0.5) * w_lo
        b = (k_eff - 0.5 - c_hi) * w_hi
        by_count = lo + (hi - lo) * (a / (a + b))
        by_key = _key_to_f32(k_lo + lax.shift_right_logical(k_hi - k_lo, 1))
        pivot = jnp.where(it % KEY_PASS_EVERY == KEY_PASS_EVERY - 1, by_key, by_count)
        pivot = jnp.where((pivot > lo) & (pivot < hi), pivot, halfway(lo, hi))
        pivot = jnp.where(active, pivot, lo)
        cnt = count(ge, pivot)
        up = active & (cnt >= k_eff)
        down = active & (cnt < k_eff)
        w_hi = jnp.where(up, jnp.where(moved > 0.5, w_hi * 0.5, 1.0), jnp.where(down, 1.0, w_hi))
        w_lo = jnp.where(down, jnp.where(moved < -0.5, w_lo * 0.5, 1.0), jnp.where(up, 1.0, w_lo))
        moved = jnp.where(up, 1.0, jnp.where(down, -1.0, moved))
        lo, c_lo = jnp.where(up, pivot, lo), jnp.where(up, cnt, c_lo)
        hi, c_hi = jnp.where(down, pivot, hi), jnp.where(down, cnt, c_hi)
        return lo, hi, c_lo, c_hi, w_lo, w_hi, moved

    ones = jnp.ones((1, Q), f32)
    state = lax.fori_loop(0, FIXED_PASSES, one_pass, (lo, hi, c_lo, c_hi, ones, ones, jnp.zeros((1, Q), f32)))
    any_open = lambda st: jnp.max(still_open(st[0], st[1], st[2]))

    def more_passes(carry):
        it, state, _ = carry
        state = one_pass(it + 1, one_pass(it, state))
        return it + 2, state, any_open(state)

    _, state, _ = lax.while_loop(lambda carry: (carry[2] > 0.5) & (carry[0] < MAX_PASSES), more_passes,
                                 (jnp.int32(FIXED_PASSES), state, any_open(state)))
    thr, cnt_u, cnt_above = state[0], state[2], state[3]
    has_tie = jnp.max(jnp.where(cnt_u != k_eff, 1.0, 0.0)) > 0.5

    @pl.when(jnp.logical_not(has_tie))
    def _():
        def bias_body(c, carry):
            bias_ref[c] = jnp.where(sc_ref[c] >= thr, 0.0, NEG)
            return carry

        lax.fori_loop(0, nc, bias_body, 0)

    @pl.when(has_tie)
    def _():
        need = jnp.where(cnt_u == k_eff, F32_MAX, k_eff - cnt_above)
        r_i = lax.broadcasted_iota(jnp.int32, (TK, TK), 0)
        c_i = lax.broadcasted_iota(jnp.int32, (TK, TK), 1)
        tri = jnp.where(c_i <= r_i, 1.0, 0.0).astype(bf16)

        def bias_body(c, run):
            x = sc_ref[c]
            eqf = jnp.where(x == thr, 1.0, 0.0)
            incl = _dot(tri, eqf.astype(bf16))
            tie_ok = jnp.where(run + incl - eqf < need, eqf, 0.0)
            bias_ref[c] = jnp.where(x > thr, 0.0, jnp.where(tie_ok > 0.5, 0.0, NEG))
            return run + incl[TK - 1:TK, :]

        lax.fori_loop(0, nc, bias_body, jnp.zeros((1, Q), f32))


def _merge(att_g, cvg, mg, x, wo_ref, gp):
    o = (_dot(att_g, wo_ref[0:D_ATT, :])
         + _dot(cvg, wo_ref[D_ATT:D_ATT + D_CONV, :])
         + _dot(mg, wo_ref[D_ATT + D_CONV:D_MODEL, :]))
    return x + _rms(o, gp)


def _memkv_kernel(mem_ref, g_ref, w_ref, mk_ref, mv_ref, mkb_ref, mvb_ref):
    kv = _dot(_rms(mem_ref[0], g_ref[...]).astype(bf16), w_ref[...])
    mk = kv[:, :D_MEM]
    mv = kv[:, D_MEM:]
    mk_ref[0] = mk
    mv_ref[0] = mv
    mkb_ref[0] = mk.astype(bf16)
    mvb_ref[0] = mv.astype(bf16)


def _memkv(mem, g, w):
    B, M, _ = mem.shape
    blk = lambda: pl.BlockSpec((1, M, D_MEM), lambda b: (b, 0, 0))
    return pl.pallas_call(
        _memkv_kernel,
        grid=(B,),
        in_specs=[pl.BlockSpec((1, M, D_MODEL), lambda b: (b, 0, 0)),
                  pl.BlockSpec((1, D_MODEL), lambda b: (0, 0)),
                  pl.BlockSpec((D_MODEL, 2 * D_MEM), lambda b: (0, 0))],
        out_specs=[blk(), blk(), blk(), blk()],
        out_shape=[jax.ShapeDtypeStruct((B, M, D_MEM), f32)] * 2
        + [jax.ShapeDtypeStruct((B, M, D_MEM), bf16)] * 2,
        compiler_params=pltpu.CompilerParams(dimension_semantics=("arbitrary",)),
        name="memkv",
    )(mem, g, w)


def _front_common(x_ref, g_ref, w_ref, wt_ref, cos_ref, sa_ref, sb_ref,
                  q_ref, k_ref, v_ref, kb_ref, vt_ref, ga_ref, qi_ref, ki_ref, kk_ref, wit_ref):
    xb = _rms(x_ref[0], g_ref[...]).astype(bf16)
    cos, sa, sb = cos_ref[...], sa_ref[...], sb_ref[...]
    proj = lambda lo, hi: _dot(xb, w_ref[:, lo:hi])
    rope = lambda z: _rope(z, cos, sa, sb)

    zq = proj(P_Q, P_K)
    zqi = proj(P_QIDX, P_KK)
    for s in range(D_ATT // LANES):
        sl = slice(s * LANES, (s + 1) * LANES)
        q_ref[0, :, sl] = (rope(zq[:, sl]) * (QK_SCALE * LOG2E)).astype(bf16)
        qi_ref[0, :, sl] = (rope(zqi[:, sl]) * IDX_SCALE).astype(bf16)
    zkv = proj(P_K, P_GATT)
    zk = rope(zkv[:, :D_KV])
    k_ref[0] = zk
    kb_ref[0] = zk.astype(bf16)
    v_ref[0] = zkv[:, D_KV:]
    ga_ref[0] = _silu(proj(P_GATT, P_QIDX))
    zkk = rope(proj(P_KK, P_U))
    ki_ref[0] = zkk[:, :IDX_DIM]
    kk_ref[0] = zkk.astype(bf16)
    vw_t = _dot_nt(wt_ref[...], xb)
    TK = vt_ref.shape[3]
    for c in range(vt_ref.shape[1]):
        vt_ref[0, c] = vw_t[0:D_KV, c * TK:(c + 1) * TK].astype(bf16)
    wit_ref[0] = vw_t[D_KV:D_KV + N_IDX_HEADS, :] * W_SCALE
    return proj


def _front_prompt_kernel(x_ref, g_ref, w_ref, wt_ref, cos_ref, sa_ref, sb_ref, cw_ref, mkb_ref, mvb_ref,
                         q_ref, k_ref, v_ref, kb_ref, vt_ref, ga_ref, qi_ref, ki_ref, kk_ref, wit_ref,
                         cvg_ref, cst_ref, mg_ref, ubuf):
    TM = x_ref.shape[1]
    proj = _front_common(x_ref, g_ref, w_ref, wt_ref, cos_ref, sa_ref, sb_ref,
                         q_ref, k_ref, v_ref, kb_ref, vt_ref, ga_ref, qi_ref, ki_ref, kk_ref, wit_ref)

    @pl.when(pl.program_id(1) == 0)
    def _():
        ubuf[0:8, :] = jnp.zeros((8, D_CONV), f32)

    u = proj(P_C, P_GCONV) * proj(P_U, P_B)
    ubuf[8:TM + 8, :] = u
    cw = cw_ref[...]
    conv = proj(P_B, P_C) * (cw[0:1] * ubuf[6:TM + 6, :] + cw[1:2] * ubuf[7:TM + 7, :] + cw[2:3] * u)
    cvg_ref[0] = (conv * _silu(proj(P_GCONV, P_QMEM))).astype(bf16)
    tail = u[TM - 2:TM, :]
    ubuf[6:8, :] = tail
    cst_ref[0] = tail

    qm = (proj(P_QMEM, P_GMEM) * QK_SCALE).astype(bf16)
    mkb = mkb_ref[0]
    mvb = mvb_ref[0]
    outs = []
    for h in range(N_MEM_HEADS):
        sl = slice(h * HEAD_DIM, (h + 1) * HEAD_DIM)
        p, l = _softmax_parts(_dot_nt(qm[:, sl], mkb[:, sl]))
        outs.append(_dot(p.astype(bf16), mvb[:, sl]) / l)
    mo = jnp.concatenate(outs, axis=1)
    mg_ref[0] = (mo * _silu(proj(P_GMEM, P_END))).astype(bf16)


def _front_sample_kernel(x_ref, g_ref, w_ref, wt_ref, cos_ref, sa_ref, sb_ref, cw_ref, st0_ref, st1_ref,
                         q_ref, k_ref, v_ref, kb_ref, vt_ref, ga_ref, qi_ref, ki_ref, kk_ref, wit_ref,
                         cvg_ref, u_ref, qm_ref, gm_ref):
    proj = _front_common(x_ref, g_ref, w_ref, wt_ref, cos_ref, sa_ref, sb_ref,
                         q_ref, k_ref, v_ref, kb_ref, vt_ref, ga_ref, qi_ref, ki_ref, kk_ref, wit_ref)
    u = proj(P_C, P_GCONV) * proj(P_U, P_B)
    cw = cw_ref[...]
    conv = proj(P_B, P_C) * (cw[0:1] * st0_ref[...] + cw[1:2] * st1_ref[...] + cw[2:3] * u)
    cvg_ref[0] = (conv * _silu(proj(P_GCONV, P_QMEM))).astype(bf16)
    u_ref[0] = u
    qm_ref[0] = (proj(P_QMEM, P_GMEM) * QK_SCALE).astype(bf16)
    gm_ref[0] = _silu(proj(P_GMEM, P_END))


def _front_out(B, S, TM, TK):
    tok = lambda d: pl.BlockSpec((1, TM, d), lambda b, j: (b, j, 0))
    specs = [tok(D_ATT), tok(D_KV), tok(D_KV), tok(D_KV),
             pl.BlockSpec((1, TM // TK, D_KV, TK), lambda b, j: (b, j, 0, 0)),
             tok(D_ATT), tok(D_QIDX), tok(IDX_DIM), tok(LANES),
             pl.BlockSpec((1, N_IDX_HEADS, TM), lambda b, j: (b, 0, j))]
    sds = lambda d, t: jax.ShapeDtypeStruct((B, S, d), t)
    shapes = [sds(D_ATT, bf16), sds(D_KV, f32), sds(D_KV, f32), sds(D_KV, bf16),
              jax.ShapeDtypeStruct((B, S // TK, D_KV, TK), bf16),
              sds(D_ATT, f32), sds(D_QIDX, bf16), sds(IDX_DIM, f32), sds(LANES, bf16),
              jax.ShapeDtypeStruct((B, N_IDX_HEADS, S), f32)]
    return tok, sds, specs, shapes


def _front_in(TM):
    full = lambda r, c: pl.BlockSpec((r, c), lambda b, j: (0, 0))
    rows = lambda: pl.BlockSpec((TM, LANES), lambda b, j: (j, 0))
    return [pl.BlockSpec((1, TM, D_MODEL), lambda b, j: (b, j, 0)), full(1, D_MODEL), full(D_MODEL, P_END),
            full(D_T, D_MODEL), rows(), rows(), rows(), full(CONV_WIDTH, D_CONV)]


def _front_prompt(x, g, w, wt, cos, sa, sb, cw, mkb, mvb, TM, TK):
    B, S, _ = x.shape
    M = mkb.shape[1]
    tok, sds, specs, shapes = _front_out(B, S, TM, TK)
    mem = lambda: pl.BlockSpec((1, M, D_MEM), lambda b, j: (b, 0, 0))
    return pl.pallas_call(
        _front_prompt_kernel,
        grid=(B, S // TM),
        in_specs=_front_in(TM) + [mem(), mem()],
        out_specs=specs + [tok(D_CONV), pl.BlockSpec((1, CONV_WIDTH - 1, D_CONV), lambda b, j: (b, 0, 0)),
                           tok(D_MEM)],
        out_shape=shapes + [sds(D_CONV, bf16), jax.ShapeDtypeStruct((B, CONV_WIDTH - 1, D_CONV), f32),
                            sds(D_MEM, bf16)],
        scratch_shapes=[pltpu.VMEM((TM + 8, D_CONV), f32)],
        compiler_params=pltpu.CompilerParams(dimension_semantics=("arbitrary", "arbitrary"),
                                             vmem_limit_bytes=VMEM_LIMIT),
        name="front_prompt",
    )(x, g, w, wt, cos, sa, sb, cw, mkb, mvb)


def _front_sample(x, g, w, wt, cos, sa, sb, cw, st0, st1):
    _, T, _ = x.shape
    tok, sds, specs, shapes = _front_out(1, T, T, T)
    st = lambda: pl.BlockSpec((T, D_CONV), lambda b, j: (0, 0))
    return pl.pallas_call(
        _front_sample_kernel,
        grid=(1, 1),
        in_specs=_front_in(T) + [st(), st()],
        out_specs=specs + [tok(D_CONV), tok(D_CONV), tok(D_MEM), tok(D_MEM)],
        out_shape=shapes + [sds(D_CONV, bf16), sds(D_CONV, f32), sds(D_MEM, bf16), sds(D_MEM, f32)],
        compiler_params=pltpu.CompilerParams(dimension_semantics=("arbitrary", "arbitrary"),
                                             vmem_limit_bytes=VMEM_LIMIT),
        name="front_sample",
    )(x, g, w, wt, cos, sa, sb, cw, st0, st1)


def _half_masked(slab, half):
    lane = lax.broadcasted_iota(jnp.int32, slab.shape, 1)
    keep = (lane < HEAD_DIM) if half == 0 else (lane >= HEAD_DIM)
    return jnp.where(keep, slab, jnp.zeros((), slab.dtype))


def _attn_prompt_body(q_ref, qi_ref, wit_ref, kb_ref, vt_ref, kk_ref, ga_ref, cvg_ref, mg_ref, x_ref,
                      wo_ref, gp_ref, y_ref, sc_ref, bias_ref, acc_ref):
    TQ = q_ref.shape[1]
    TK = TQ
    S = kb_ref.shape[1]
    i = pl.program_id(1)
    chunk = lambda c: pl.ds(pl.multiple_of(c * TK, TK), TK)
    slab = lambda ref, p: ref[0, :, (p // 2) * LANES:(p // 2 + 1) * LANES]

    blk = lambda a, p: a[:, p * TQ:(p + 1) * TQ]
    qi_all = jnp.concatenate([_half_masked(slab(qi_ref, h), h % 2) for h in range(N_IDX_HEADS)], axis=0)
    wit = wit_ref[0]

    def scores_t(c):
        d = _dot_nt(kk_ref[0, chunk(c), :], qi_all)
        acc = jnp.maximum(blk(d, 0), 0.0) * wit[0:1, :]
        for h in range(1, N_IDX_HEADS):
            acc = acc + jnp.maximum(blk(d, h), 0.0) * wit[h:h + 1, :]
        return acc

    def score_body(j, carry):
        for r in range(2):
            c = jnp.minimum(2 * j + r, i - 1)
            sc_ref[c] = scores_t(c)
        return carry

    lax.fori_loop(0, lax.shift_right_logical(i + 1, 1), score_body, 0)
    key_pos = lax.broadcasted_iota(jnp.int32, (TK, TQ), 0)
    qry_pos = lax.broadcasted_iota(jnp.int32, (TK, TQ), 1)
    sc_ref[i] = jnp.where(key_pos <= qry_pos, scores_t(i), -jnp.inf)

    t1 = i * TQ + 1 + lax.broadcasted_iota(jnp.int32, (1, TQ), 1)
    k_eff = jnp.minimum(t1, min(TOPK_MAX, S // 4)).astype(f32)
    _select_bias_t(sc_ref, bias_ref, i + 1, k_eff, t1.astype(f32))

    q_all = jnp.concatenate([_half_masked(slab(q_ref, p), p % 2) for p in range(N_ATT_HEADS)], axis=0)
    acc_ref[...] = jnp.zeros(acc_ref.shape, f32)

    def att_chunk(c_kv, c_bias, ms, ls):
        bias = bias_ref[c_bias]
        s_all = _dot_nt(kb_ref[0, chunk(c_kv), :], q_all)
        ms_new, ls_new, alphas, pts = [], [], [], []
        for p in range(N_ATT_HEADS):
            s = blk(s_all, p) + bias
            m_new = jnp.maximum(ms[p], jnp.max(s, axis=0, keepdims=True))
            alpha = jnp.exp2(ms[p] - m_new)
            pt = jnp.exp2(s - m_new)
            ls_new.append(alpha * ls[p] + jnp.sum(pt, axis=0, keepdims=True))
            ms_new.append(m_new)
            alphas.append(alpha)
            pts.append(pt.astype(bf16))
        pv_all = _dot(vt_ref[0, c_kv], jnp.concatenate(pts, axis=1))
        for p in range(N_ATT_HEADS):
            rows = slice(p * HEAD_DIM, (p + 1) * HEAD_DIM)
            g0 = (p % 2) * HEAD_DIM
            acc_ref[rows, :] = alphas[p] * acc_ref[rows, :] + blk(pv_all, p)[g0:g0 + HEAD_DIM, :]
        return tuple(ms_new), tuple(ls_new)

    def att_body(j, carry):
        return att_chunk(2 * j + 1, 2 * j + 1, *att_chunk(2 * j, 2 * j, *carry))

    init = (tuple(jnp.full((1, TQ), NEG, f32) for _ in range(N_ATT_HEADS)),
            tuple(jnp.zeros((1, TQ), f32) for _ in range(N_ATT_HEADS)))
    carry = lax.fori_loop(0, lax.shift_right_logical(i + 1, 1), att_body, init)
    _, ls = lax.cond((i + 1) % 2 == 1, lambda ms, ls: att_chunk(i, i, ms, ls), lambda ms, ls: (ms, ls), *carry)
    for p in range(N_ATT_HEADS):
        rows = slice(p * HEAD_DIM, (p + 1) * HEAD_DIM)
        acc_ref[rows, :] = acc_ref[rows, :] / ls[p]
    att_g = (acc_ref[...].T * ga_ref[0]).astype(bf16)
    y_ref[0] = _merge(att_g, cvg_ref[0], mg_ref[0], x_ref[0], wo_ref, gp_ref[...])


N_PROMPT_IN = 12


def _attn_prompt_kernel(*refs):
    _attn_prompt_body(*refs)


def _attn_fused_kernel(pt_ref, *refs):
    prompt_in, refs = refs[:N_PROMPT_IN], refs[N_PROMPT_IN:]
    qbd_ref, sbias_ref, knew_ref, vnew_ref, qmbd_ref, cmk_ref, cmv_ref, k_hbm, v_hbm = refs[:9]
    y_ref, o_ref, mo_ref, sc_ref, bias_ref, acc_ref, kbuf, vbuf, semk, semv = refs[9:]
    s = pl.program_id(0) * pl.num_programs(1) + pl.program_id(1)
    ns = pl.num_programs(0) * pl.num_programs(1)
    _sample_prefetch(pt_ref, s, ns, k_hbm, v_hbm, kbuf, vbuf, semk, semv)
    _attn_prompt_body(*prompt_in, y_ref, sc_ref, bias_ref, acc_ref)
    _sample_attend(pt_ref, s, qbd_ref, sbias_ref, knew_ref, vnew_ref, qmbd_ref, cmk_ref, cmv_ref, k_hbm, v_hbm,
                   o_ref, mo_ref, kbuf, vbuf, semk, semv)


def _attn_prompt(q, qi, wit, kb, vt, kk, ga, cvg, mg, x, wo, gp, TQ, sample=None):
    B, S, _ = x.shape
    NC = S // TQ
    imap = lambda f: (lambda b, i, *_: f(b, i))
    tok = lambda d: pl.BlockSpec((1, TQ, d), imap(lambda b, i: (b, i, 0)))
    seq = lambda d: pl.BlockSpec((1, S, d), imap(lambda b, i: (b, 0, 0)))
    full = lambda r, c: pl.BlockSpec((r, c), imap(lambda b, i: (0, 0)))
    in_specs = [tok(D_ATT), tok(D_QIDX), pl.BlockSpec((1, N_IDX_HEADS, TQ), imap(lambda b, i: (b, 0, i))),
                seq(D_KV), pl.BlockSpec((1, NC, D_KV, TQ), imap(lambda b, i: (b, 0, 0, 0))), seq(LANES),
                tok(D_ATT), tok(D_CONV), tok(D_MEM), tok(D_MODEL), full(D_MODEL, D_MODEL), full(1, D_MODEL)]
    assert len(in_specs) == N_PROMPT_IN
    scratch = [pltpu.VMEM((NC, TQ, TQ), f32), pltpu.VMEM((NC, TQ, TQ), f32), pltpu.VMEM((D_ATT, TQ), f32)]
    params = pltpu.CompilerParams(dimension_semantics=("arbitrary", "arbitrary"), vmem_limit_bytes=VMEM_LIMIT)
    y_shape = jax.ShapeDtypeStruct((B, S, D_MODEL), f32)
    if sample is None:
        return pl.pallas_call(
            _attn_prompt_kernel, grid=(B, NC), in_specs=in_specs, out_specs=tok(D_MODEL), out_shape=y_shape,
            scratch_shapes=scratch, compiler_params=params, name="attn_prompt",
        )(q, qi, wit, kb, vt, kk, ga, cvg, mg, x, wo, gp)
    pt, *sample_ops = sample
    s_in, s_out, s_shapes, s_scratch = _sattn_specs(pt, *sample_ops, index=lambda b, i: b * NC + i)
    return pl.pallas_call(
        _attn_fused_kernel,
        grid_spec=pltpu.PrefetchScalarGridSpec(
            num_scalar_prefetch=1, grid=(B, NC), in_specs=in_specs + s_in, out_specs=[tok(D_MODEL)] + s_out,
            scratch_shapes=scratch + s_scratch),
        out_shape=[y_shape] + s_shapes, compiler_params=params, name="attn_prompt",
    )(pt, q, qi, wit, kb, vt, kk, ga, cvg, mg, x, wo, gp, *sample_ops)


def _paged_copies(pt_ref, b, hbm_ref, buf, sem, slot, n_pages, page):
    return [pltpu.make_async_copy(hbm_ref.at[pt_ref[b, p]], buf.at[slot, :, pl.ds(p * page, page)], sem.at[slot])
            for p in range(n_pages)]


def _sidx_kernel(pt_ref, qi_ref, wi_ref, knew_ref, kidx_hbm, sc_ref, dn_ref, buf, sem):
    b = pl.program_id(0)
    nb = pl.num_programs(0)
    n_pages = pt_ref.shape[1]
    page = kidx_hbm.shape[2]
    copies = lambda bb, slot: _paged_copies(pt_ref, bb, kidx_hbm, buf, sem, slot, n_pages, page)
    slot = lax.rem(b, 2)

    @pl.when(b == 0)
    def _():
        for p, cp in enumerate(copies(0, 0)):
            cp.start(priority=p % 2)

    @pl.when(b + 1 < nb)
    def _():
        for p, cp in enumerate(copies(b + 1, 1 - slot)):
            cp.start(priority=p % 2)

    for cp in copies(b, slot):
        cp.wait()

    qi = qi_ref[0]
    wi = wi_ref[0]
    weigh = lambda d: jnp.sum(jnp.maximum(d, 0.0) * wi, axis=0, keepdims=True)
    sub = 1024
    for j in range(n_pages * page // sub):
        sc_ref[0, :, j * sub:(j + 1) * sub] = weigh(_dot(qi, buf[slot, :, j * sub:(j + 1) * sub].astype(bf16)))
    dn_ref[0] = weigh(_dot_nt(qi, knew_ref[...]))


def _sidx(pt, qi8, wi8, knew, kidx_t_pool):
    nb, n_pages = pt.shape
    page = kidx_t_pool.shape[2]
    past = n_pages * page
    return pl.pallas_call(
        _sidx_kernel,
        grid_spec=pltpu.PrefetchScalarGridSpec(
            num_scalar_prefetch=1,
            grid=(nb,),
            in_specs=[pl.BlockSpec((1, N_IDX_HEADS, IDX_DIM), lambda b, pt: (b, 0, 0)),
                      pl.BlockSpec((1, N_IDX_HEADS, 1), lambda b, pt: (b, 0, 0)),
                      pl.BlockSpec((nb, IDX_DIM), lambda b, pt: (0, 0)),
                      pl.BlockSpec(memory_space=pl.ANY)],
            out_specs=[pl.BlockSpec((1, 1, past), lambda b, pt: (b, 0, 0)),
                       pl.BlockSpec((1, 1, nb), lambda b, pt: (b, 0, 0))],
            scratch_shapes=[pltpu.VMEM((2, IDX_DIM, past), f32), pltpu.SemaphoreType.DMA((2,))]),
        out_shape=[jax.ShapeDtypeStruct((nb, 1, past), f32), jax.ShapeDtypeStruct((nb, 1, nb), f32)],
        compiler_params=pltpu.CompilerParams(dimension_semantics=("arbitrary",), vmem_limit_bytes=VMEM_LIMIT),
        name="sample_indexer",
    )(pt, qi8, wi8, knew, kidx_t_pool)


def _ssel_kernel(sct_ref, dn_ref, out_ref, sc_ref, bias_ref):
    nc, TK, Q = sc_ref.shape
    for c in range(nc - 1):
        sc_ref[c] = sct_ref[c * TK:(c + 1) * TK, :]
    row = lax.broadcasted_iota(jnp.int32, (TK, Q), 0)
    col = lax.broadcasted_iota(jnp.int32, (TK, Q), 1)
    sc_ref[nc - 1] = jnp.where(row == col, dn_ref[...], -jnp.inf)
    n_valid = (nc - 1) * TK + 1
    k_eff = jnp.full((1, Q), float(min(TOPK_MAX, n_valid // 4)), f32)
    _select_bias_t(sc_ref, bias_ref, nc, k_eff, jnp.full((1, Q), float(n_valid), f32))
    for c in range(nc):
        out_ref[c * TK:(c + 1) * TK, :] = bias_ref[c]


def _ssel(sc_t, dn_t):
    past, nb = sc_t.shape
    nc = past // nb + 1
    return pl.pallas_call(
        _ssel_kernel,
        out_shape=jax.ShapeDtypeStruct((past + nb, nb), f32),
        scratch_shapes=[pltpu.VMEM((nc, nb, nb), f32), pltpu.VMEM((nc, nb, nb), f32)],
        compiler_params=pltpu.CompilerParams(vmem_limit_bytes=VMEM_LIMIT),
        name="sample_select",
    )(sc_t, dn_t)


def _sample_copies(pt_ref, s, slot, k_hbm, v_hbm, kbuf, vbuf, semk, semv):
    n_pages, page = pt_ref.shape[1], k_hbm.shape[2]
    return (_paged_copies(pt_ref, s, k_hbm, kbuf, semk, slot, n_pages, page)
            + _paged_copies(pt_ref, s, v_hbm, vbuf, semv, slot, n_pages, page))


def _sample_prefetch(pt_ref, s, ns, k_hbm, v_hbm, kbuf, vbuf, semk, semv):
    slot = lax.rem(s, 2)

    @pl.when(s == 0)
    def _():
        for cp in _sample_copies(pt_ref, 0, 0, k_hbm, v_hbm, kbuf, vbuf, semk, semv):
            cp.start()

    @pl.when(s + 1 < ns)
    def _():
        for cp in _sample_copies(pt_ref, s + 1, 1 - slot, k_hbm, v_hbm, kbuf, vbuf, semk, semv):
            cp.start()


def _sample_attend(pt_ref, s, qbd_ref, bias_ref, knew_ref, vnew_ref, qmbd_ref, cmk_ref, cmv_ref, k_hbm, v_hbm,
                   o_ref, mo_ref, kbuf, vbuf, semk, semv):
    past = kbuf.shape[2]
    nb = knew_ref.shape[0]
    slot = lax.rem(s, 2)

    qm = qmbd_ref[0]
    pm, lm = _softmax_parts(_dot(qm, cmk_ref[0].astype(bf16)))
    mo = _dot_nt(pm.astype(bf16), cmv_ref[0].astype(bf16)) / lm
    for h in range(N_MEM_HEADS):
        hs = slice(h * HEAD_DIM, (h + 1) * HEAD_DIM)
        mo_ref[0, :, hs] = mo[h:h + 1, hs]

    for cp in _sample_copies(pt_ref, s, slot, k_hbm, v_hbm, kbuf, vbuf, semk, semv):
        cp.wait()

    qbd = qbd_ref[0]

    def softmax_part(logits, value_product):
        m_j = jnp.max(logits, axis=1, keepdims=True)
        p_j = jnp.exp2(logits - m_j)
        return m_j, jnp.sum(p_j, axis=1, keepdims=True), value_product(p_j.astype(bf16))

    seg = min(SAMPLE_SEG, past)
    parts = []
    for j in range(past // seg):
        ks = slice(j * seg, (j + 1) * seg)
        parts.append(softmax_part(_dot(qbd, kbuf[slot, :, ks].astype(bf16)) + bias_ref[0, :, ks],
                                  lambda p_j, ks=ks: _dot_nt(p_j, vbuf[slot, :, ks].astype(bf16))))
    parts.append(softmax_part(_dot_nt(qbd, knew_ref[...]) + bias_ref[0, :, past:past + nb],
                              lambda p_j: _dot(p_j, vnew_ref[...])))
    m = functools.reduce(jnp.maximum, [m_j for m_j, _, _ in parts])
    scales = [jnp.exp2(m_j - m) for m_j, _, _ in parts]
    l = sum(l_j * c for (_, l_j, _), c in zip(parts, scales))
    o = sum(o_j * c for (_, _, o_j), c in zip(parts, scales)) / l
    for p in range(N_ATT_HEADS):
        g0 = (p % 2) * HEAD_DIM
        o_ref[0, :, p * HEAD_DIM:(p + 1) * HEAD_DIM] = o[p:p + 1, g0:g0 + HEAD_DIM]


def _sattn_kernel(pt_ref, qbd_ref, bias_ref, knew_ref, vnew_ref, qmbd_ref, cmk_ref, cmv_ref, k_hbm, v_hbm,
                  o_ref, mo_ref, kbuf, vbuf, semk, semv):
    s = pl.program_id(0)
    _sample_prefetch(pt_ref, s, pl.num_programs(0), k_hbm, v_hbm, kbuf, vbuf, semk, semv)
    _sample_attend(pt_ref, s, qbd_ref, bias_ref, knew_ref, vnew_ref, qmbd_ref, cmk_ref, cmv_ref, k_hbm, v_hbm,
                   o_ref, mo_ref, kbuf, vbuf, semk, semv)


def _sattn_specs(pt, qbd, bias, knew, vnew, qmbd, cmk_t, cmv_t, k_t_pool, v_t_pool, index):
    nb, n_pages = pt.shape
    past = n_pages * k_t_pool.shape[2]
    M = cmk_t.shape[2]
    per = lambda r, c: pl.BlockSpec((1, r, c), lambda *g: (index(*g[:-1]), 0, 0))
    full = lambda r, c: pl.BlockSpec((r, c), lambda *g: (0, 0))
    in_specs = [per(N_ATT_HEADS, D_KV), per(1, past + nb), full(nb, D_KV), full(nb, D_KV),
                per(N_MEM_HEADS, D_MEM), per(D_MEM, M), per(D_MEM, M),
                pl.BlockSpec(memory_space=pl.ANY), pl.BlockSpec(memory_space=pl.ANY)]
    out_specs = [per(1, D_ATT), per(1, D_MEM)]
    out_shapes = [jax.ShapeDtypeStruct((nb, 1, D_ATT), f32), jax.ShapeDtypeStruct((nb, 1, D_MEM), f32)]
    scratch = [pltpu.VMEM((2, D_KV, past), f32), pltpu.VMEM((2, D_KV, past), f32),
               pltpu.SemaphoreType.DMA((2,)), pltpu.SemaphoreType.DMA((2,))]
    return in_specs, out_specs, out_shapes, scratch


def _sattn(pt, *ops):
    in_specs, out_specs, out_shapes, scratch = _sattn_specs(pt, *ops, index=lambda b: b)
    return pl.pallas_call(
        _sattn_kernel,
        grid_spec=pltpu.PrefetchScalarGridSpec(num_scalar_prefetch=1, grid=(pt.shape[0],), in_specs=in_specs,
                                               out_specs=out_specs, scratch_shapes=scratch),
        out_shape=out_shapes,
        compiler_params=pltpu.CompilerParams(dimension_semantics=("arbitrary",), vmem_limit_bytes=VMEM_LIMIT),
        name="sample_attention",
    )(pt, *ops)


def _smerge_kernel(att_ref, ga_ref, cvg_ref, mo_ref, gm_ref, x_ref, wo_ref, gp_ref, y_ref):
    att_g = (att_ref[...] * ga_ref[...]).astype(bf16)
    mg = (mo_ref[...] * gm_ref[...]).astype(bf16)
    y_ref[...] = _merge(att_g, cvg_ref[...], mg, x_ref[...], wo_ref, gp_ref[...])


def _smerge(att, ga, cvg, mo, gm, x, wo, gp):
    return pl.pallas_call(
        _smerge_kernel,
        out_shape=jax.ShapeDtypeStruct(x.shape, f32),
        compiler_params=pltpu.CompilerParams(vmem_limit_bytes=VMEM_LIMIT),
        name="sample_merge",
    )(att, ga, cvg, mo, gm, x, wo, gp)


def _rope_tables(pos):
    half = HEAD_DIM // 2
    inv = ROPE_THETA ** (-jnp.arange(half, dtype=f32) / half)
    ang = pos.astype(f32)[:, None] * inv[None, :]
    cos, sin = jnp.cos(ang), jnp.sin(ang)
    zero = jnp.zeros_like(sin)
    rep = LANES // HEAD_DIM
    return (jnp.tile(jnp.concatenate([cos, cos], axis=1), (1, rep)),
            jnp.tile(jnp.concatenate([-sin, zero], axis=1), (1, rep)),
            jnp.tile(jnp.concatenate([zero, sin], axis=1), (1, rep)))


def _pack_w_in(w):
    head = lambda base: jnp.concatenate([w[:, base + h * HEAD_DIM:base + (h + 1) * HEAD_DIM] for h in ATT_ORDER], 1)
    kidx = w[:, R_KIDX:R_WIDX]
    packed = jnp.concatenate([head(R_Q), w[:, R_K:R_GATT], head(R_GATT), w[:, R_QIDX:R_KIDX], kidx, kidx,
                              w[:, R_U:]], axis=1)
    w_t = jnp.concatenate([w[:, R_V:R_GATT], w[:, R_WIDX:R_U], jnp.zeros((D_MODEL, D_T - D_KV - N_IDX_HEADS), w.dtype)],
                          axis=1).T
    return packed.astype(bf16), w_t.astype(bf16)


def _pack_w_out(w):
    att = jnp.concatenate([w[h * HEAD_DIM:(h + 1) * HEAD_DIM] for h in ATT_ORDER], axis=0)
    return jnp.concatenate([att, w[D_ATT:]], axis=0).astype(bf16)


def _slab_rows(x, n_rows):
    T = x.shape[0]
    slabs = jnp.repeat(x.reshape(T, n_rows // 2, 1, 2, HEAD_DIM), 2, axis=2)
    keep = jnp.arange(2)[:, None] == jnp.arange(2)[None, :]
    return jnp.where(keep[None, None, :, :, None], slabs, jnp.zeros((), x.dtype)).reshape(T, n_rows, LANES)


def _diag_rows(x, n_heads):
    T = x.shape[0]
    keep = jnp.arange(n_heads)[:, None] == jnp.arange(n_heads)[None, :]
    xh = x.reshape(T, 1, n_heads, HEAD_DIM)
    return jnp.where(keep[None, :, :, None], xh, jnp.zeros((), x.dtype)).reshape(T, n_heads, n_heads * HEAD_DIM)


def _layer(l, xp, xs, mem_prompt, cache_k, cache_v, cache_kidx, cache_mem_k, cache_mem_v, state_conv,
           page_table, g_pre, w_in, conv_w, g_memnorm, w_mem_kv, w_out, g_post):
    B, S, _ = xp.shape
    nb, T, _ = xs.shape
    assert T == 1
    n_pages = page_table.shape[1]
    n_pool, page = cache_k.shape[1], cache_k.shape[2]
    past = n_pages * page
    M = mem_prompt.shape[1]
    assert page == nb and past % 1024 == 0 and past % min(SAMPLE_SEG, past) == 0

    row = lambda a: a[l][None, :]
    w_pack, w_t = _pack_w_in(w_in[l])
    wo = _pack_w_out(w_out[l])
    gp = row(g_post)

    mk, mv, mkb, mvb = _memkv(mem_prompt, row(g_memnorm), w_mem_kv[l].astype(bf16))
    cos, sa, sb = _rope_tables(jnp.arange(S, dtype=jnp.int32))
    TQ = min(256, S)
    TM = min(512, S)
    (q, k, v, kb, vt, ga, qi, ki, kk, wit, cvg, cst, mg) = _front_prompt(
        xp, row(g_pre), w_pack, w_t, cos, sa, sb, conv_w[l], mkb, mvb, TM, TQ)

    cos, sa, sb = _rope_tables(jnp.full((nb,), past, jnp.int32))
    st = state_conv[l]
    (q_s, k_s, v_s, kb_s, vt_s, ga_s, qi_s, ki_s, kk_s, wit_s, cvg_s, u_s, qm_s, gm_s) = _front_sample(
        xs.reshape(1, nb, D_MODEL), row(g_pre), w_pack, w_t, cos, sa, sb, conv_w[l], st[:, 0, :], st[:, 1, :])
    kidx_t = jnp.swapaxes(cache_kidx[l], 1, 2)
    k_t = jnp.transpose(cache_k[l], (0, 2, 3, 1)).reshape(n_pool, D_KV, page)
    v_t = jnp.transpose(cache_v[l], (0, 2, 3, 1)).reshape(n_pool, D_KV, page)
    cmk_t = jnp.transpose(cache_mem_k[l], (0, 2, 3, 1)).reshape(nb, D_MEM, M)
    cmv_t = jnp.transpose(cache_mem_v[l], (0, 2, 3, 1)).reshape(nb, D_MEM, M)

    sc, dn = _sidx(page_table, qi_s.reshape(nb, N_IDX_HEADS, IDX_DIM), wit_s[0].T.reshape(nb, N_IDX_HEADS, 1),
                   kk_s[0, :, :IDX_DIM], kidx_t)
    bias_t = _ssel(sc.reshape(nb, past).T, dn.reshape(nb, nb).T)
    sample_ops = (page_table, _slab_rows(q_s[0], N_ATT_HEADS), bias_t.T.reshape(nb, 1, past + nb),
                  kb_s[0], vt_s[0, 0].T, _diag_rows(qm_s[0], N_MEM_HEADS), cmk_t, cmv_t, k_t, v_t)

    prompt_ops = (q, qi, wit, kb, vt, kk, ga, cvg, mg, xp, wo, gp, TQ)
    if nb == B * (S // TQ):
        yp, att_s, mo_s = _attn_prompt(*prompt_ops, sample=sample_ops)
    else:
        yp = _attn_prompt(*prompt_ops)
        att_s, mo_s = _sattn(*sample_ops)
    ys = _smerge(att_s.reshape(nb, D_ATT), ga_s[0], cvg_s[0], mo_s.reshape(nb, D_MEM), gm_s[0],
                 xs.reshape(nb, D_MODEL), wo, gp)

    outs_p = (k.reshape(B, S, N_KV_HEADS, HEAD_DIM), v.reshape(B, S, N_KV_HEADS, HEAD_DIM), ki, cst,
              mk.reshape(B, M, N_MEM_HEADS, HEAD_DIM), mv.reshape(B, M, N_MEM_HEADS, HEAD_DIM))
    outs_s = (k_s.reshape(nb, 1, N_KV_HEADS, HEAD_DIM), v_s.reshape(nb, 1, N_KV_HEADS, HEAD_DIM),
              ki_s.reshape(nb, 1, IDX_DIM), jnp.stack([st[:, 1, :], u_s[0]], axis=1))
    return yp, ys.reshape(nb, 1, D_MODEL), outs_p, outs_s


def kernel(x_prompt, x_sample, mem_prompt, cache_k, cache_v, cache_kidx, cache_mem_k, cache_mem_v, state_conv,
           page_table, g_pre, w_in, conv_w, g_memnorm, w_mem_kv, w_out, g_post):
    depth = w_in.shape[0]
    xp, xs = x_prompt, x_sample
    per_p, per_s = [], []
    for l in range(depth):
        xp, xs, op, os_ = _layer(l, xp, xs, mem_prompt, cache_k, cache_v, cache_kidx, cache_mem_k, cache_mem_v,
                                 state_conv, page_table, g_pre, w_in, conv_w, g_memnorm, w_mem_kv, w_out, g_post)
        per_p.append(op)
        per_s.append(os_)
    stack = lambda outs, i: jnp.stack([o[i] for o in outs])
    return ((xp, xs) + tuple(stack(per_p, i) for i in range(6)) + tuple(stack(per_s, i) for i in range(4)))
```

```python
import functools

import jax
import jax.numpy as jnp
from jax import lax
from jax.experimental import pallas as pl
from jax.experimental.pallas import tpu as pltpu

D_MODEL = 1024
HEAD_DIM = 64
N_MEM_HEADS = 4
D_MEM = N_MEM_HEADS * HEAD_DIM
D_ATT = 512
N_ATT_HEADS = 8
N_KV_HEADS = 2
Q_PER_KV = N_ATT_HEADS // N_KV_HEADS
D_KV = N_KV_HEADS * HEAD_DIM
D_CONV = 256
N_IDX_HEADS = 8
IDX_DIM = 64
D_QIDX = N_IDX_HEADS * IDX_DIM
TOPK_MAX = 256
CONV_WIDTH = 3
ROPE_THETA = 10000.0
RMS_EPS = 1e-6
LANES = 128
SUBLANES = 8

R_Q = 0
R_K = R_Q + D_ATT
R_V = R_K + D_KV
R_GATT = R_V + D_KV
R_QIDX = R_GATT + D_ATT
R_KIDX = R_QIDX + D_QIDX
R_WIDX = R_KIDX + IDX_DIM
R_U = R_WIDX + N_IDX_HEADS

ATT_ORDER = (0, 4, 1, 5, 2, 6, 3, 7)
P_Q = 0
P_K = P_Q + D_ATT
P_V = P_K + D_KV
P_GATT = P_V + D_KV
P_QIDX = P_GATT + D_ATT
P_KK = P_QIDX + D_QIDX
P_U = P_KK + LANES
P_B = P_U + D_CONV
P_C = P_B + D_CONV
P_GCONV = P_C + D_CONV
P_QMEM = P_GCONV + D_CONV
P_GMEM = P_QMEM + D_MEM
P_END = P_GMEM + D_MEM

QK_SCALE = HEAD_DIM ** -0.5
LOG2E = 1.4426950408889634
IDX_SCALE = IDX_DIM ** -0.5
W_SCALE = N_IDX_HEADS ** -0.5

F32_MAX = 3.4028234663852886e38
SAMPLE_SEG = 4096
FIXED_PASSES = 16
KEY_PASS_EVERY = 8
MAX_PASSES = 8 * 34
NEG = -1e30
VMEM_LIMIT = 56 * 1024 * 1024
ACC_ROWS = 32
D_T = D_KV + 16

bf16 = jnp.bfloat16
f32 = jnp.float32
NT = (((1,), (1,)), ((), ()))


def _dot(a, b):
    return jnp.dot(a, b, preferred_element_type=f32)


def _dot_nt(a, b):
    return lax.dot_general(a, b, NT, preferred_element_type=f32)


def _rms(x, g):
    return x * lax.rsqrt(jnp.mean(x * x, axis=-1, keepdims=True) + RMS_EPS) * g


def _silu(x):
    return x / (1.0 + jnp.exp(-x))


def _rope(z, cos, sa, sb):
    return z * cos + pltpu.roll(z, 96, 1) * sa + pltpu.roll(z, 32, 1) * sb


def _key_to_f32(k):
    return lax.bitcast_convert_type(k ^ ((k >> 31) & jnp.int32(0x7FFFFFFF)), f32)


def _softmax_parts(s):
    m = jnp.max(s, axis=-1, keepdims=True)
    p = jnp.exp(s - m)
    return p, jnp.sum(p, axis=-1, keepdims=True)


def _f32_to_key(x):
    bits = lax.bitcast_convert_type(x, jnp.int32)
    return bits ^ ((bits >> 31) & jnp.int32(0x7FFFFFFF))


def _select_bias_t(sc_ref, bias_ref, nc, k_eff, n_valid):
    _, TK, Q = sc_ref.shape
    groups = -(-TOPK_MAX // TK)

    def count(cmp, thr):
        thr_b = jnp.broadcast_to(thr, (ACC_ROWS, Q))

        def body(c, acc):
            for j in range(TK // ACC_ROWS):
                acc = acc + jnp.where(cmp(sc_ref[c, j * ACC_ROWS:(j + 1) * ACC_ROWS, :], thr_b), 1.0, 0.0)
            return acc

        acc = lax.fori_loop(0, nc, body, jnp.zeros((ACC_ROWS, Q), f32))
        return jnp.sum(acc, axis=0, keepdims=True)

    ge = lambda a, b: a >= b

    if groups == 1:
        bias_ref[0] = sc_ref[0]

        def gmax_body(c, carry):
            bias_ref[0] = jnp.maximum(bias_ref[0], sc_ref[c])
            return carry

        lax.fori_loop(1, nc, gmax_body, 0)
    else:
        for c in range(nc):
            bias_ref[c % groups] = sc_ref[c] if c < groups else jnp.maximum(bias_ref[c % groups], sc_ref[c])
    gmaxs = [bias_ref[g] for g in range(groups)]
    low = jnp.min(functools.reduce(jnp.minimum, gmaxs), axis=0, keepdims=True)
    top = jnp.max(functools.reduce(jnp.maximum, gmaxs), axis=0, keepdims=True)

    low_b = jnp.broadcast_to(low, (ACC_ROWS, Q))

    def start_body(c, accs):
        a_low, a_ge0, a_gt0 = accs
        for j in range(TK // ACC_ROWS):
            x = sc_ref[c, j * ACC_ROWS:(j + 1) * ACC_ROWS, :]
            a_low = a_low + jnp.where(x >= low_b, 1.0, 0.0)
            a_ge0 = a_ge0 + jnp.where(x >= 0.0, 1.0, 0.0)
            a_gt0 = a_gt0 + jnp.where(x > 0.0, 1.0, 0.0)
        return a_low, a_ge0, a_gt0

    zeros = jnp.zeros((ACC_ROWS, Q), f32)
    c_low, c_ge0, c_gt0 = [jnp.sum(a, axis=0, keepdims=True)
                           for a in lax.fori_loop(0, nc, start_body, (zeros, zeros, zeros))]
    all_in = n_valid == k_eff
    at_zero = (c_gt0 < k_eff) & (c_ge0 >= k_eff)
    above = c_gt0 >= k_eff
    low_ok = low > -F32_MAX
    lo = jnp.where(all_in, -F32_MAX, jnp.where(
        at_zero, 0.0, jnp.where(above, jnp.maximum(low, 0.0), jnp.where(low_ok, low, -F32_MAX))))
    c_lo = jnp.where(all_in, n_valid, jnp.where(
        at_zero, c_ge0, jnp.where(above, jnp.where(low > 0.0, c_low, c_ge0), jnp.where(low_ok, c_low, n_valid))))
    hi = jnp.where(above, _key_to_f32(_f32_to_key(top) + 1), 0.0)
    c_hi = jnp.where(at_zero, c_gt0, jnp.where(above, 0.0, c_ge0))

    settled = all_in | at_zero
    halfway = lambda lo, hi: lo + (hi - lo) * 0.5

    def still_open(lo, hi, c_lo):
        mid = halfway(lo, hi)
        return jnp.where(jnp.logical_not(settled) & (c_lo != k_eff) & (mid > lo) & (mid < hi), 1.0, 0.0)

    def one_pass(it, state):
        lo, hi, c_lo, c_hi, w_lo, w_hi, moved = state
        active = still_open(lo, hi, c_lo) > 0.5
        k_lo, k_hi = _f32_to_key(lo), _f32_to_key(hi)
        a = (c_lo - k_eff + 0.5) * w_lo
        b = (k_eff - 0.5 - c_hi) * w_hi
        by_count = lo + (hi - lo) * (a / (a + b))
        by_key = _key_to_f32(k_lo + lax.shift_right_logical(k_hi - k_lo, 1))
        pivot = jnp.where(it % KEY_PASS_EVERY == KEY_PASS_EVERY - 1, by_key, by_count)
        pivot = jnp.where((pivot > lo) & (pivot < hi), pivot, halfway(lo, hi))
        pivot = jnp.where(active, pivot, lo)
        cnt = count(ge, pivot)
        up = active & (cnt >= k_eff)
        down = active & (cnt < k_eff)
        w_hi = jnp.where(up, jnp.where(moved > 0.5, w_hi * 0.5, 1.0), jnp.where(down, 1.0, w_hi))
        w_lo = jnp.where(down, jnp.where(moved < -0.5, w_lo * 0.5, 1.0), jnp.where(up, 1.0, w_lo))
        moved = jnp.where(up, 1.0, jnp.where(down, -1.0, moved))
        lo, c_lo = jnp.where(up, pivot, lo), jnp.where(up, cnt, c_lo)
        hi, c_hi = jnp.where(down, pivot, hi), jnp.where(down, cnt, c_hi)
        return lo, hi, c_lo, c_hi, w_lo, w_hi, moved

    ones = jnp.ones((1, Q), f32)
    state = lax.fori_loop(0, FIXED_PASSES, one_pass, (lo, hi, c_lo, c_hi, ones, ones, jnp.zeros((1, Q), f32)))
    any_open = lambda st: jnp.max(still_open(st[0], st[1], st[2]))

    def more_passes(carry):
        it, state, _ = carry
        state = one_pass(it + 1, one_pass(it, state))
        return it + 2, state, any_open(state)

    _, state, _ = lax.while_loop(lambda carry: (carry[2] > 0.5) & (carry[0] < MAX_PASSES), more_passes,
                                 (jnp.int32(FIXED_PASSES), state, any_open(state)))
    thr, cnt_u, cnt_above = state[0], state[2], state[3]
    has_tie = jnp.max(jnp.where(cnt_u != k_eff, 1.0, 0.0)) > 0.5

    @pl.when(jnp.logical_not(has_tie))
    def _():
        def bias_body(c, carry):
            bias_ref[c] = jnp.where(sc_ref[c] >= thr, 0.0, NEG)
            return carry

        lax.fori_loop(0, nc, bias_body, 0)

    @pl.when(has_tie)
    def _():
        need = jnp.where(cnt_u == k_eff, F32_MAX, k_eff - cnt_above)
        r_i = lax.broadcasted_iota(jnp.int32, (TK, TK), 0)
        c_i = lax.broadcasted_iota(jnp.int32, (TK, TK), 1)
        tri = jnp.where(c_i <= r_i, 1.0, 0.0).astype(bf16)

        def tie_chunk(c, run):
            x = sc_ref[c]
            eq = x == thr
            incl = _dot(tri, jnp.where(eq, 1.0, 0.0).astype(bf16))
            tied = jnp.where(incl < need + 0.5 - run, jnp.where(eq, 0.0, NEG), NEG)
            bias_ref[c] = jnp.where(x > thr, 0.0, tied)
            return run + incl[TK - 1:TK, :]

        def bias_body(j, run):
            return tie_chunk(2 * j + 1, tie_chunk(2 * j, run))

        run = lax.fori_loop(0, nc // 2, bias_body, jnp.zeros((1, Q), f32))
        lax.cond(nc % 2 == 1, lambda r: tie_chunk(nc - 1, r), lambda r: r, run)


def _merge(att_g, cvg, mg, x, wo_ref, gp):
    o = (_dot(att_g, wo_ref[0:D_ATT, :])
         + _dot(cvg, wo_ref[D_ATT:D_ATT + D_CONV, :])
         + _dot(mg, wo_ref[D_ATT + D_CONV:D_MODEL, :]))
    return x + _rms(o, gp)


def _memkv_kernel(mem_ref, g_ref, w_ref, mk_ref, mv_ref, mkb_ref, mvb_ref):
    kv = _dot(_rms(mem_ref[0], g_ref[...]).astype(bf16), w_ref[...])
    mk = kv[:, :D_MEM]
    mv = kv[:, D_MEM:]
    mk_ref[0] = mk
    mv_ref[0] = mv
    mkb_ref[0] = mk.astype(bf16)
    mvb_ref[0] = mv.astype(bf16)


def _memkv(mem, g, w):
    B, M, _ = mem.shape
    blk = lambda: pl.BlockSpec((1, M, D_MEM), lambda b: (b, 0, 0))
    return pl.pallas_call(
        _memkv_kernel,
        grid=(B,),
        in_specs=[pl.BlockSpec((1, M, D_MODEL), lambda b: (b, 0, 0)),
                  pl.BlockSpec((1, D_MODEL), lambda b: (0, 0)),
                  pl.BlockSpec((D_MODEL, 2 * D_MEM), lambda b: (0, 0))],
        out_specs=[blk(), blk(), blk(), blk()],
        out_shape=[jax.ShapeDtypeStruct((B, M, D_MEM), f32)] * 2
        + [jax.ShapeDtypeStruct((B, M, D_MEM), bf16)] * 2,
        compiler_params=pltpu.CompilerParams(dimension_semantics=("arbitrary",)),
        name="memkv",
    )(mem, g, w)


def _front_common(x_ref, g_ref, w_ref, wt_ref, cos_ref, sa_ref, sb_ref,
                  q_ref, k_ref, v_ref, kb_ref, vt_ref, ga_ref, qi_ref, ki_ref, kk_ref, wit_ref):
    xb = _rms(x_ref[0], g_ref[...]).astype(bf16)
    cos, sa, sb = cos_ref[...], sa_ref[...], sb_ref[...]
    proj = lambda lo, hi: _dot(xb, w_ref[:, lo:hi])
    rope = lambda z: _rope(z, cos, sa, sb)

    zq = proj(P_Q, P_K)
    zqi = proj(P_QIDX, P_KK)
    for s in range(D_ATT // LANES):
        sl = slice(s * LANES, (s + 1) * LANES)
        q_ref[0, :, sl] = (rope(zq[:, sl]) * (QK_SCALE * LOG2E)).astype(bf16)
        qi_ref[0, :, sl] = (rope(zqi[:, sl]) * IDX_SCALE).astype(bf16)
    zkv = proj(P_K, P_GATT)
    zk = rope(zkv[:, :D_KV])
    k_ref[0] = zk
    kb_ref[0] = zk.astype(bf16)
    v_ref[0] = zkv[:, D_KV:]
    ga_ref[0] = _silu(proj(P_GATT, P_QIDX))
    zkk = rope(proj(P_KK, P_U))
    ki_ref[0] = zkk[:, :IDX_DIM]
    kk_ref[0] = zkk.astype(bf16)
    vw_t = _dot_nt(wt_ref[...], xb)
    TK = vt_ref.shape[3]
    for c in range(vt_ref.shape[1]):
        vt_ref[0, c] = vw_t[0:D_KV, c * TK:(c + 1) * TK].astype(bf16)
    wit_ref[0] = vw_t[D_KV:D_KV + N_IDX_HEADS, :] * W_SCALE
    return proj


def _front_prompt_kernel(x_ref, g_ref, w_ref, wt_ref, cos_ref, sa_ref, sb_ref, cw_ref, mkb_ref, mvb_ref,
                         q_ref, k_ref, v_ref, kb_ref, vt_ref, ga_ref, qi_ref, ki_ref, kk_ref, wit_ref,
                         cvg_ref, cst_ref, mg_ref, ubuf):
    TM = x_ref.shape[1]
    proj = _front_common(x_ref, g_ref, w_ref, wt_ref, cos_ref, sa_ref, sb_ref,
                         q_ref, k_ref, v_ref, kb_ref, vt_ref, ga_ref, qi_ref, ki_ref, kk_ref, wit_ref)

    @pl.when(pl.program_id(1) == 0)
    def _():
        ubuf[0:8, :] = jnp.zeros((8, D_CONV), f32)

    u = proj(P_C, P_GCONV) * proj(P_U, P_B)
    ubuf[8:TM + 8, :] = u
    cw = cw_ref[...]
    conv = proj(P_B, P_C) * (cw[0:1] * ubuf[6:TM + 6, :] + cw[1:2] * ubuf[7:TM + 7, :] + cw[2:3] * u)
    cvg_ref[0] = (conv * _silu(proj(P_GCONV, P_QMEM))).astype(bf16)
    tail = u[TM - 2:TM, :]
    ubuf[6:8, :] = tail
    cst_ref[0] = tail

    qm = (proj(P_QMEM, P_GMEM) * QK_SCALE).astype(bf16)
    mkb = mkb_ref[0]
    mvb = mvb_ref[0]
    outs = []
    for h in range(N_MEM_HEADS):
        sl = slice(h * HEAD_DIM, (h + 1) * HEAD_DIM)
        p, l = _softmax_parts(_dot_nt(qm[:, sl], mkb[:, sl]))
        outs.append(_dot(p.astype(bf16), mvb[:, sl]) / l)
    mo = jnp.concatenate(outs, axis=1)
    mg_ref[0] = (mo * _silu(proj(P_GMEM, P_END))).astype(bf16)


def _front_sample_kernel(x_ref, g_ref, w_ref, wt_ref, cos_ref, sa_ref, sb_ref, cw_ref, st0_ref, st1_ref,
                         q_ref, k_ref, v_ref, kb_ref, vt_ref, ga_ref, qi_ref, ki_ref, kk_ref, wit_ref,
                         cvg_ref, u_ref, qm_ref, gm_ref):
    proj = _front_common(x_ref, g_ref, w_ref, wt_ref, cos_ref, sa_ref, sb_ref,
                         q_ref, k_ref, v_ref, kb_ref, vt_ref, ga_ref, qi_ref, ki_ref, kk_ref, wit_ref)
    u = proj(P_C, P_GCONV) * proj(P_U, P_B)
    cw = cw_ref[...]
    conv = proj(P_B, P_C) * (cw[0:1] * st0_ref[...] + cw[1:2] * st1_ref[...] + cw[2:3] * u)
    cvg_ref[0] = (conv * _silu(proj(P_GCONV, P_QMEM))).astype(bf16)
    u_ref[0] = u
    qm_ref[0] = (proj(P_QMEM, P_GMEM) * QK_SCALE).astype(bf16)
    gm_ref[0] = _silu(proj(P_GMEM, P_END))


def _front_out(B, S, TM, TK):
    tok = lambda d: pl.BlockSpec((1, TM, d), lambda b, j: (b, j, 0))
    specs = [tok(D_ATT), tok(D_KV), tok(D_KV), tok(D_KV),
             pl.BlockSpec((1, TM // TK, D_KV, TK), lambda b, j: (b, j, 0, 0)),
             tok(D_ATT), tok(D_QIDX), tok(IDX_DIM), tok(LANES),
             pl.BlockSpec((1, N_IDX_HEADS, TM), lambda b, j: (b, 0, j))]
    sds = lambda d, t: jax.ShapeDtypeStruct((B, S, d), t)
    shapes = [sds(D_ATT, bf16), sds(D_KV, f32), sds(D_KV, f32), sds(D_KV, bf16),
              jax.ShapeDtypeStruct((B, S // TK, D_KV, TK), bf16),
              sds(D_ATT, f32), sds(D_QIDX, bf16), sds(IDX_DIM, f32), sds(LANES, bf16),
              jax.ShapeDtypeStruct((B, N_IDX_HEADS, S), f32)]
    return tok, sds, specs, shapes


def _front_in(TM):
    full = lambda r, c: pl.BlockSpec((r, c), lambda b, j: (0, 0))
    rows = lambda: pl.BlockSpec((TM, LANES), lambda b, j: (j, 0))
    return [pl.BlockSpec((1, TM, D_MODEL), lambda b, j: (b, j, 0)), full(1, D_MODEL), full(D_MODEL, P_END),
            full(D_T, D_MODEL), rows(), rows(), rows(), full(CONV_WIDTH, D_CONV)]


def _front_prompt(x, g, w, wt, cos, sa, sb, cw, mkb, mvb, TM, TK):
    B, S, _ = x.shape
    M = mkb.shape[1]
    tok, sds, specs, shapes = _front_out(B, S, TM, TK)
    mem = lambda: pl.BlockSpec((1, M, D_MEM), lambda b, j: (b, 0, 0))
    return pl.pallas_call(
        _front_prompt_kernel,
        grid=(B, S // TM),
        in_specs=_front_in(TM) + [mem(), mem()],
        out_specs=specs + [tok(D_CONV), pl.BlockSpec((1, CONV_WIDTH - 1, D_CONV), lambda b, j: (b, 0, 0)),
                           tok(D_MEM)],
        out_shape=shapes + [sds(D_CONV, bf16), jax.ShapeDtypeStruct((B, CONV_WIDTH - 1, D_CONV), f32),
                            sds(D_MEM, bf16)],
        scratch_shapes=[pltpu.VMEM((TM + 8, D_CONV), f32)],
        compiler_params=pltpu.CompilerParams(dimension_semantics=("arbitrary", "arbitrary"),
                                             vmem_limit_bytes=VMEM_LIMIT),
        name="front_prompt",
    )(x, g, w, wt, cos, sa, sb, cw, mkb, mvb)


def _front_sample(x, g, w, wt, cos, sa, sb, cw, st0, st1):
    _, T, _ = x.shape
    tok, sds, specs, shapes = _front_out(1, T, T, T)
    st = lambda: pl.BlockSpec((T, D_CONV), lambda b, j: (0, 0))
    return pl.pallas_call(
        _front_sample_kernel,
        grid=(1, 1),
        in_specs=_front_in(T) + [st(), st()],
        out_specs=specs + [tok(D_CONV), tok(D_CONV), tok(D_MEM), tok(D_MEM)],
        out_shape=shapes + [sds(D_CONV, bf16), sds(D_CONV, f32), sds(D_MEM, bf16), sds(D_MEM, f32)],
        compiler_params=pltpu.CompilerParams(dimension_semantics=("arbitrary", "arbitrary"),
                                             vmem_limit_bytes=VMEM_LIMIT),
        name="front_sample",
    )(x, g, w, wt, cos, sa, sb, cw, st0, st1)


def _half_masked(slab, half):
    lane = lax.broadcasted_iota(jnp.int32, slab.shape, 1)
    keep = (lane < HEAD_DIM) if half == 0 else (lane >= HEAD_DIM)
    return jnp.where(keep, slab, jnp.zeros((), slab.dtype))


def _attn_prompt_body(q_ref, qi_ref, wit_ref, kb_ref, vt_ref, kk_ref, ga_ref, cvg_ref, mg_ref, x_ref,
                      wo_ref, gp_ref, y_ref, sc_ref, bias_ref, acc_ref):
    TQ = q_ref.shape[1]
    TK = TQ
    S = kb_ref.shape[1]
    i = pl.program_id(1)
    chunk = lambda c: pl.ds(pl.multiple_of(c * TK, TK), TK)
    slab = lambda ref, p: ref[0, :, (p // 2) * LANES:(p // 2 + 1) * LANES]

    blk = lambda a, p: a[:, p * TQ:(p + 1) * TQ]
    qi_all = jnp.concatenate([_half_masked(slab(qi_ref, h), h % 2) for h in range(N_IDX_HEADS)], axis=0)
    wit = wit_ref[0]

    def scores_t(c):
        d = _dot_nt(kk_ref[0, chunk(c), :], qi_all)
        acc = jnp.maximum(blk(d, 0), 0.0) * wit[0:1, :]
        for h in range(1, N_IDX_HEADS):
            acc = acc + jnp.maximum(blk(d, h), 0.0) * wit[h:h + 1, :]
        return acc

    def score_body(j, carry):
        for r in range(2):
            c = jnp.minimum(2 * j + r, i - 1)
            sc_ref[c] = scores_t(c)
        return carry

    lax.fori_loop(0, lax.shift_right_logical(i + 1, 1), score_body, 0)
    key_pos = lax.broadcasted_iota(jnp.int32, (TK, TQ), 0)
    qry_pos = lax.broadcasted_iota(jnp.int32, (TK, TQ), 1)
    sc_ref[i] = jnp.where(key_pos <= qry_pos, scores_t(i), -jnp.inf)

    t1 = i * TQ + 1 + lax.broadcasted_iota(jnp.int32, (1, TQ), 1)
    k_eff = jnp.minimum(t1, min(TOPK_MAX, S // 4)).astype(f32)
    _select_bias_t(sc_ref, bias_ref, i + 1, k_eff, t1.astype(f32))

    q_all = jnp.concatenate([_half_masked(slab(q_ref, p), p % 2) for p in range(N_ATT_HEADS)], axis=0)
    acc_ref[...] = jnp.zeros(acc_ref.shape, f32)

    def att_chunk(c_kv, c_bias, ms, ls):
        bias = bias_ref[c_bias]
        s_all = _dot_nt(kb_ref[0, chunk(c_kv), :], q_all)
        ms_new, ls_new, alphas, pts = [], [], [], []
        for p in range(N_ATT_HEADS):
            s = blk(s_all, p) + bias
            m_new = jnp.maximum(ms[p], jnp.max(s, axis=0, keepdims=True))
            alpha = jnp.exp2(ms[p] - m_new)
            pt = jnp.exp2(s - m_new)
            ls_new.append(alpha * ls[p] + jnp.sum(pt, axis=0, keepdims=True))
            ms_new.append(m_new)
            alphas.append(alpha)
            pts.append(pt.astype(bf16))
        pv_all = _dot(vt_ref[0, c_kv], jnp.concatenate(pts, axis=1))
        for p in range(N_ATT_HEADS):
            rows = slice(p * HEAD_DIM, (p + 1) * HEAD_DIM)
            g0 = (p % 2) * HEAD_DIM
            acc_ref[rows, :] = alphas[p] * acc_ref[rows, :] + blk(pv_all, p)[g0:g0 + HEAD_DIM, :]
        return tuple(ms_new), tuple(ls_new)

    def att_body(j, carry):
        return att_chunk(2 * j + 1, 2 * j + 1, *att_chunk(2 * j, 2 * j, *carry))

    init = (tuple(jnp.full((1, TQ), NEG, f32) for _ in range(N_ATT_HEADS)),
            tuple(jnp.zeros((1, TQ), f32) for _ in range(N_ATT_HEADS)))
    carry = lax.fori_loop(0, lax.shift_right_logical(i + 1, 1), att_body, init)
    _, ls = lax.cond((i + 1) % 2 == 1, lambda ms, ls: att_chunk(i, i, ms, ls), lambda ms, ls: (ms, ls), *carry)
    for p in range(N_ATT_HEADS):
        rows = slice(p * HEAD_DIM, (p + 1) * HEAD_DIM)
        acc_ref[rows, :] = acc_ref[rows, :] / ls[p]
    att_g = (acc_ref[...].T * ga_ref[0]).astype(bf16)
    y_ref[0] = _merge(att_g, cvg_ref[0], mg_ref[0], x_ref[0], wo_ref, gp_ref[...])


N_PROMPT_IN = 12


def _attn_prompt_kernel(*refs):
    _attn_prompt_body(*refs)


def _attn_fused_kernel(pt_ref, *refs):
    prompt_in, refs = refs[:N_PROMPT_IN], refs[N_PROMPT_IN:]
    qbd_ref, sbias_ref, knew_ref, vnew_ref, qmbd_ref, cmk_ref, cmv_ref, k_hbm, v_hbm = refs[:9]
    y_ref, o_ref, mo_ref, sc_ref, bias_ref, acc_ref, kbuf, vbuf, semk, semv = refs[9:]
    s = pl.program_id(0) * pl.num_programs(1) + pl.program_id(1)
    ns = pl.num_programs(0) * pl.num_programs(1)
    _sample_prefetch(pt_ref, s, ns, k_hbm, v_hbm, kbuf, vbuf, semk, semv)
    _attn_prompt_body(*prompt_in, y_ref, sc_ref, bias_ref, acc_ref)
    _sample_attend(pt_ref, s, qbd_ref, sbias_ref, knew_ref, vnew_ref, qmbd_ref, cmk_ref, cmv_ref, k_hbm, v_hbm,
                   o_ref, mo_ref, kbuf, vbuf, semk, semv)


def _attn_prompt(q, qi, wit, kb, vt, kk, ga, cvg, mg, x, wo, gp, TQ, sample=None):
    B, S, _ = x.shape
    NC = S // TQ
    imap = lambda f: (lambda b, i, *_: f(b, i))
    tok = lambda d: pl.BlockSpec((1, TQ, d), imap(lambda b, i: (b, i, 0)))
    seq = lambda d: pl.BlockSpec((1, S, d), imap(lambda b, i: (b, 0, 0)))
    full = lambda r, c: pl.BlockSpec((r, c), imap(lambda b, i: (0, 0)))
    in_specs = [tok(D_ATT), tok(D_QIDX), pl.BlockSpec((1, N_IDX_HEADS, TQ), imap(lambda b, i: (b, 0, i))),
                seq(D_KV), pl.BlockSpec((1, NC, D_KV, TQ), imap(lambda b, i: (b, 0, 0, 0))), seq(LANES),
                tok(D_ATT), tok(D_CONV), tok(D_MEM), tok(D_MODEL), full(D_MODEL, D_MODEL), full(1, D_MODEL)]
    assert len(in_specs) == N_PROMPT_IN
    scratch = [pltpu.VMEM((NC, TQ, TQ), f32), pltpu.VMEM((NC, TQ, TQ), f32), pltpu.VMEM((D_ATT, TQ), f32)]
    params = pltpu.CompilerParams(dimension_semantics=("arbitrary", "arbitrary"), vmem_limit_bytes=VMEM_LIMIT)
    y_shape = jax.ShapeDtypeStruct((B, S, D_MODEL), f32)
    if sample is None:
        return pl.pallas_call(
            _attn_prompt_kernel, grid=(B, NC), in_specs=in_specs, out_specs=tok(D_MODEL), out_shape=y_shape,
            scratch_shapes=scratch, compiler_params=params, name="attn_prompt",
        )(q, qi, wit, kb, vt, kk, ga, cvg, mg, x, wo, gp)
    pt, *sample_ops = sample
    s_in, s_out, s_shapes, s_scratch = _sattn_specs(pt, *sample_ops, index=lambda b, i: b * NC + i)
    return pl.pallas_call(
        _attn_fused_kernel,
        grid_spec=pltpu.PrefetchScalarGridSpec(
            num_scalar_prefetch=1, grid=(B, NC), in_specs=in_specs + s_in, out_specs=[tok(D_MODEL)] + s_out,
            scratch_shapes=scratch + s_scratch),
        out_shape=[y_shape] + s_shapes, compiler_params=params, name="attn_prompt",
    )(pt, q, qi, wit, kb, vt, kk, ga, cvg, mg, x, wo, gp, *sample_ops)


def _paged_copies(pt_ref, b, hbm_ref, buf, sem, slot, n_pages, page):
    return [pltpu.make_async_copy(hbm_ref.at[pt_ref[b, p]], buf.at[slot, :, pl.ds(p * page, page)], sem.at[slot])
            for p in range(n_pages)]


def _sidx_kernel(pt_ref, qi_ref, wi_ref, knew_ref, kidx_hbm, sc_ref, dn_ref, buf, sem):
    b = pl.program_id(0)
    nb = pl.num_programs(0)
    n_pages = pt_ref.shape[1]
    page = kidx_hbm.shape[2]
    copies = lambda bb, slot: _paged_copies(pt_ref, bb, kidx_hbm, buf, sem, slot, n_pages, page)
    slot = lax.rem(b, 2)

    @pl.when(b == 0)
    def _():
        for p, cp in enumerate(copies(0, 0)):
            cp.start(priority=p % 2)

    @pl.when(b + 1 < nb)
    def _():
        for p, cp in enumerate(copies(b + 1, 1 - slot)):
            cp.start(priority=p % 2)

    for cp in copies(b, slot):
        cp.wait()

    qi = qi_ref[0]
    wi = wi_ref[0]
    weigh = lambda d: jnp.sum(jnp.maximum(d, 0.0) * wi, axis=0, keepdims=True)
    sub = 1024
    for j in range(n_pages * page // sub):
        sc_ref[0, :, j * sub:(j + 1) * sub] = weigh(_dot(qi, buf[slot, :, j * sub:(j + 1) * sub].astype(bf16)))
    dn_ref[0] = weigh(_dot_nt(qi, knew_ref[...]))


def _sidx(pt, qi8, wi8, knew, kidx_t_pool):
    nb, n_pages = pt.shape
    page = kidx_t_pool.shape[2]
    past = n_pages * page
    return pl.pallas_call(
        _sidx_kernel,
        grid_spec=pltpu.PrefetchScalarGridSpec(
            num_scalar_prefetch=1,
            grid=(nb,),
            in_specs=[pl.BlockSpec((1, N_IDX_HEADS, IDX_DIM), lambda b, pt: (b, 0, 0)),
                      pl.BlockSpec((1, N_IDX_HEADS, 1), lambda b, pt: (b, 0, 0)),
                      pl.BlockSpec((nb, IDX_DIM), lambda b, pt: (0, 0)),
                      pl.BlockSpec(memory_space=pl.ANY)],
            out_specs=[pl.BlockSpec((1, 1, past), lambda b, pt: (b, 0, 0)),
                       pl.BlockSpec((1, 1, nb), lambda b, pt: (b, 0, 0))],
            scratch_shapes=[pltpu.VMEM((2, IDX_DIM, past), f32), pltpu.SemaphoreType.DMA((2,))]),
        out_shape=[jax.ShapeDtypeStruct((nb, 1, past), f32), jax.ShapeDtypeStruct((nb, 1, nb), f32)],
        compiler_params=pltpu.CompilerParams(dimension_semantics=("arbitrary",), vmem_limit_bytes=VMEM_LIMIT),
        name="sample_indexer",
    )(pt, qi8, wi8, knew, kidx_t_pool)


def _ssel_kernel(sct_ref, dn_ref, out_ref, sc_ref, bias_ref):
    nc, TK, Q = sc_ref.shape
    for c in range(nc - 1):
        sc_ref[c] = sct_ref[c * TK:(c + 1) * TK, :]
    row = lax.broadcasted_iota(jnp.int32, (TK, Q), 0)
    col = lax.broadcasted_iota(jnp.int32, (TK, Q), 1)
    sc_ref[nc - 1] = jnp.where(row == col, dn_ref[...], -jnp.inf)
    n_valid = (nc - 1) * TK + 1
    k_eff = jnp.full((1, Q), float(min(TOPK_MAX, n_valid // 4)), f32)
    _select_bias_t(sc_ref, bias_ref, nc, k_eff, jnp.full((1, Q), float(n_valid), f32))
    for c in range(nc):
        out_ref[c * TK:(c + 1) * TK, :] = bias_ref[c]


def _ssel(sc_t, dn_t):
    past, nb = sc_t.shape
    nc = past // nb + 1
    return pl.pallas_call(
        _ssel_kernel,
        out_shape=jax.ShapeDtypeStruct((past + nb, nb), f32),
        scratch_shapes=[pltpu.VMEM((nc, nb, nb), f32), pltpu.VMEM((nc, nb, nb), f32)],
        compiler_params=pltpu.CompilerParams(vmem_limit_bytes=VMEM_LIMIT),
        name="sample_select",
    )(sc_t, dn_t)


def _sample_copies(pt_ref, s, slot, k_hbm, v_hbm, kbuf, vbuf, semk, semv):
    n_pages, page = pt_ref.shape[1], k_hbm.shape[2]
    return (_paged_copies(pt_ref, s, k_hbm, kbuf, semk, slot, n_pages, page)
            + _paged_copies(pt_ref, s, v_hbm, vbuf, semv, slot, n_pages, page))


def _sample_prefetch(pt_ref, s, ns, k_hbm, v_hbm, kbuf, vbuf, semk, semv):
    slot = lax.rem(s, 2)

    @pl.when(s == 0)
    def _():
        for cp in _sample_copies(pt_ref, 0, 0, k_hbm, v_hbm, kbuf, vbuf, semk, semv):
            cp.start()

    @pl.when(s + 1 < ns)
    def _():
        for cp in _sample_copies(pt_ref, s + 1, 1 - slot, k_hbm, v_hbm, kbuf, vbuf, semk, semv):
            cp.start()


def _sample_attend(pt_ref, s, qbd_ref, bias_ref, knew_ref, vnew_ref, qmbd_ref, cmk_ref, cmv_ref, k_hbm, v_hbm,
                   o_ref, mo_ref, kbuf, vbuf, semk, semv):
    past = kbuf.shape[2]
    nb = knew_ref.shape[0]
    slot = lax.rem(s, 2)

    qm = qmbd_ref[0]
    pm, lm = _softmax_parts(_dot(qm, cmk_ref[0].astype(bf16)))
    mo = _dot_nt(pm.astype(bf16), cmv_ref[0].astype(bf16)) / lm
    for h in range(N_MEM_HEADS):
        hs = slice(h * HEAD_DIM, (h + 1) * HEAD_DIM)
        mo_ref[0, :, hs] = mo[h:h + 1, hs]

    for cp in _sample_copies(pt_ref, s, slot, k_hbm, v_hbm, kbuf, vbuf, semk, semv):
        cp.wait()

    qbd = qbd_ref[0]

    def softmax_part(logits, value_product):
        m_j = jnp.max(logits, axis=1, keepdims=True)
        p_j = jnp.exp2(logits - m_j)
        return m_j, jnp.sum(p_j, axis=1, keepdims=True), value_product(p_j.astype(bf16))

    seg = min(SAMPLE_SEG, past)
    parts = []
    for j in range(past // seg):
        ks = slice(j * seg, (j + 1) * seg)
        parts.append(softmax_part(_dot(qbd, kbuf[slot, :, ks].astype(bf16)) + bias_ref[0, :, ks],
                                  lambda p_j, ks=ks: _dot_nt(p_j, vbuf[slot, :, ks].astype(bf16))))
    parts.append(softmax_part(_dot_nt(qbd, knew_ref[...]) + bias_ref[0, :, past:past + nb],
                              lambda p_j: _dot(p_j, vnew_ref[...])))
    m = functools.reduce(jnp.maximum, [m_j for m_j, _, _ in parts])
    scales = [jnp.exp2(m_j - m) for m_j, _, _ in parts]
    l = sum(l_j * c for (_, l_j, _), c in zip(parts, scales))
    o = sum(o_j * c for (_, _, o_j), c in zip(parts, scales)) / l
    for p in range(N_ATT_HEADS):
        g0 = (p % 2) * HEAD_DIM
        o_ref[0, :, p * HEAD_DIM:(p + 1) * HEAD_DIM] = o[p:p + 1, g0:g0 + HEAD_DIM]


def _sattn_kernel(pt_ref, qbd_ref, bias_ref, knew_ref, vnew_ref, qmbd_ref, cmk_ref, cmv_ref, k_hbm, v_hbm,
                  o_ref, mo_ref, kbuf, vbuf, semk, semv):
    s = pl.program_id(0)
    _sample_prefetch(pt_ref, s, pl.num_programs(0), k_hbm, v_hbm, kbuf, vbuf, semk, semv)
    _sample_attend(pt_ref, s, qbd_ref, bias_ref, knew_ref, vnew_ref, qmbd_ref, cmk_ref, cmv_ref, k_hbm, v_hbm,
                   o_ref, mo_ref, kbuf, vbuf, semk, semv)


def _sattn_specs(pt, qbd, bias, knew, vnew, qmbd, cmk_t, cmv_t, k_t_pool, v_t_pool, index):
    nb, n_pages = pt.shape
    past = n_pages * k_t_pool.shape[2]
    M = cmk_t.shape[2]
    per = lambda r, c: pl.BlockSpec((1, r, c), lambda *g: (index(*g[:-1]), 0, 0))
    full = lambda r, c: pl.BlockSpec((r, c), lambda *g: (0, 0))
    in_specs = [per(N_ATT_HEADS, D_KV), per(1, past + nb), full(nb, D_KV), full(nb, D_KV),
                per(N_MEM_HEADS, D_MEM), per(D_MEM, M), per(D_MEM, M),
                pl.BlockSpec(memory_space=pl.ANY), pl.BlockSpec(memory_space=pl.ANY)]
    out_specs = [per(1, D_ATT), per(1, D_MEM)]
    out_shapes = [jax.ShapeDtypeStruct((nb, 1, D_ATT), f32), jax.ShapeDtypeStruct((nb, 1, D_MEM), f32)]
    scratch = [pltpu.VMEM((2, D_KV, past), f32), pltpu.VMEM((2, D_KV, past), f32),
               pltpu.SemaphoreType.DMA((2,)), pltpu.SemaphoreType.DMA((2,))]
    return in_specs, out_specs, out_shapes, scratch


def _sattn(pt, *ops):
    in_specs, out_specs, out_shapes, scratch = _sattn_specs(pt, *ops, index=lambda b: b)
    return pl.pallas_call(
        _sattn_kernel,
        grid_spec=pltpu.PrefetchScalarGridSpec(num_scalar_prefetch=1, grid=(pt.shape[0],), in_specs=in_specs,
                                               out_specs=out_specs, scratch_shapes=scratch),
        out_shape=out_shapes,
        compiler_params=pltpu.CompilerParams(dimension_semantics=("arbitrary",), vmem_limit_bytes=VMEM_LIMIT),
        name="sample_attention",
    )(pt, *ops)


def _smerge_kernel(att_ref, ga_ref, cvg_ref, mo_ref, gm_ref, x_ref, wo_ref, gp_ref, y_ref):
    att_g = (att_ref[...] * ga_ref[...]).astype(bf16)
    mg = (mo_ref[...] * gm_ref[...]).astype(bf16)
    y_ref[...] = _merge(att_g, cvg_ref[...], mg, x_ref[...], wo_ref, gp_ref[...])


def _smerge(att, ga, cvg, mo, gm, x, wo, gp):
    return pl.pallas_call(
        _smerge_kernel,
        out_shape=jax.ShapeDtypeStruct(x.shape, f32),
        compiler_params=pltpu.CompilerParams(vmem_limit_bytes=VMEM_LIMIT),
        name="sample_merge",
    )(att, ga, cvg, mo, gm, x, wo, gp)


def _rope_tables(pos):
    half = HEAD_DIM // 2
    inv = ROPE_THETA ** (-jnp.arange(half, dtype=f32) / half)
    ang = pos.astype(f32)[:, None] * inv[None, :]
    cos, sin = jnp.cos(ang), jnp.sin(ang)
    zero = jnp.zeros_like(sin)
    rep = LANES // HEAD_DIM
    return (jnp.tile(jnp.concatenate([cos, cos], axis=1), (1, rep)),
            jnp.tile(jnp.concatenate([-sin, zero], axis=1), (1, rep)),
            jnp.tile(jnp.concatenate([zero, sin], axis=1), (1, rep)))


def _pack_w_in(w):
    head = lambda base: jnp.concatenate([w[:, base + h * HEAD_DIM:base + (h + 1) * HEAD_DIM] for h in ATT_ORDER], 1)
    kidx = w[:, R_KIDX:R_WIDX]
    packed = jnp.concatenate([head(R_Q), w[:, R_K:R_GATT], head(R_GATT), w[:, R_QIDX:R_KIDX], kidx, kidx,
                              w[:, R_U:]], axis=1)
    w_t = jnp.concatenate([w[:, R_V:R_GATT], w[:, R_WIDX:R_U], jnp.zeros((D_MODEL, D_T - D_KV - N_IDX_HEADS), w.dtype)],
                          axis=1).T
    return packed.astype(bf16), w_t.astype(bf16)


def _pack_w_out(w):
    att = jnp.concatenate([w[h * HEAD_DIM:(h + 1) * HEAD_DIM] for h in ATT_ORDER], axis=0)
    return jnp.concatenate([att, w[D_ATT:]], axis=0).astype(bf16)


def _slab_rows(x, n_rows):
    T = x.shape[0]
    slabs = jnp.repeat(x.reshape(T, n_rows // 2, 1, 2, HEAD_DIM), 2, axis=2)
    keep = jnp.arange(2)[:, None] == jnp.arange(2)[None, :]
    return jnp.where(keep[None, None, :, :, None], slabs, jnp.zeros((), x.dtype)).reshape(T, n_rows, LANES)


def _diag_rows(x, n_heads):
    T = x.shape[0]
    keep = jnp.arange(n_heads)[:, None] == jnp.arange(n_heads)[None, :]
    xh = x.reshape(T, 1, n_heads, HEAD_DIM)
    return jnp.where(keep[None, :, :, None], xh, jnp.zeros((), x.dtype)).reshape(T, n_heads, n_heads * HEAD_DIM)


def _layer(l, xp, xs, mem_prompt, cache_k, cache_v, cache_kidx, cache_mem_k, cache_mem_v, state_conv,
           page_table, g_pre, w_in, conv_w, g_memnorm, w_mem_kv, w_out, g_post):
    B, S, _ = xp.shape
    nb, T, _ = xs.shape
    assert T == 1
    n_pages = page_table.shape[1]
    n_pool, page = cache_k.shape[1], cache_k.shape[2]
    past = n_pages * page
    M = mem_prompt.shape[1]
    assert page == nb and past % 1024 == 0 and past % min(SAMPLE_SEG, past) == 0

    row = lambda a: a[l][None, :]
    w_pack, w_t = _pack_w_in(w_in[l])
    wo = _pack_w_out(w_out[l])
    gp = row(g_post)

    mk, mv, mkb, mvb = _memkv(mem_prompt, row(g_memnorm), w_mem_kv[l].astype(bf16))
    cos, sa, sb = _rope_tables(jnp.arange(S, dtype=jnp.int32))
    TQ = min(256, S)
    TM = min(512, S)
    (q, k, v, kb, vt, ga, qi, ki, kk, wit, cvg, cst, mg) = _front_prompt(
        xp, row(g_pre), w_pack, w_t, cos, sa, sb, conv_w[l], mkb, mvb, TM, TQ)

    cos, sa, sb = _rope_tables(jnp.full((nb,), past, jnp.int32))
    st = state_conv[l]
    (q_s, k_s, v_s, kb_s, vt_s, ga_s, qi_s, ki_s, kk_s, wit_s, cvg_s, u_s, qm_s, gm_s) = _front_sample(
        xs.reshape(1, nb, D_MODEL), row(g_pre), w_pack, w_t, cos, sa, sb, conv_w[l], st[:, 0, :], st[:, 1, :])
    kidx_t = jnp.swapaxes(cache_kidx[l], 1, 2)
    k_t = jnp.transpose(cache_k[l], (0, 2, 3, 1)).reshape(n_pool, D_KV, page)
    v_t = jnp.transpose(cache_v[l], (0, 2, 3, 1)).reshape(n_pool, D_KV, page)
    cmk_t = jnp.transpose(cache_mem_k[l], (0, 2, 3, 1)).reshape(nb, D_MEM, M)
    cmv_t = jnp.transpose(cache_mem_v[l], (0, 2, 3, 1)).reshape(nb, D_MEM, M)

    sc, dn = _sidx(page_table, qi_s.reshape(nb, N_IDX_HEADS, IDX_DIM), wit_s[0].T.reshape(nb, N_IDX_HEADS, 1),
                   kk_s[0, :, :IDX_DIM], kidx_t)
    bias_t = _ssel(sc.reshape(nb, past).T, dn.reshape(nb, nb).T)
    sample_ops = (page_table, _slab_rows(q_s[0], N_ATT_HEADS), bias_t.T.reshape(nb, 1, past + nb),
                  kb_s[0], vt_s[0, 0].T, _diag_rows(qm_s[0], N_MEM_HEADS), cmk_t, cmv_t, k_t, v_t)

    prompt_ops = (q, qi, wit, kb, vt, kk, ga, cvg, mg, xp, wo, gp, TQ)
    if nb == B * (S // TQ):
        yp, att_s, mo_s = _attn_prompt(*prompt_ops, sample=sample_ops)
    else:
        yp = _attn_prompt(*prompt_ops)
        att_s, mo_s = _sattn(*sample_ops)
    ys = _smerge(att_s.reshape(nb, D_ATT), ga_s[0], cvg_s[0], mo_s.reshape(nb, D_MEM), gm_s[0],
                 xs.reshape(nb, D_MODEL), wo, gp)

    outs_p = (k.reshape(B, S, N_KV_HEADS, HEAD_DIM), v.reshape(B, S, N_KV_HEADS, HEAD_DIM), ki, cst,
              mk.reshape(B, M, N_MEM_HEADS, HEAD_DIM), mv.reshape(B, M, N_MEM_HEADS, HEAD_DIM))
    outs_s = (k_s.reshape(nb, 1, N_KV_HEADS, HEAD_DIM), v_s.reshape(nb, 1, N_KV_HEADS, HEAD_DIM),
              ki_s.reshape(nb, 1, IDX_DIM), jnp.stack([st[:, 1, :], u_s[0]], axis=1))
    return yp, ys.reshape(nb, 1, D_MODEL), outs_p, outs_s


def kernel(x_prompt, x_sample, mem_prompt, cache_k, cache_v, cache_kidx, cache_mem_k, cache_mem_v, state_conv,
           page_table, g_pre, w_in, conv_w, g_memnorm, w_mem_kv, w_out, g_post):
    depth = w_in.shape[0]
    xp, xs = x_prompt, x_sample
    per_p, per_s = [], []
    for l in range(depth):
        xp, xs, op, os_ = _layer(l, xp, xs, mem_prompt, cache_k, cache_v, cache_kidx, cache_mem_k, cache_mem_v,
                                 state_conv, page_table, g_pre, w_in, conv_w, g_memnorm, w_mem_kv, w_out, g_post)
        per_p.append(op)
        per_s.append(os_)
    stack = lambda outs, i: jnp.stack([o[i] for o in outs])
    return ((xp, xs) + tuple(stack(per_p, i) for i in range(6)) + tuple(stack(per_s, i) for i in range(4)))
```

```python
import functools

import jax
import jax.numpy as jnp
from jax import lax
from jax.experimental import pallas as pl
from jax.experimental.pallas import tpu as pltpu

D_MODEL = 1024
HEAD_DIM = 64
N_MEM_HEADS = 4
D_MEM = N_MEM_HEADS * HEAD_DIM
D_ATT = 512
N_ATT_HEADS = 8
N_KV_HEADS = 2
D_KV = N_KV_HEADS * HEAD_DIM
D_CONV = 256
N_IDX_HEADS = 8
IDX_DIM = 64
D_QIDX = N_IDX_HEADS * IDX_DIM
TOPK_MAX = 256
CONV_WIDTH = 3
ROPE_THETA = 10000.0
RMS_EPS = 1e-6
LANES = 128

R_Q = 0
R_K = R_Q + D_ATT
R_V = R_K + D_KV
R_GATT = R_V + D_KV
R_QIDX = R_GATT + D_ATT
R_KIDX = R_QIDX + D_QIDX
R_WIDX = R_KIDX + IDX_DIM
R_U = R_WIDX + N_IDX_HEADS

ATT_ORDER = (0, 4, 1, 5, 2, 6, 3, 7)
P_Q = 0
P_K = P_Q + D_ATT
P_V = P_K + D_KV
P_GATT = P_V + D_KV
P_QIDX = P_GATT + D_ATT
P_KK = P_QIDX + D_QIDX
P_U = P_KK + LANES
P_B = P_U + D_CONV
P_C = P_B + D_CONV
P_GCONV = P_C + D_CONV
P_QMEM = P_GCONV + D_CONV
P_GMEM = P_QMEM + D_MEM
P_END = P_GMEM + D_MEM

QK_SCALE = HEAD_DIM ** -0.5
LOG2E = 1.4426950408889634
IDX_SCALE = IDX_DIM ** -0.5
W_SCALE = N_IDX_HEADS ** -0.5

F32_MAX = 3.4028234663852886e38
SAMPLE_SEG = 4096
FIXED_PASSES = 16
KEY_PASS_EVERY = 8
MAX_PASSES = 8 * 34
NEG = -1e30
VMEM_LIMIT = 56 * 1024 * 1024
ACC_ROWS = 32
D_T = D_KV + 16

bf16 = jnp.bfloat16
f32 = jnp.float32
NT = (((1,), (1,)), ((), ()))


def _dot(a, b):
    return jnp.dot(a, b, preferred_element_type=f32)


def _dot_nt(a, b):
    return lax.dot_general(a, b, NT, preferred_element_type=f32)


def _rms(x, g):
    return x * lax.rsqrt(jnp.mean(x * x, axis=-1, keepdims=True) + RMS_EPS) * g


def _silu(x):
    return x / (1.0 + jnp.exp(-x))


def _rope(z, cos, sa, sb):
    return z * cos + pltpu.roll(z, 96, 1) * sa + pltpu.roll(z, 32, 1) * sb


def _key_to_f32(k):
    return lax.bitcast_convert_type(k ^ ((k >> 31) & jnp.int32(0x7FFFFFFF)), f32)


def _softmax_parts(s):
    m = jnp.max(s, axis=-1, keepdims=True)
    p = jnp.exp(s - m)
    return p, jnp.sum(p, axis=-1, keepdims=True)


def _f32_to_key(x):
    bits = lax.bitcast_convert_type(x, jnp.int32)
    return bits ^ ((bits >> 31) & jnp.int32(0x7FFFFFFF))


def _select_bias_t(sc_ref, bias_ref, nc, k_eff, n_valid):
    _, TK, Q = sc_ref.shape
    groups = -(-TOPK_MAX // TK)

    def count_ge(thr):
        thr_b = jnp.broadcast_to(thr, (ACC_ROWS, Q))

        def body(c, acc):
            for j in range(TK // ACC_ROWS):
                acc = acc + jnp.where(sc_ref[c, j * ACC_ROWS:(j + 1) * ACC_ROWS, :] >= thr_b, 1.0, 0.0)
            return acc

        acc = lax.fori_loop(0, nc, body, jnp.zeros((ACC_ROWS, Q), f32))
        return jnp.sum(acc, axis=0, keepdims=True)

    if groups == 1:
        bias_ref[0] = sc_ref[0]

        def gmax_body(c, carry):
            bias_ref[0] = jnp.maximum(bias_ref[0], sc_ref[c])
            return carry

        lax.fori_loop(1, nc, gmax_body, 0)
    else:
        for c in range(nc):
            bias_ref[c % groups] = sc_ref[c] if c < groups else jnp.maximum(bias_ref[c % groups], sc_ref[c])
    gmaxs = [bias_ref[g] for g in range(groups)]
    low = jnp.min(functools.reduce(jnp.minimum, gmaxs), axis=0, keepdims=True)
    top = jnp.max(functools.reduce(jnp.maximum, gmaxs), axis=0, keepdims=True)

    low_b = jnp.broadcast_to(low, (ACC_ROWS, Q))

    def start_body(c, accs):
        a_low, a_ge0, a_gt0 = accs
        for j in range(TK // ACC_ROWS):
            x = sc_ref[c, j * ACC_ROWS:(j + 1) * ACC_ROWS, :]
            a_low = a_low + jnp.where(x >= low_b, 1.0, 0.0)
            a_ge0 = a_ge0 + jnp.where(x >= 0.0, 1.0, 0.0)
            a_gt0 = a_gt0 + jnp.where(x > 0.0, 1.0, 0.0)
        return a_low, a_ge0, a_gt0

    zeros = jnp.zeros((ACC_ROWS, Q), f32)
    c_low, c_ge0, c_gt0 = [jnp.sum(a, axis=0, keepdims=True)
                           for a in lax.fori_loop(0, nc, start_body, (zeros, zeros, zeros))]
    all_in = n_valid == k_eff
    at_zero = (c_gt0 < k_eff) & (c_ge0 >= k_eff)
    above = c_gt0 >= k_eff
    low_ok = low > -F32_MAX
    lo = jnp.where(all_in, -F32_MAX, jnp.where(
        at_zero, 0.0, jnp.where(above, jnp.maximum(low, 0.0), jnp.where(low_ok, low, -F32_MAX))))
    c_lo = jnp.where(all_in, n_valid, jnp.where(
        at_zero, c_ge0, jnp.where(above, jnp.where(low > 0.0, c_low, c_ge0), jnp.where(low_ok, c_low, n_valid))))
    hi = jnp.where(above, _key_to_f32(_f32_to_key(top) + 1), 0.0)
    c_hi = jnp.where(at_zero, c_gt0, jnp.where(above, 0.0, c_ge0))

    settled = all_in | at_zero
    halfway = lambda lo, hi: lo + (hi - lo) * 0.5

    def still_open(lo, hi, c_lo):
        mid = halfway(lo, hi)
        return jnp.where(jnp.logical_not(settled) & (c_lo != k_eff) & (mid > lo) & (mid < hi), 1.0, 0.0)

    def one_pass(it, state):
        lo, hi, c_lo, c_hi, w_lo, w_hi, moved = state
        active = still_open(lo, hi, c_lo) > 0.5
        k_lo, k_hi = _f32_to_key(lo), _f32_to_key(hi)
        a = (c_lo - k_eff + 0.5) * w_lo
        b = (k_eff - 0.5 - c_hi) * w_hi
        by_count = lo + (hi - lo) * (a / (a + b))
        by_key = _key_to_f32(k_lo + lax.shift_right_logical(k_hi - k_lo, 1))
        pivot = jnp.where(it % KEY_PASS_EVERY == KEY_PASS_EVERY - 1, by_key, by_count)
        pivot = jnp.where((pivot > lo) & (pivot < hi), pivot, halfway(lo, hi))
        pivot = jnp.where(active, pivot, lo)
        cnt = count_ge(pivot)
        up = active & (cnt >= k_eff)
        down = active & (cnt < k_eff)
        w_hi = jnp.where(up, jnp.where(moved > 0.5, w_hi * 0.5, 1.0), jnp.where(down, 1.0, w_hi))
        w_lo = jnp.where(down, jnp.where(moved < -0.5, w_lo * 0.5, 1.0), jnp.where(up, 1.0, w_lo))
        moved = jnp.where(up, 1.0, jnp.where(down, -1.0, moved))
        lo, c_lo = jnp.where(up, pivot, lo), jnp.where(up, cnt, c_lo)
        hi, c_hi = jnp.where(down, pivot, hi), jnp.where(down, cnt, c_hi)
        return lo, hi, c_lo, c_hi, w_lo, w_hi, moved

    ones = jnp.ones((1, Q), f32)
    state = lax.fori_loop(0, FIXED_PASSES, one_pass, (lo, hi, c_lo, c_hi, ones, ones, jnp.zeros((1, Q), f32)))
    any_open = lambda st: jnp.max(still_open(st[0], st[1], st[2]))

    def more_passes(carry):
        it, state, _ = carry
        state = one_pass(it + 1, one_pass(it, state))
        return it + 2, state, any_open(state)

    _, state, _ = lax.while_loop(lambda carry: (carry[2] > 0.5) & (carry[0] < MAX_PASSES), more_passes,
                                 (jnp.int32(FIXED_PASSES), state, any_open(state)))
    thr, cnt_u, cnt_above = state[0], state[2], state[3]
    has_tie = jnp.max(jnp.where(cnt_u != k_eff, 1.0, 0.0)) > 0.5

    @pl.when(jnp.logical_not(has_tie))
    def _():
        def bias_body(c, carry):
            bias_ref[c] = jnp.where(sc_ref[c] >= thr, 0.0, NEG)
            return carry

        lax.fori_loop(0, nc, bias_body, 0)

    @pl.when(has_tie)
    def _():
        need = jnp.where(cnt_u == k_eff, F32_MAX, k_eff - cnt_above)
        r_i = lax.broadcasted_iota(jnp.int32, (TK, TK), 0)
        c_i = lax.broadcasted_iota(jnp.int32, (TK, TK), 1)
        tri = jnp.where(c_i <= r_i, 1.0, 0.0).astype(bf16)

        def tie_chunk(c, run):
            x = sc_ref[c]
            eq = x == thr
            incl = _dot(tri, jnp.where(eq, 1.0, 0.0).astype(bf16))
            tied = jnp.where(incl < need + 0.5 - run, jnp.where(eq, 0.0, NEG), NEG)
            bias_ref[c] = jnp.where(x > thr, 0.0, tied)
            return run + incl[TK - 1:TK, :]

        def bias_body(j, run):
            return tie_chunk(2 * j + 1, tie_chunk(2 * j, run))

        run = lax.fori_loop(0, nc // 2, bias_body, jnp.zeros((1, Q), f32))
        lax.cond(nc % 2 == 1, lambda r: tie_chunk(nc - 1, r), lambda r: r, run)


def _merge(att_g, cvg, mg, x, wo_ref, gp):
    o = (_dot(att_g, wo_ref[0:D_ATT, :])
         + _dot(cvg, wo_ref[D_ATT:D_ATT + D_CONV, :])
         + _dot(mg, wo_ref[D_ATT + D_CONV:D_MODEL, :]))
    return x + _rms(o, gp)


def _memkv_kernel(mem_ref, g_ref, w_ref, mk_ref, mv_ref, mkb_ref, mvb_ref):
    kv = _dot(_rms(mem_ref[0], g_ref[...]).astype(bf16), w_ref[...])
    mk = kv[:, :D_MEM]
    mv = kv[:, D_MEM:]
    mk_ref[0] = mk
    mv_ref[0] = mv
    mkb_ref[0] = mk.astype(bf16)
    mvb_ref[0] = mv.astype(bf16)


def _memkv(mem, g, w):
    B, M, _ = mem.shape
    blk = lambda: pl.BlockSpec((1, M, D_MEM), lambda b: (b, 0, 0))
    return pl.pallas_call(
        _memkv_kernel,
        grid=(B,),
        in_specs=[pl.BlockSpec((1, M, D_MODEL), lambda b: (b, 0, 0)),
                  pl.BlockSpec((1, D_MODEL), lambda b: (0, 0)),
                  pl.BlockSpec((D_MODEL, 2 * D_MEM), lambda b: (0, 0))],
        out_specs=[blk(), blk(), blk(), blk()],
        out_shape=[jax.ShapeDtypeStruct((B, M, D_MEM), f32)] * 2
        + [jax.ShapeDtypeStruct((B, M, D_MEM), bf16)] * 2,
        compiler_params=pltpu.CompilerParams(dimension_semantics=("arbitrary",)),
        name="memkv",
    )(mem, g, w)


def _front_common(x_ref, g_ref, w_ref, wt_ref, cos_ref, sa_ref, sb_ref,
                  q_ref, k_ref, v_ref, kb_ref, vt_ref, ga_ref, qi_ref, ki_ref, kk_ref, wit_ref):
    xb = _rms(x_ref[0], g_ref[...]).astype(bf16)
    cos, sa, sb = cos_ref[...], sa_ref[...], sb_ref[...]
    proj = lambda lo, hi: _dot(xb, w_ref[:, lo:hi])
    rope = lambda z: _rope(z, cos, sa, sb)

    zq = proj(P_Q, P_K)
    zqi = proj(P_QIDX, P_KK)
    for s in range(D_ATT // LANES):
        sl = slice(s * LANES, (s + 1) * LANES)
        q_ref[0, :, sl] = (rope(zq[:, sl]) * (QK_SCALE * LOG2E)).astype(bf16)
        qi_ref[0, :, sl] = (rope(zqi[:, sl]) * IDX_SCALE).astype(bf16)
    zkv = proj(P_K, P_GATT)
    zk = rope(zkv[:, :D_KV])
    k_ref[0] = zk
    kb_ref[0] = zk.astype(bf16)
    v_ref[0] = zkv[:, D_KV:]
    ga_ref[0] = _silu(proj(P_GATT, P_QIDX))
    zkk = rope(proj(P_KK, P_U))
    ki_ref[0] = zkk[:, :IDX_DIM]
    kk_ref[0] = zkk.astype(bf16)
    vw_t = _dot_nt(wt_ref[...], xb)
    TK = vt_ref.shape[3]
    for c in range(vt_ref.shape[1]):
        vt_ref[0, c] = vw_t[0:D_KV, c * TK:(c + 1) * TK].astype(bf16)
    wit_ref[0] = vw_t[D_KV:D_KV + N_IDX_HEADS, :] * W_SCALE
    return proj


def _front_prompt_kernel(x_ref, g_ref, w_ref, wt_ref, cos_ref, sa_ref, sb_ref, cw_ref, mkb_ref, mvb_ref,
                         q_ref, k_ref, v_ref, kb_ref, vt_ref, ga_ref, qi_ref, ki_ref, kk_ref, wit_ref,
                         cvg_ref, cst_ref, mg_ref, ubuf):
    TM = x_ref.shape[1]
    proj = _front_common(x_ref, g_ref, w_ref, wt_ref, cos_ref, sa_ref, sb_ref,
                         q_ref, k_ref, v_ref, kb_ref, vt_ref, ga_ref, qi_ref, ki_ref, kk_ref, wit_ref)

    @pl.when(pl.program_id(1) == 0)
    def _():
        ubuf[0:8, :] = jnp.zeros((8, D_CONV), f32)

    u = proj(P_C, P_GCONV) * proj(P_U, P_B)
    ubuf[8:TM + 8, :] = u
    cw = cw_ref[...]
    conv = proj(P_B, P_C) * (cw[0:1] * ubuf[6:TM + 6, :] + cw[1:2] * ubuf[7:TM + 7, :] + cw[2:3] * u)
    cvg_ref[0] = (conv * _silu(proj(P_GCONV, P_QMEM))).astype(bf16)
    tail = u[TM - 2:TM, :]
    ubuf[6:8, :] = tail
    cst_ref[0] = tail

    qm = (proj(P_QMEM, P_GMEM) * QK_SCALE).astype(bf16)
    mkb = mkb_ref[0]
    mvb = mvb_ref[0]
    outs = []
    for h in range(N_MEM_HEADS):
        sl = slice(h * HEAD_DIM, (h + 1) * HEAD_DIM)
        p, l = _softmax_parts(_dot_nt(qm[:, sl], mkb[:, sl]))
        outs.append(_dot(p.astype(bf16), mvb[:, sl]) / l)
    mo = jnp.concatenate(outs, axis=1)
    mg_ref[0] = (mo * _silu(proj(P_GMEM, P_END))).astype(bf16)


def _front_sample_kernel(x_ref, g_ref, w_ref, wt_ref, cos_ref, sa_ref, sb_ref, cw_ref, st0_ref, st1_ref,
                         q_ref, k_ref, v_ref, kb_ref, vt_ref, ga_ref, qi_ref, ki_ref, kk_ref, wit_ref,
                         cvg_ref, u_ref, qm_ref, gm_ref):
    proj = _front_common(x_ref, g_ref, w_ref, wt_ref, cos_ref, sa_ref, sb_ref,
                         q_ref, k_ref, v_ref, kb_ref, vt_ref, ga_ref, qi_ref, ki_ref, kk_ref, wit_ref)
    u = proj(P_C, P_GCONV) * proj(P_U, P_B)
    cw = cw_ref[...]
    conv = proj(P_B, P_C) * (cw[0:1] * st0_ref[...] + cw[1:2] * st1_ref[...] + cw[2:3] * u)
    cvg_ref[0] = (conv * _silu(proj(P_GCONV, P_QMEM))).astype(bf16)
    u_ref[0] = u
    qm_ref[0] = (proj(P_QMEM, P_GMEM) * QK_SCALE).astype(bf16)
    gm_ref[0] = _silu(proj(P_GMEM, P_END))


def _front_out(B, S, TM, TK):
    tok = lambda d: pl.BlockSpec((1, TM, d), lambda b, j: (b, j, 0))
    specs = [tok(D_ATT), tok(D_KV), tok(D_KV), tok(D_KV),
             pl.BlockSpec((1, TM // TK, D_KV, TK), lambda b, j: (b, j, 0, 0)),
             tok(D_ATT), tok(D_QIDX), tok(IDX_DIM), tok(LANES),
             pl.BlockSpec((1, N_IDX_HEADS, TM), lambda b, j: (b, 0, j))]
    sds = lambda d, t: jax.ShapeDtypeStruct((B, S, d), t)
    shapes = [sds(D_ATT, bf16), sds(D_KV, f32), sds(D_KV, f32), sds(D_KV, bf16),
              jax.ShapeDtypeStruct((B, S // TK, D_KV, TK), bf16),
              sds(D_ATT, f32), sds(D_QIDX, bf16), sds(IDX_DIM, f32), sds(LANES, bf16),
              jax.ShapeDtypeStruct((B, N_IDX_HEADS, S), f32)]
    return tok, sds, specs, shapes


def _front_in(TM):
    full = lambda r, c: pl.BlockSpec((r, c), lambda b, j: (0, 0))
    rows = lambda: pl.BlockSpec((TM, LANES), lambda b, j: (j, 0))
    return [pl.BlockSpec((1, TM, D_MODEL), lambda b, j: (b, j, 0)), full(1, D_MODEL), full(D_MODEL, P_END),
            full(D_T, D_MODEL), rows(), rows(), rows(), full(CONV_WIDTH, D_CONV)]


def _front_prompt(x, g, w, wt, cos, sa, sb, cw, mkb, mvb, TM, TK):
    B, S, _ = x.shape
    M = mkb.shape[1]
    tok, sds, specs, shapes = _front_out(B, S, TM, TK)
    mem = lambda: pl.BlockSpec((1, M, D_MEM), lambda b, j: (b, 0, 0))
    return pl.pallas_call(
        _front_prompt_kernel,
        grid=(B, S // TM),
        in_specs=_front_in(TM) + [mem(), mem()],
        out_specs=specs + [tok(D_CONV), pl.BlockSpec((1, CONV_WIDTH - 1, D_CONV), lambda b, j: (b, 0, 0)),
                           tok(D_MEM)],
        out_shape=shapes + [sds(D_CONV, bf16), jax.ShapeDtypeStruct((B, CONV_WIDTH - 1, D_CONV), f32),
                            sds(D_MEM, bf16)],
        scratch_shapes=[pltpu.VMEM((TM + 8, D_CONV), f32)],
        compiler_params=pltpu.CompilerParams(dimension_semantics=("arbitrary", "arbitrary"),
                                             vmem_limit_bytes=VMEM_LIMIT),
        name="front_prompt",
    )(x, g, w, wt, cos, sa, sb, cw, mkb, mvb)


def _front_sample(x, g, w, wt, cos, sa, sb, cw, st0, st1):
    _, T, _ = x.shape
    tok, sds, specs, shapes = _front_out(1, T, T, T)
    st = lambda: pl.BlockSpec((T, D_CONV), lambda b, j: (0, 0))
    return pl.pallas_call(
        _front_sample_kernel,
        grid=(1, 1),
        in_specs=_front_in(T) + [st(), st()],
        out_specs=specs + [tok(D_CONV), tok(D_CONV), tok(D_MEM), tok(D_MEM)],
        out_shape=shapes + [sds(D_CONV, bf16), sds(D_CONV, f32), sds(D_MEM, bf16), sds(D_MEM, f32)],
        compiler_params=pltpu.CompilerParams(dimension_semantics=("arbitrary", "arbitrary"),
                                             vmem_limit_bytes=VMEM_LIMIT),
        name="front_sample",
    )(x, g, w, wt, cos, sa, sb, cw, st0, st1)


def _half_masked(slab, half):
    lane = lax.broadcasted_iota(jnp.int32, slab.shape, 1)
    keep = (lane < HEAD_DIM) if half == 0 else (lane >= HEAD_DIM)
    return jnp.where(keep, slab, jnp.zeros((), slab.dtype))


def _attn_prompt_body(q_ref, qi_ref, wit_ref, kb_ref, vt_ref, kk_ref, ga_ref, cvg_ref, mg_ref, x_ref,
                      wo_ref, gp_ref, y_ref, sc_ref, bias_ref, acc_ref):
    TQ = q_ref.shape[1]
    TK = TQ
    S = kb_ref.shape[1]
    i = pl.program_id(1)
    chunk = lambda c: pl.ds(pl.multiple_of(c * TK, TK), TK)
    slab = lambda ref, p: ref[0, :, (p // 2) * LANES:(p // 2 + 1) * LANES]

    blk = lambda a, p: a[:, p * TQ:(p + 1) * TQ]
    qi_all = jnp.concatenate([_half_masked(slab(qi_ref, h), h % 2) for h in range(N_IDX_HEADS)], axis=0)
    wit = wit_ref[0]

    def scores_t(c):
        d = _dot_nt(kk_ref[0, chunk(c), :], qi_all)
        acc = jnp.maximum(blk(d, 0), 0.0) * wit[0:1, :]
        for h in range(1, N_IDX_HEADS):
            acc = acc + jnp.maximum(blk(d, h), 0.0) * wit[h:h + 1, :]
        return acc

    def score_body(j, carry):
        for r in range(2):
            c = jnp.minimum(2 * j + r, i - 1)
            sc_ref[c] = scores_t(c)
        return carry

    lax.fori_loop(0, lax.shift_right_logical(i + 1, 1), score_body, 0)
    key_pos = lax.broadcasted_iota(jnp.int32, (TK, TQ), 0)
    qry_pos = lax.broadcasted_iota(jnp.int32, (TK, TQ), 1)
    sc_ref[i] = jnp.where(key_pos <= qry_pos, scores_t(i), -jnp.inf)

    t1 = i * TQ + 1 + lax.broadcasted_iota(jnp.int32, (1, TQ), 1)
    k_eff = jnp.minimum(t1, min(TOPK_MAX, S // 4)).astype(f32)
    _select_bias_t(sc_ref, bias_ref, i + 1, k_eff, t1.astype(f32))

    q_all = jnp.concatenate([_half_masked(slab(q_ref, p), p % 2) for p in range(N_ATT_HEADS)], axis=0)
    acc_ref[...] = jnp.zeros(acc_ref.shape, f32)

    def att_chunk(c_kv, c_bias, ms, ls):
        bias = bias_ref[c_bias]
        s_all = _dot_nt(kb_ref[0, chunk(c_kv), :], q_all)
        ms_new, ls_new, alphas, pts = [], [], [], []
        for p in range(N_ATT_HEADS):
            s = blk(s_all, p) + bias
            m_new = jnp.maximum(ms[p], jnp.max(s, axis=0, keepdims=True))
            alpha = jnp.exp2(ms[p] - m_new)
            pt = jnp.exp2(s - m_new)
            ls_new.append(alpha * ls[p] + jnp.sum(pt, axis=0, keepdims=True))
            ms_new.append(m_new)
            alphas.append(alpha)
            pts.append(pt.astype(bf16))
        pv_all = _dot(vt_ref[0, c_kv], jnp.concatenate(pts, axis=1))
        for p in range(N_ATT_HEADS):
            rows = slice(p * HEAD_DIM, (p + 1) * HEAD_DIM)
            g0 = (p % 2) * HEAD_DIM
            acc_ref[rows, :] = alphas[p] * acc_ref[rows, :] + blk(pv_all, p)[g0:g0 + HEAD_DIM, :]
        return tuple(ms_new), tuple(ls_new)

    def att_body(j, carry):
        return att_chunk(2 * j + 1, 2 * j + 1, *att_chunk(2 * j, 2 * j, *carry))

    init = (tuple(jnp.full((1, TQ), NEG, f32) for _ in range(N_ATT_HEADS)),
            tuple(jnp.zeros((1, TQ), f32) for _ in range(N_ATT_HEADS)))
    carry = lax.fori_loop(0, lax.shift_right_logical(i + 1, 1), att_body, init)
    _, ls = lax.cond((i + 1) % 2 == 1, lambda ms, ls: att_chunk(i, i, ms, ls), lambda ms, ls: (ms, ls), *carry)
    for p in range(N_ATT_HEADS):
        rows = slice(p * HEAD_DIM, (p + 1) * HEAD_DIM)
        acc_ref[rows, :] = acc_ref[rows, :] / ls[p]
    att_g = (acc_ref[...].T * ga_ref[0]).astype(bf16)
    y_ref[0] = _merge(att_g, cvg_ref[0], mg_ref[0], x_ref[0], wo_ref, gp_ref[...])


N_PROMPT_IN = 12


def _attn_prompt_kernel(*refs):
    _attn_prompt_body(*refs)


def _attn_fused_kernel(pt_ref, *refs):
    prompt_in, refs = refs[:N_PROMPT_IN], refs[N_PROMPT_IN:]
    qbd_ref, sbias_ref, knew_ref, vnew_ref, qmbd_ref, cmk_ref, cmv_ref, k_hbm, v_hbm = refs[:9]
    y_ref, o_ref, mo_ref, sc_ref, bias_ref, acc_ref, kbuf, vbuf, semk, semv = refs[9:]
    s = pl.program_id(0) * pl.num_programs(1) + pl.program_id(1)
    ns = pl.num_programs(0) * pl.num_programs(1)
    _sample_prefetch(pt_ref, s, ns, k_hbm, v_hbm, kbuf, vbuf, semk, semv)
    _attn_prompt_body(*prompt_in, y_ref, sc_ref, bias_ref, acc_ref)
    _sample_attend(pt_ref, s, qbd_ref, sbias_ref, knew_ref, vnew_ref, qmbd_ref, cmk_ref, cmv_ref, k_hbm, v_hbm,
                   o_ref, mo_ref, kbuf, vbuf, semk, semv)


def _attn_prompt(q, qi, wit, kb, vt, kk, ga, cvg, mg, x, wo, gp, TQ, sample=None):
    B, S, _ = x.shape
    NC = S // TQ
    imap = lambda f: (lambda b, i, *_: f(b, i))
    tok = lambda d: pl.BlockSpec((1, TQ, d), imap(lambda b, i: (b, i, 0)))
    seq = lambda d: pl.BlockSpec((1, S, d), imap(lambda b, i: (b, 0, 0)))
    full = lambda r, c: pl.BlockSpec((r, c), imap(lambda b, i: (0, 0)))
    in_specs = [tok(D_ATT), tok(D_QIDX), pl.BlockSpec((1, N_IDX_HEADS, TQ), imap(lambda b, i: (b, 0, i))),
                seq(D_KV), pl.BlockSpec((1, NC, D_KV, TQ), imap(lambda b, i: (b, 0, 0, 0))), seq(LANES),
                tok(D_ATT), tok(D_CONV), tok(D_MEM), tok(D_MODEL), full(D_MODEL, D_MODEL), full(1, D_MODEL)]
    assert len(in_specs) == N_PROMPT_IN
    scratch = [pltpu.VMEM((NC, TQ, TQ), f32), pltpu.VMEM((NC, TQ, TQ), f32), pltpu.VMEM((D_ATT, TQ), f32)]
    params = pltpu.CompilerParams(dimension_semantics=("arbitrary", "arbitrary"), vmem_limit_bytes=VMEM_LIMIT)
    y_shape = jax.ShapeDtypeStruct((B, S, D_MODEL), f32)
    if sample is None:
        return pl.pallas_call(
            _attn_prompt_kernel, grid=(B, NC), in_specs=in_specs, out_specs=tok(D_MODEL), out_shape=y_shape,
            scratch_shapes=scratch, compiler_params=params, name="attn_prompt",
        )(q, qi, wit, kb, vt, kk, ga, cvg, mg, x, wo, gp)
    pt, *sample_ops = sample
    s_in, s_out, s_shapes, s_scratch = _sattn_specs(pt, *sample_ops, index=lambda b, i: b * NC + i)
    return pl.pallas_call(
        _attn_fused_kernel,
        grid_spec=pltpu.PrefetchScalarGridSpec(
            num_scalar_prefetch=1, grid=(B, NC), in_specs=in_specs + s_in, out_specs=[tok(D_MODEL)] + s_out,
            scratch_shapes=scratch + s_scratch),
        out_shape=[y_shape] + s_shapes, compiler_params=params, name="attn_prompt",
    )(pt, q, qi, wit, kb, vt, kk, ga, cvg, mg, x, wo, gp, *sample_ops)


def _paged_copies(pt_ref, b, hbm_ref, buf, sem, slot, n_pages, page):
    return [pltpu.make_async_copy(hbm_ref.at[pt_ref[b, p]], buf.at[slot, :, pl.ds(p * page, page)], sem.at[slot])
            for p in range(n_pages)]


def _sidx_kernel(pt_ref, qi_ref, wi_ref, knew_ref, kidx_hbm, sc_ref, dn_ref, buf, sem):
    b = pl.program_id(0)
    nb = pl.num_programs(0)
    n_pages = pt_ref.shape[1]
    page = kidx_hbm.shape[2]
    copies = lambda bb, slot: _paged_copies(pt_ref, bb, kidx_hbm, buf, sem, slot, n_pages, page)
    slot = lax.rem(b, 2)

    @pl.when(b == 0)
    def _():
        for p, cp in enumerate(copies(0, 0)):
            cp.start(priority=p % 2)

    @pl.when(b + 1 < nb)
    def _():
        for p, cp in enumerate(copies(b + 1, 1 - slot)):
            cp.start(priority=p % 2)

    for cp in copies(b, slot):
        cp.wait()

    qi = qi_ref[0]
    wi = wi_ref[0]
    weigh = lambda d: jnp.sum(jnp.maximum(d, 0.0) * wi, axis=0, keepdims=True)
    sub = 1024
    for j in range(n_pages * page // sub):
        sc_ref[0, :, j * sub:(j + 1) * sub] = weigh(_dot(qi, buf[slot, :, j * sub:(j + 1) * sub].astype(bf16)))
    dn_ref[0] = weigh(_dot_nt(qi, knew_ref[...]))


def _sidx(pt, qi8, wi8, knew, kidx_t_pool):
    nb, n_pages = pt.shape
    page = kidx_t_pool.shape[2]
    past = n_pages * page
    return pl.pallas_call(
        _sidx_kernel,
        grid_spec=pltpu.PrefetchScalarGridSpec(
            num_scalar_prefetch=1,
            grid=(nb,),
            in_specs=[pl.BlockSpec((1, N_IDX_HEADS, IDX_DIM), lambda b, pt: (b, 0, 0)),
                      pl.BlockSpec((1, N_IDX_HEADS, 1), lambda b, pt: (b, 0, 0)),
                      pl.BlockSpec((nb, IDX_DIM), lambda b, pt: (0, 0)),
                      pl.BlockSpec(memory_space=pl.ANY)],
            out_specs=[pl.BlockSpec((1, 1, past), lambda b, pt: (b, 0, 0)),
                       pl.BlockSpec((1, 1, nb), lambda b, pt: (b, 0, 0))],
            scratch_shapes=[pltpu.VMEM((2, IDX_DIM, past), f32), pltpu.SemaphoreType.DMA((2,))]),
        out_shape=[jax.ShapeDtypeStruct((nb, 1, past), f32), jax.ShapeDtypeStruct((nb, 1, nb), f32)],
        compiler_params=pltpu.CompilerParams(dimension_semantics=("arbitrary",), vmem_limit_bytes=VMEM_LIMIT),
        name="sample_indexer",
    )(pt, qi8, wi8, knew, kidx_t_pool)


def _ssel_kernel(sct_ref, dn_ref, out_ref, sc_ref, bias_ref):
    nc, TK, Q = sc_ref.shape
    for c in range(nc - 1):
        sc_ref[c] = sct_ref[c * TK:(c + 1) * TK, :]
    row = lax.broadcasted_iota(jnp.int32, (TK, Q), 0)
    col = lax.broadcasted_iota(jnp.int32, (TK, Q), 1)
    sc_ref[nc - 1] = jnp.where(row == col, dn_ref[...], -jnp.inf)
    n_valid = (nc - 1) * TK + 1
    k_eff = jnp.full((1, Q), float(min(TOPK_MAX, n_valid // 4)), f32)
    _select_bias_t(sc_ref, bias_ref, nc, k_eff, jnp.full((1, Q), float(n_valid), f32))
    for c in range(nc):
        out_ref[c * TK:(c + 1) * TK, :] = bias_ref[c]


def _ssel(sc_t, dn_t):
    past, nb = sc_t.shape
    nc = past // nb + 1
    return pl.pallas_call(
        _ssel_kernel,
        out_shape=jax.ShapeDtypeStruct((past + nb, nb), f32),
        scratch_shapes=[pltpu.VMEM((nc, nb, nb), f32), pltpu.VMEM((nc, nb, nb), f32)],
        compiler_params=pltpu.CompilerParams(vmem_limit_bytes=VMEM_LIMIT),
        name="sample_select",
    )(sc_t, dn_t)


def _sample_copies(pt_ref, s, slot, k_hbm, v_hbm, kbuf, vbuf, semk, semv):
    n_pages, page = pt_ref.shape[1], k_hbm.shape[2]
    return (_paged_copies(pt_ref, s, k_hbm, kbuf, semk, slot, n_pages, page)
            + _paged_copies(pt_ref, s, v_hbm, vbuf, semv, slot, n_pages, page))


def _sample_prefetch(pt_ref, s, ns, k_hbm, v_hbm, kbuf, vbuf, semk, semv):
    slot = lax.rem(s, 2)

    @pl.when(s == 0)
    def _():
        for cp in _sample_copies(pt_ref, 0, 0, k_hbm, v_hbm, kbuf, vbuf, semk, semv):
            cp.start()

    @pl.when(s + 1 < ns)
    def _():
        for cp in _sample_copies(pt_ref, s + 1, 1 - slot, k_hbm, v_hbm, kbuf, vbuf, semk, semv):
            cp.start()


def _sample_attend(pt_ref, s, qbd_ref, bias_ref, knew_ref, vnew_ref, qmbd_ref, cmk_ref, cmv_ref, k_hbm, v_hbm,
                   o_ref, mo_ref, kbuf, vbuf, semk, semv):
    past = kbuf.shape[2]
    nb = knew_ref.shape[0]
    slot = lax.rem(s, 2)

    qm = qmbd_ref[0]
    pm, lm = _softmax_parts(_dot(qm, cmk_ref[0].astype(bf16)))
    mo = _dot_nt(pm.astype(bf16), cmv_ref[0].astype(bf16)) / lm
    for h in range(N_MEM_HEADS):
        hs = slice(h * HEAD_DIM, (h + 1) * HEAD_DIM)
        mo_ref[0, :, hs] = mo[h:h + 1, hs]

    for cp in _sample_copies(pt_ref, s, slot, k_hbm, v_hbm, kbuf, vbuf, semk, semv):
        cp.wait()

    qbd = qbd_ref[0]

    def softmax_part(logits, value_product):
        m_j = jnp.max(logits, axis=1, keepdims=True)
        p_j = jnp.exp2(logits - m_j)
        return m_j, jnp.sum(p_j, axis=1, keepdims=True), value_product(p_j.astype(bf16))

    seg = min(SAMPLE_SEG, past)
    parts = []
    for j in range(past // seg):
        ks = slice(j * seg, (j + 1) * seg)
        parts.append(softmax_part(_dot(qbd, kbuf[slot, :, ks].astype(bf16)) + bias_ref[0, :, ks],
                                  lambda p_j, ks=ks: _dot_nt(p_j, vbuf[slot, :, ks].astype(bf16))))
    parts.append(softmax_part(_dot_nt(qbd, knew_ref[...]) + bias_ref[0, :, past:past + nb],
                              lambda p_j: _dot(p_j, vnew_ref[...])))
    m = functools.reduce(jnp.maximum, [m_j for m_j, _, _ in parts])
    scales = [jnp.exp2(m_j - m) for m_j, _, _ in parts]
    l = sum(l_j * c for (_, l_j, _), c in zip(parts, scales))
    o = sum(o_j * c for (_, _, o_j), c in zip(parts, scales)) / l
    for p in range(N_ATT_HEADS):
        g0 = (p % 2) * HEAD_DIM
        o_ref[0, :, p * HEAD_DIM:(p + 1) * HEAD_DIM] = o[p:p + 1, g0:g0 + HEAD_DIM]


def _sattn_kernel(pt_ref, qbd_ref, bias_ref, knew_ref, vnew_ref, qmbd_ref, cmk_ref, cmv_ref, k_hbm, v_hbm,
                  o_ref, mo_ref, kbuf, vbuf, semk, semv):
    s = pl.program_id(0)
    _sample_prefetch(pt_ref, s, pl.num_programs(0), k_hbm, v_hbm, kbuf, vbuf, semk, semv)
    _sample_attend(pt_ref, s, qbd_ref, bias_ref, knew_ref, vnew_ref, qmbd_ref, cmk_ref, cmv_ref, k_hbm, v_hbm,
                   o_ref, mo_ref, kbuf, vbuf, semk, semv)


def _sattn_specs(pt, qbd, bias, knew, vnew, qmbd, cmk_t, cmv_t, k_t_pool, v_t_pool, index):
    nb, n_pages = pt.shape
    past = n_pages * k_t_pool.shape[2]
    M = cmk_t.shape[2]
    per = lambda r, c: pl.BlockSpec((1, r, c), lambda *g: (index(*g[:-1]), 0, 0))
    full = lambda r, c: pl.BlockSpec((r, c), lambda *g: (0, 0))
    in_specs = [per(N_ATT_HEADS, D_KV), per(1, past + nb), full(nb, D_KV), full(nb, D_KV),
                per(N_MEM_HEADS, D_MEM), per(D_MEM, M), per(D_MEM, M),
                pl.BlockSpec(memory_space=pl.ANY), pl.BlockSpec(memory_space=pl.ANY)]
    out_specs = [per(1, D_ATT), per(1, D_MEM)]
    out_shapes = [jax.ShapeDtypeStruct((nb, 1, D_ATT), f32), jax.ShapeDtypeStruct((nb, 1, D_MEM), f32)]
    scratch = [pltpu.VMEM((2, D_KV, past), f32), pltpu.VMEM((2, D_KV, past), f32),
               pltpu.SemaphoreType.DMA((2,)), pltpu.SemaphoreType.DMA((2,))]
    return in_specs, out_specs, out_shapes, scratch


def _sattn(pt, *ops):
    in_specs, out_specs, out_shapes, scratch = _sattn_specs(pt, *ops, index=lambda b: b)
    return pl.pallas_call(
        _sattn_kernel,
        grid_spec=pltpu.PrefetchScalarGridSpec(num_scalar_prefetch=1, grid=(pt.shape[0],), in_specs=in_specs,
                                               out_specs=out_specs, scratch_shapes=scratch),
        out_shape=out_shapes,
        compiler_params=pltpu.CompilerParams(dimension_semantics=("arbitrary",), vmem_limit_bytes=VMEM_LIMIT),
        name="sample_attention",
    )(pt, *ops)


def _smerge_kernel(att_ref, ga_ref, cvg_ref, mo_ref, gm_ref, x_ref, wo_ref, gp_ref, y_ref):
    att_g = (att_ref[...] * ga_ref[...]).astype(bf16)
    mg = (mo_ref[...] * gm_ref[...]).astype(bf16)
    y_ref[...] = _merge(att_g, cvg_ref[...], mg, x_ref[...], wo_ref, gp_ref[...])


def _smerge(att, ga, cvg, mo, gm, x, wo, gp):
    return pl.pallas_call(
        _smerge_kernel,
        out_shape=jax.ShapeDtypeStruct(x.shape, f32),
        compiler_params=pltpu.CompilerParams(vmem_limit_bytes=VMEM_LIMIT),
        name="sample_merge",
    )(att, ga, cvg, mo, gm, x, wo, gp)


def _rope_tables(pos):
    half = HEAD_DIM // 2
    inv = ROPE_THETA ** (-jnp.arange(half, dtype=f32) / half)
    ang = pos.astype(f32)[:, None] * inv[None, :]
    cos, sin = jnp.cos(ang), jnp.sin(ang)
    zero = jnp.zeros_like(sin)
    rep = LANES // HEAD_DIM
    return (jnp.tile(jnp.concatenate([cos, cos], axis=1), (1, rep)),
            jnp.tile(jnp.concatenate([-sin, zero], axis=1), (1, rep)),
            jnp.tile(jnp.concatenate([zero, sin], axis=1), (1, rep)))


def _pack_w_in(w):
    head = lambda base: jnp.concatenate([w[:, base + h * HEAD_DIM:base + (h + 1) * HEAD_DIM] for h in ATT_ORDER], 1)
    kidx = w[:, R_KIDX:R_WIDX]
    packed = jnp.concatenate([head(R_Q), w[:, R_K:R_GATT], head(R_GATT), w[:, R_QIDX:R_KIDX], kidx, kidx,
                              w[:, R_U:]], axis=1)
    w_t = jnp.concatenate([w[:, R_V:R_GATT], w[:, R_WIDX:R_U], jnp.zeros((D_MODEL, D_T - D_KV - N_IDX_HEADS), w.dtype)],
                          axis=1).T
    return packed.astype(bf16), w_t.astype(bf16)


def _pack_w_out(w):
    att = jnp.concatenate([w[h * HEAD_DIM:(h + 1) * HEAD_DIM] for h in ATT_ORDER], axis=0)
    return jnp.concatenate([att, w[D_ATT:]], axis=0).astype(bf16)


def _slab_rows(x, n_rows):
    T = x.shape[0]
    slabs = jnp.repeat(x.reshape(T, n_rows // 2, 1, 2, HEAD_DIM), 2, axis=2)
    keep = jnp.arange(2)[:, None] == jnp.arange(2)[None, :]
    return jnp.where(keep[None, None, :, :, None], slabs, jnp.zeros((), x.dtype)).reshape(T, n_rows, LANES)


def _diag_rows(x, n_heads):
    T = x.shape[0]
    keep = jnp.arange(n_heads)[:, None] == jnp.arange(n_heads)[None, :]
    xh = x.reshape(T, 1, n_heads, HEAD_DIM)
    return jnp.where(keep[None, :, :, None], xh, jnp.zeros((), x.dtype)).reshape(T, n_heads, n_heads * HEAD_DIM)


def _layer(l, xp, xs, mem_prompt, cache_k, cache_v, cache_kidx, cache_mem_k, cache_mem_v, state_conv,
           page_table, g_pre, w_in, conv_w, g_memnorm, w_mem_kv, w_out, g_post):
    B, S, _ = xp.shape
    nb, T, _ = xs.shape
    assert T == 1
    n_pages = page_table.shape[1]
    n_pool, page = cache_k.shape[1], cache_k.shape[2]
    past = n_pages * page
    M = mem_prompt.shape[1]
    assert page == nb and past % 1024 == 0 and past % min(SAMPLE_SEG, past) == 0

    row = lambda a: a[l][None, :]
    w_pack, w_t = _pack_w_in(w_in[l])
    wo = _pack_w_out(w_out[l])
    gp = row(g_post)

    mk, mv, mkb, mvb = _memkv(mem_prompt, row(g_memnorm), w_mem_kv[l].astype(bf16))
    cos, sa, sb = _rope_tables(jnp.arange(S, dtype=jnp.int32))
    TQ = min(256, S)
    TM = min(1024, S)
    (q, k, v, kb, vt, ga, qi, ki, kk, wit, cvg, cst, mg) = _front_prompt(
        xp, row(g_pre), w_pack, w_t, cos, sa, sb, conv_w[l], mkb, mvb, TM, TQ)

    cos, sa, sb = _rope_tables(jnp.full((nb,), past, jnp.int32))
    st = state_conv[l]
    (q_s, k_s, v_s, kb_s, vt_s, ga_s, qi_s, ki_s, kk_s, wit_s, cvg_s, u_s, qm_s, gm_s) = _front_sample(
        xs.reshape(1, nb, D_MODEL), row(g_pre), w_pack, w_t, cos, sa, sb, conv_w[l], st[:, 0, :], st[:, 1, :])
    kidx_t = jnp.swapaxes(cache_kidx[l], 1, 2)
    k_t = jnp.transpose(cache_k[l], (0, 2, 3, 1)).reshape(n_pool, D_KV, page)
    v_t = jnp.transpose(cache_v[l], (0, 2, 3, 1)).reshape(n_pool, D_KV, page)
    cmk_t = jnp.transpose(cache_mem_k[l], (0, 2, 3, 1)).reshape(nb, D_MEM, M)
    cmv_t = jnp.transpose(cache_mem_v[l], (0, 2, 3, 1)).reshape(nb, D_MEM, M)

    sc, dn = _sidx(page_table, qi_s.reshape(nb, N_IDX_HEADS, IDX_DIM), wit_s[0].T.reshape(nb, N_IDX_HEADS, 1),
                   kk_s[0, :, :IDX_DIM], kidx_t)
    bias_t = _ssel(sc.reshape(nb, past).T, dn.reshape(nb, nb).T)
    sample_ops = (page_table, _slab_rows(q_s[0], N_ATT_HEADS), bias_t.T.reshape(nb, 1, past + nb),
                  kb_s[0], vt_s[0, 0].T, _diag_rows(qm_s[0], N_MEM_HEADS), cmk_t, cmv_t, k_t, v_t)

    prompt_ops = (q, qi, wit, kb, vt, kk, ga, cvg, mg, xp, wo, gp, TQ)
    if nb == B * (S // TQ):
        yp, att_s, mo_s = _attn_prompt(*prompt_ops, sample=sample_ops)
    else:
        yp = _attn_prompt(*prompt_ops)
        att_s, mo_s = _sattn(*sample_ops)
    ys = _smerge(att_s.reshape(nb, D_ATT), ga_s[0], cvg_s[0], mo_s.reshape(nb, D_MEM), gm_s[0],
                 xs.reshape(nb, D_MODEL), wo, gp)

    outs_p = (k.reshape(B, S, N_KV_HEADS, HEAD_DIM), v.reshape(B, S, N_KV_HEADS, HEAD_DIM), ki, cst,
              mk.reshape(B, M, N_MEM_HEADS, HEAD_DIM), mv.reshape(B, M, N_MEM_HEADS, HEAD_DIM))
    outs_s = (k_s.reshape(nb, 1, N_KV_HEADS, HEAD_DIM), v_s.reshape(nb, 1, N_KV_HEADS, HEAD_DIM),
              ki_s.reshape(nb, 1, IDX_DIM), jnp.stack([st[:, 1, :], u_s[0]], axis=1))
    return yp, ys.reshape(nb, 1, D_MODEL), outs_p, outs_s


def kernel(x_prompt, x_sample, mem_prompt, cache_k, cache_v, cache_kidx, cache_mem_k, cache_mem_v, state_conv,
           page_table, g_pre, w_in, conv_w, g_memnorm, w_mem_kv, w_out, g_post):
    depth = w_in.shape[0]
    xp, xs = x_prompt, x_sample
    per_p, per_s = [], []
    for l in range(depth):
        xp, xs, op, os_ = _layer(l, xp, xs, mem_prompt, cache_k, cache_v, cache_kidx, cache_mem_k, cache_mem_v,
                                 state_conv, page_table, g_pre, w_in, conv_w, g_memnorm, w_mem_kv, w_out, g_post)
        per_p.append(op)
        per_s.append(os_)
    stack = lambda outs, i: jnp.stack([o[i] for o in outs])
    return ((xp, xs) + tuple(stack(per_p, i) for i in range(6)) + tuple(stack(per_s, i) for i in range(4)))
```

```python
import functools

import jax
import jax.numpy as jnp
from jax import lax
from jax.experimental import pallas as pl
from jax.experimental.pallas import tpu as pltpu

D_MODEL = 1024
HEAD_DIM = 64
N_MEM_HEADS = 4
D_MEM = N_MEM_HEADS * HEAD_DIM
D_ATT = 512
N_ATT_HEADS = 8
N_KV_HEADS = 2
D_KV = N_KV_HEADS * HEAD_DIM
D_CONV = 256
N_IDX_HEADS = 8
IDX_DIM = 64
D_QIDX = N_IDX_HEADS * IDX_DIM
TOPK_MAX = 256
CONV_WIDTH = 3
ROPE_THETA = 10000.0
RMS_EPS = 1e-6
LANES = 128

R_Q = 0
R_K = R_Q + D_ATT
R_V = R_K + D_KV
R_GATT = R_V + D_KV
R_QIDX = R_GATT + D_ATT
R_KIDX = R_QIDX + D_QIDX
R_WIDX = R_KIDX + IDX_DIM
R_U = R_WIDX + N_IDX_HEADS

ATT_ORDER = (0, 4, 1, 5, 2, 6, 3, 7)
P_Q = 0
P_K = P_Q + D_ATT
P_V = P_K + D_KV
P_GATT = P_V + D_KV
P_QIDX = P_GATT + D_ATT
P_KK = P_QIDX + D_QIDX
P_U = P_KK + LANES
P_B = P_U + D_CONV
P_C = P_B + D_CONV
P_GCONV = P_C + D_CONV
P_QMEM = P_GCONV + D_CONV
P_GMEM = P_QMEM + D_MEM
P_END = P_GMEM + D_MEM

QK_SCALE = HEAD_DIM ** -0.5
LOG2E = 1.4426950408889634
IDX_SCALE = IDX_DIM ** -0.5
W_SCALE = N_IDX_HEADS ** -0.5

F32_MAX = 3.4028234663852886e38
SAMPLE_SEG = 4096
FIXED_PASSES = 16
KEY_PASS_EVERY = 8
MAX_PASSES = 8 * 34
NEG = -1e30
VMEM_LIMIT = 56 * 1024 * 1024
ACC_ROWS = 32
D_T = D_KV + 16

bf16 = jnp.bfloat16
f32 = jnp.float32
NT = (((1,), (1,)), ((), ()))


def _dot(a, b):
    return jnp.dot(a, b, preferred_element_type=f32)


def _dot_nt(a, b):
    return lax.dot_general(a, b, NT, preferred_element_type=f32)


def _rms(x, g):
    return x * lax.rsqrt(jnp.mean(x * x, axis=-1, keepdims=True) + RMS_EPS) * g


def _silu(x):
    return x / (1.0 + jnp.exp(-x))


def _rope(z, cos, sa, sb):
    return z * cos + pltpu.roll(z, 96, 1) * sa + pltpu.roll(z, 32, 1) * sb


def _key_to_f32(k):
    return lax.bitcast_convert_type(k ^ ((k >> 31) & jnp.int32(0x7FFFFFFF)), f32)


def _softmax_parts(s):
    m = jnp.max(s, axis=-1, keepdims=True)
    p = jnp.exp(s - m)
    return p, jnp.sum(p, axis=-1, keepdims=True)


def _f32_to_key(x):
    bits = lax.bitcast_convert_type(x, jnp.int32)
    return bits ^ ((bits >> 31) & jnp.int32(0x7FFFFFFF))


def _select_bias_t(sc_ref, bias_ref, nc, k_eff, n_valid):
    _, TK, Q = sc_ref.shape
    groups = -(-TOPK_MAX // TK)

    def count_ge(thr):
        thr_b = jnp.broadcast_to(thr, (ACC_ROWS, Q))

        def body(c, acc):
            for j in range(TK // ACC_ROWS):
                acc = acc + jnp.where(sc_ref[c, j * ACC_ROWS:(j + 1) * ACC_ROWS, :] >= thr_b, 1.0, 0.0)
            return acc

        acc = lax.fori_loop(0, nc, body, jnp.zeros((ACC_ROWS, Q), f32))
        return jnp.sum(acc, axis=0, keepdims=True)

    if groups == 1:
        bias_ref[0] = sc_ref[0]

        def gmax_body(c, carry):
            bias_ref[0] = jnp.maximum(bias_ref[0], sc_ref[c])
            return carry

        lax.fori_loop(1, nc, gmax_body, 0)
    else:
        for c in range(nc):
            bias_ref[c % groups] = sc_ref[c] if c < groups else jnp.maximum(bias_ref[c % groups], sc_ref[c])
    gmaxs = [bias_ref[g] for g in range(groups)]
    low = jnp.min(functools.reduce(jnp.minimum, gmaxs), axis=0, keepdims=True)
    top = jnp.max(functools.reduce(jnp.maximum, gmaxs), axis=0, keepdims=True)

    low_b = jnp.broadcast_to(low, (ACC_ROWS, Q))

    def start_body(c, accs):
        a_low, a_ge0, a_gt0 = accs
        for j in range(TK // ACC_ROWS):
            x = sc_ref[c, j * ACC_ROWS:(j + 1) * ACC_ROWS, :]
            a_low = a_low + jnp.where(x >= low_b, 1.0, 0.0)
            a_ge0 = a_ge0 + jnp.where(x >= 0.0, 1.0, 0.0)
            a_gt0 = a_gt0 + jnp.where(x > 0.0, 1.0, 0.0)
        return a_low, a_ge0, a_gt0

    zeros = jnp.zeros((ACC_ROWS, Q), f32)
    c_low, c_ge0, c_gt0 = [jnp.sum(a, axis=0, keepdims=True)
                           for a in lax.fori_loop(0, nc, start_body, (zeros, zeros, zeros))]
    all_in = n_valid == k_eff
    at_zero = (c_gt0 < k_eff) & (c_ge0 >= k_eff)
    above = c_gt0 >= k_eff
    low_ok = low > -F32_MAX
    lo = jnp.where(all_in, -F32_MAX, jnp.where(
        at_zero, 0.0, jnp.where(above, jnp.maximum(low, 0.0), jnp.where(low_ok, low, -F32_MAX))))
    c_lo = jnp.where(all_in, n_valid, jnp.where(
        at_zero, c_ge0, jnp.where(above, jnp.where(low > 0.0, c_low, c_ge0), jnp.where(low_ok, c_low, n_valid))))
    hi = jnp.where(above, _key_to_f32(_f32_to_key(top) + 1), 0.0)
    c_hi = jnp.where(at_zero, c_gt0, jnp.where(above, 0.0, c_ge0))

    settled = all_in | at_zero
    halfway = lambda lo, hi: lo + (hi - lo) * 0.5

    def still_open(lo, hi, c_lo):
        mid = halfway(lo, hi)
        return jnp.where(jnp.logical_not(settled) & (c_lo != k_eff) & (mid > lo) & (mid < hi), 1.0, 0.0)

    def one_pass(it, state):
        lo, hi, c_lo, c_hi, w_lo, w_hi, moved = state
        active = still_open(lo, hi, c_lo) > 0.5
        k_lo, k_hi = _f32_to_key(lo), _f32_to_key(hi)
        a = (c_lo - k_eff + 0.5) * w_lo
        b = (k_eff - 0.5 - c_hi) * w_hi
        by_count = lo + (hi - lo) * (a / (a + b))
        by_key = _key_to_f32(k_lo + lax.shift_right_logical(k_hi - k_lo, 1))
        pivot = jnp.where(it % KEY_PASS_EVERY == KEY_PASS_EVERY - 1, by_key, by_count)
        pivot = jnp.where((pivot > lo) & (pivot < hi), pivot, halfway(lo, hi))
        pivot = jnp.where(active, pivot, lo)
        cnt = count_ge(pivot)
        up = active & (cnt >= k_eff)
        down = active & (cnt < k_eff)
        w_hi = jnp.where(up, jnp.where(moved > 0.5, w_hi * 0.5, 1.0), jnp.where(down, 1.0, w_hi))
        w_lo = jnp.where(down, jnp.where(moved < -0.5, w_lo * 0.5, 1.0), jnp.where(up, 1.0, w_lo))
        moved = jnp.where(up, 1.0, jnp.where(down, -1.0, moved))
        lo, c_lo = jnp.where(up, pivot, lo), jnp.where(up, cnt, c_lo)
        hi, c_hi = jnp.where(down, pivot, hi), jnp.where(down, cnt, c_hi)
        return lo, hi, c_lo, c_hi, w_lo, w_hi, moved

    ones = jnp.ones((1, Q), f32)
    state = lax.fori_loop(0, FIXED_PASSES, one_pass, (lo, hi, c_lo, c_hi, ones, ones, jnp.zeros((1, Q), f32)))
    any_open = lambda st: jnp.max(still_open(st[0], st[1], st[2]))

    def more_passes(carry):
        it, state, _ = carry
        state = one_pass(it + 1, one_pass(it, state))
        return it + 2, state, any_open(state)

    _, state, _ = lax.while_loop(lambda carry: (carry[2] > 0.5) & (carry[0] < MAX_PASSES), more_passes,
                                 (jnp.int32(FIXED_PASSES), state, any_open(state)))
    thr, cnt_u, cnt_above = state[0], state[2], state[3]
    has_tie = jnp.max(jnp.where(cnt_u != k_eff, 1.0, 0.0)) > 0.5

    @pl.when(jnp.logical_not(has_tie))
    def _():
        def bias_body(c, carry):
            bias_ref[c] = jnp.where(sc_ref[c] >= thr, 0.0, NEG)
            return carry

        lax.fori_loop(0, nc, bias_body, 0)

    @pl.when(has_tie)
    def _():
        need = jnp.where(cnt_u == k_eff, F32_MAX, k_eff - cnt_above)
        r_i = lax.broadcasted_iota(jnp.int32, (TK, TK), 0)
        c_i = lax.broadcasted_iota(jnp.int32, (TK, TK), 1)
        tri = jnp.where(c_i <= r_i, 1.0, 0.0).astype(bf16)

        def tie_chunk(c, run):
            x = sc_ref[c]
            eq = x == thr
            incl = _dot(tri, jnp.where(eq, 1.0, 0.0).astype(bf16))
            tied = jnp.where(incl < need + 0.5 - run, jnp.where(eq, 0.0, NEG), NEG)
            bias_ref[c] = jnp.where(x > thr, 0.0, tied)
            return run + incl[TK - 1:TK, :]

        def bias_body(j, run):
            return tie_chunk(2 * j + 1, tie_chunk(2 * j, run))

        run = lax.fori_loop(0, nc // 2, bias_body, jnp.zeros((1, Q), f32))
        lax.cond(nc % 2 == 1, lambda r: tie_chunk(nc - 1, r), lambda r: r, run)


def _merge(att_g, cvg, mg, x, wo_ref, gp):
    o = (_dot(att_g, wo_ref[0:D_ATT, :])
         + _dot(cvg, wo_ref[D_ATT:D_ATT + D_CONV, :])
         + _dot(mg, wo_ref[D_ATT + D_CONV:D_MODEL, :]))
    return x + _rms(o, gp)


def _memkv_kernel(mem_ref, g_ref, w_ref, mk_ref, mv_ref, mkb_ref, mvb_ref):
    kv = _dot(_rms(mem_ref[0], g_ref[...]).astype(bf16), w_ref[...])
    mk = kv[:, :D_MEM]
    mv = kv[:, D_MEM:]
    mk_ref[0] = mk
    mv_ref[0] = mv
    mkb_ref[0] = mk.astype(bf16)
    mvb_ref[0] = mv.astype(bf16)


def _memkv(mem, g, w):
    B, M, _ = mem.shape
    blk = lambda: pl.BlockSpec((1, M, D_MEM), lambda b: (b, 0, 0))
    return pl.pallas_call(
        _memkv_kernel,
        grid=(B,),
        in_specs=[pl.BlockSpec((1, M, D_MODEL), lambda b: (b, 0, 0)),
                  pl.BlockSpec((1, D_MODEL), lambda b: (0, 0)),
                  pl.BlockSpec((D_MODEL, 2 * D_MEM), lambda b: (0, 0))],
        out_specs=[blk(), blk(), blk(), blk()],
        out_shape=[jax.ShapeDtypeStruct((B, M, D_MEM), f32)] * 2
        + [jax.ShapeDtypeStruct((B, M, D_MEM), bf16)] * 2,
        compiler_params=pltpu.CompilerParams(dimension_semantics=("arbitrary",)),
        name="memkv",
    )(mem, g, w)


def _front_common(x_ref, g_ref, w_ref, wt_ref, cos_ref, sa_ref, sb_ref,
                  q_ref, k_ref, v_ref, kb_ref, vt_ref, ga_ref, qi_ref, ki_ref, kk_ref, wit_ref):
    xb = _rms(x_ref[0], g_ref[...]).astype(bf16)
    cos, sa, sb = cos_ref[...], sa_ref[...], sb_ref[...]
    proj = lambda lo, hi: _dot(xb, w_ref[:, lo:hi])
    rope = lambda z: _rope(z, cos, sa, sb)

    zq = proj(P_Q, P_K)
    zqi = proj(P_QIDX, P_KK)
    for s in range(D_ATT // LANES):
        sl = slice(s * LANES, (s + 1) * LANES)
        q_ref[0, :, sl] = (rope(zq[:, sl]) * (QK_SCALE * LOG2E)).astype(bf16)
        qi_ref[0, :, sl] = (rope(zqi[:, sl]) * IDX_SCALE).astype(bf16)
    zkv = proj(P_K, P_GATT)
    zk = rope(zkv[:, :D_KV])
    k_ref[0] = zk
    kb_ref[0] = zk.astype(bf16)
    v_ref[0] = zkv[:, D_KV:]
    ga_ref[0] = _silu(proj(P_GATT, P_QIDX))
    zkk = rope(proj(P_KK, P_U))
    ki_ref[0] = zkk[:, :IDX_DIM]
    kk_ref[0] = zkk.astype(bf16)
    vw_t = _dot_nt(wt_ref[...], xb)
    TK = vt_ref.shape[3]
    for c in range(vt_ref.shape[1]):
        vt_ref[0, c] = vw_t[0:D_KV, c * TK:(c + 1) * TK].astype(bf16)
    wit_ref[0] = vw_t[D_KV:D_KV + N_IDX_HEADS, :] * W_SCALE
    return proj


def _front_prompt_kernel(x_ref, g_ref, w_ref, wt_ref, cos_ref, sa_ref, sb_ref, cw_ref, mkb_ref, mvb_ref,
                         q_ref, k_ref, v_ref, kb_ref, vt_ref, ga_ref, qi_ref, ki_ref, kk_ref, wit_ref,
                         cvg_ref, cst_ref, mg_ref, ubuf):
    TM = x_ref.shape[1]
    proj = _front_common(x_ref, g_ref, w_ref, wt_ref, cos_ref, sa_ref, sb_ref,
                         q_ref, k_ref, v_ref, kb_ref, vt_ref, ga_ref, qi_ref, ki_ref, kk_ref, wit_ref)

    @pl.when(pl.program_id(1) == 0)
    def _():
        ubuf[0:8, :] = jnp.zeros((8, D_CONV), f32)

    u = proj(P_C, P_GCONV) * proj(P_U, P_B)
    ubuf[8:TM + 8, :] = u
    cw = cw_ref[...]
    conv = proj(P_B, P_C) * (cw[0:1] * ubuf[6:TM + 6, :] + cw[1:2] * ubuf[7:TM + 7, :] + cw[2:3] * u)
    cvg_ref[0] = (conv * _silu(proj(P_GCONV, P_QMEM))).astype(bf16)
    tail = u[TM - 2:TM, :]
    ubuf[6:8, :] = tail
    cst_ref[0] = tail

    qm = (proj(P_QMEM, P_GMEM) * QK_SCALE).astype(bf16)
    mkb = mkb_ref[0]
    mvb = mvb_ref[0]
    outs = []
    for h in range(N_MEM_HEADS):
        sl = slice(h * HEAD_DIM, (h + 1) * HEAD_DIM)
        p, l = _softmax_parts(_dot_nt(qm[:, sl], mkb[:, sl]))
        outs.append(_dot(p.astype(bf16), mvb[:, sl]) / l)
    mo = jnp.concatenate(outs, axis=1)
    mg_ref[0] = (mo * _silu(proj(P_GMEM, P_END))).astype(bf16)


def _front_sample_kernel(x_ref, g_ref, w_ref, wt_ref, cos_ref, sa_ref, sb_ref, cw_ref, st0_ref, st1_ref,
                         q_ref, k_ref, v_ref, kb_ref, vt_ref, ga_ref, qi_ref, ki_ref, kk_ref, wit_ref,
                         cvg_ref, u_ref, qm_ref, gm_ref):
    proj = _front_common(x_ref, g_ref, w_ref, wt_ref, cos_ref, sa_ref, sb_ref,
                         q_ref, k_ref, v_ref, kb_ref, vt_ref, ga_ref, qi_ref, ki_ref, kk_ref, wit_ref)
    u = proj(P_C, P_GCONV) * proj(P_U, P_B)
    cw = cw_ref[...]
    conv = proj(P_B, P_C) * (cw[0:1] * st0_ref[...] + cw[1:2] * st1_ref[...] + cw[2:3] * u)
    cvg_ref[0] = (conv * _silu(proj(P_GCONV, P_QMEM))).astype(bf16)
    u_ref[0] = u
    qm_ref[0] = (proj(P_QMEM, P_GMEM) * QK_SCALE).astype(bf16)
    gm_ref[0] = _silu(proj(P_GMEM, P_END))


def _front_out(B, S, TM, TK):
    tok = lambda d: pl.BlockSpec((1, TM, d), lambda b, j: (b, j, 0))
    specs = [tok(D_ATT), tok(D_KV), tok(D_KV), tok(D_KV),
             pl.BlockSpec((1, TM // TK, D_KV, TK), lambda b, j: (b, j, 0, 0)),
             tok(D_ATT), tok(D_QIDX), tok(IDX_DIM), tok(LANES),
             pl.BlockSpec((1, N_IDX_HEADS, TM), lambda b, j: (b, 0, j))]
    sds = lambda d, t: jax.ShapeDtypeStruct((B, S, d), t)
    shapes = [sds(D_ATT, bf16), sds(D_KV, f32), sds(D_KV, f32), sds(D_KV, bf16),
              jax.ShapeDtypeStruct((B, S // TK, D_KV, TK), bf16),
              sds(D_ATT, f32), sds(D_QIDX, bf16), sds(IDX_DIM, f32), sds(LANES, bf16),
              jax.ShapeDtypeStruct((B, N_IDX_HEADS, S), f32)]
    return tok, sds, specs, shapes


def _front_in(TM):
    full = lambda r, c: pl.BlockSpec((r, c), lambda b, j: (0, 0))
    rows = lambda: pl.BlockSpec((TM, LANES), lambda b, j: (j, 0))
    return [pl.BlockSpec((1, TM, D_MODEL), lambda b, j: (b, j, 0)), full(1, D_MODEL), full(D_MODEL, P_END),
            full(D_T, D_MODEL), rows(), rows(), rows(), full(CONV_WIDTH, D_CONV)]


def _front_prompt(x, g, w, wt, cos, sa, sb, cw, mkb, mvb, TM, TK):
    B, S, _ = x.shape
    M = mkb.shape[1]
    tok, sds, specs, shapes = _front_out(B, S, TM, TK)
    mem = lambda: pl.BlockSpec((1, M, D_MEM), lambda b, j: (b, 0, 0))
    return pl.pallas_call(
        _front_prompt_kernel,
        grid=(B, S // TM),
        in_specs=_front_in(TM) + [mem(), mem()],
        out_specs=specs + [tok(D_CONV), pl.BlockSpec((1, CONV_WIDTH - 1, D_CONV), lambda b, j: (b, 0, 0)),
                           tok(D_MEM)],
        out_shape=shapes + [sds(D_CONV, bf16), jax.ShapeDtypeStruct((B, CONV_WIDTH - 1, D_CONV), f32),
                            sds(D_MEM, bf16)],
        scratch_shapes=[pltpu.VMEM((TM + 8, D_CONV), f32)],
        compiler_params=pltpu.CompilerParams(dimension_semantics=("arbitrary", "arbitrary"),
                                             vmem_limit_bytes=VMEM_LIMIT),
        name="front_prompt",
    )(x, g, w, wt, cos, sa, sb, cw, mkb, mvb)


def _front_sample(x, g, w, wt, cos, sa, sb, cw, st0, st1):
    _, T, _ = x.shape
    tok, sds, specs, shapes = _front_out(1, T, T, T)
    st = lambda: pl.BlockSpec((T, D_CONV), lambda b, j: (0, 0))
    return pl.pallas_call(
        _front_sample_kernel,
        grid=(1, 1),
        in_specs=_front_in(T) + [st(), st()],
        out_specs=specs + [tok(D_CONV), tok(D_CONV), tok(D_MEM), tok(D_MEM)],
        out_shape=shapes + [sds(D_CONV, bf16), sds(D_CONV, f32), sds(D_MEM, bf16), sds(D_MEM, f32)],
        compiler_params=pltpu.CompilerParams(dimension_semantics=("arbitrary", "arbitrary"),
                                             vmem_limit_bytes=VMEM_LIMIT),
        name="front_sample",
    )(x, g, w, wt, cos, sa, sb, cw, st0, st1)


def _half_masked(slab, half):
    lane = lax.broadcasted_iota(jnp.int32, slab.shape, 1)
    keep = (lane < HEAD_DIM) if half == 0 else (lane >= HEAD_DIM)
    return jnp.where(keep, slab, jnp.zeros((), slab.dtype))


def _attn_prompt_body(q_ref, qi_ref, wit_ref, kb_ref, vt_ref, kk_ref, ga_ref, cvg_ref, mg_ref, x_ref,
                      wo_ref, gp_ref, y_ref, sc_ref, bias_ref, acc_ref):
    TQ = q_ref.shape[1]
    TK = TQ
    S = kb_ref.shape[1]
    i = pl.program_id(1)
    chunk = lambda c: pl.ds(pl.multiple_of(c * TK, TK), TK)
    slab = lambda ref, p: ref[0, :, (p // 2) * LANES:(p // 2 + 1) * LANES]

    blk = lambda a, p: a[:, p * TQ:(p + 1) * TQ]
    qi_all = jnp.concatenate([_half_masked(slab(qi_ref, h), h % 2) for h in range(N_IDX_HEADS)], axis=0)
    wit = wit_ref[0]

    def scores_t(c):
        d = _dot_nt(kk_ref[0, chunk(c), :], qi_all)
        acc = jnp.maximum(blk(d, 0), 0.0) * wit[0:1, :]
        for h in range(1, N_IDX_HEADS):
            acc = acc + jnp.maximum(blk(d, h), 0.0) * wit[h:h + 1, :]
        return acc

    def score_body(j, carry):
        for r in range(2):
            sc_ref[2 * j + r] = scores_t(2 * j + r)
        return carry

    lax.fori_loop(0, lax.shift_right_logical(i, 1), score_body, 0)

    @pl.when(i % 2 == 1)
    def _():
        sc_ref[i - 1] = scores_t(i - 1)

    key_pos = lax.broadcasted_iota(jnp.int32, (TK, TQ), 0)
    qry_pos = lax.broadcasted_iota(jnp.int32, (TK, TQ), 1)
    sc_ref[i] = jnp.where(key_pos <= qry_pos, scores_t(i), -jnp.inf)

    t1 = i * TQ + 1 + lax.broadcasted_iota(jnp.int32, (1, TQ), 1)
    k_eff = jnp.minimum(t1, min(TOPK_MAX, S // 4)).astype(f32)
    _select_bias_t(sc_ref, bias_ref, i + 1, k_eff, t1.astype(f32))

    q_all = jnp.concatenate([_half_masked(slab(q_ref, p), p % 2) for p in range(N_ATT_HEADS)], axis=0)
    acc_ref[...] = jnp.zeros(acc_ref.shape, f32)

    def att_chunk(c_kv, c_bias, ms, ls):
        bias = bias_ref[c_bias]
        s_all = _dot_nt(kb_ref[0, chunk(c_kv), :], q_all)
        ms_new, ls_new, alphas, pts = [], [], [], []
        for p in range(N_ATT_HEADS):
            s = blk(s_all, p) + bias
            m_new = jnp.maximum(ms[p], jnp.max(s, axis=0, keepdims=True))
            alpha = jnp.exp2(ms[p] - m_new)
            pt = jnp.exp2(s - m_new)
            ls_new.append(alpha * ls[p] + jnp.sum(pt, axis=0, keepdims=True))
            ms_new.append(m_new)
            alphas.append(alpha)
            pts.append(pt.astype(bf16))
        pv_all = _dot(vt_ref[0, c_kv], jnp.concatenate(pts, axis=1))
        for p in range(N_ATT_HEADS):
            rows = slice(p * HEAD_DIM, (p + 1) * HEAD_DIM)
            g0 = (p % 2) * HEAD_DIM
            acc_ref[rows, :] = alphas[p] * acc_ref[rows, :] + blk(pv_all, p)[g0:g0 + HEAD_DIM, :]
        return tuple(ms_new), tuple(ls_new)

    def att_body(j, carry):
        return att_chunk(2 * j + 1, 2 * j + 1, *att_chunk(2 * j, 2 * j, *carry))

    init = (tuple(jnp.full((1, TQ), NEG, f32) for _ in range(N_ATT_HEADS)),
            tuple(jnp.zeros((1, TQ), f32) for _ in range(N_ATT_HEADS)))
    carry = lax.fori_loop(0, lax.shift_right_logical(i + 1, 1), att_body, init)
    _, ls = lax.cond((i + 1) % 2 == 1, lambda ms, ls: att_chunk(i, i, ms, ls), lambda ms, ls: (ms, ls), *carry)
    for p in range(N_ATT_HEADS):
        rows = slice(p * HEAD_DIM, (p + 1) * HEAD_DIM)
        acc_ref[rows, :] = acc_ref[rows, :] / ls[p]
    att_g = (acc_ref[...].T * ga_ref[0]).astype(bf16)
    y_ref[0] = _merge(att_g, cvg_ref[0], mg_ref[0], x_ref[0], wo_ref, gp_ref[...])


N_PROMPT_IN = 12


def _attn_prompt_kernel(*refs):
    _attn_prompt_body(*refs)


def _attn_fused_kernel(pt_ref, *refs):
    prompt_in, refs = refs[:N_PROMPT_IN], refs[N_PROMPT_IN:]
    qbd_ref, sbias_ref, knew_ref, vnew_ref, qmbd_ref, cmk_ref, cmv_ref, k_hbm, v_hbm = refs[:9]
    y_ref, o_ref, mo_ref, sc_ref, bias_ref, acc_ref, kbuf, vbuf, semk, semv = refs[9:]
    s = pl.program_id(0) * pl.num_programs(1) + pl.program_id(1)
    ns = pl.num_programs(0) * pl.num_programs(1)
    _sample_prefetch(pt_ref, s, ns, k_hbm, v_hbm, kbuf, vbuf, semk, semv)
    _attn_prompt_body(*prompt_in, y_ref, sc_ref, bias_ref, acc_ref)
    _sample_attend(pt_ref, s, qbd_ref, sbias_ref, knew_ref, vnew_ref, qmbd_ref, cmk_ref, cmv_ref, k_hbm, v_hbm,
                   o_ref, mo_ref, kbuf, vbuf, semk, semv)


def _attn_prompt(q, qi, wit, kb, vt, kk, ga, cvg, mg, x, wo, gp, TQ, sample=None):
    B, S, _ = x.shape
    NC = S // TQ
    imap = lambda f: (lambda b, i, *_: f(b, i))
    tok = lambda d: pl.BlockSpec((1, TQ, d), imap(lambda b, i: (b, i, 0)))
    seq = lambda d: pl.BlockSpec((1, S, d), imap(lambda b, i: (b, 0, 0)))
    full = lambda r, c: pl.BlockSpec((r, c), imap(lambda b, i: (0, 0)))
    in_specs = [tok(D_ATT), tok(D_QIDX), pl.BlockSpec((1, N_IDX_HEADS, TQ), imap(lambda b, i: (b, 0, i))),
                seq(D_KV), pl.BlockSpec((1, NC, D_KV, TQ), imap(lambda b, i: (b, 0, 0, 0))), seq(LANES),
                tok(D_ATT), tok(D_CONV), tok(D_MEM), tok(D_MODEL), full(D_MODEL, D_MODEL), full(1, D_MODEL)]
    assert len(in_specs) == N_PROMPT_IN
    scratch = [pltpu.VMEM((NC, TQ, TQ), f32), pltpu.VMEM((NC, TQ, TQ), f32), pltpu.VMEM((D_ATT, TQ), f32)]
    params = pltpu.CompilerParams(dimension_semantics=("arbitrary", "arbitrary"), vmem_limit_bytes=VMEM_LIMIT)
    y_shape = jax.ShapeDtypeStruct((B, S, D_MODEL), f32)
    if sample is None:
        return pl.pallas_call(
            _attn_prompt_kernel, grid=(B, NC), in_specs=in_specs, out_specs=tok(D_MODEL), out_shape=y_shape,
            scratch_shapes=scratch, compiler_params=params, name="attn_prompt",
        )(q, qi, wit, kb, vt, kk, ga, cvg, mg, x, wo, gp)
    pt, *sample_ops = sample
    s_in, s_out, s_shapes, s_scratch = _sattn_specs(pt, *sample_ops, index=lambda b, i: b * NC + i)
    return pl.pallas_call(
        _attn_fused_kernel,
        grid_spec=pltpu.PrefetchScalarGridSpec(
            num_scalar_prefetch=1, grid=(B, NC), in_specs=in_specs + s_in, out_specs=[tok(D_MODEL)] + s_out,
            scratch_shapes=scratch + s_scratch),
        out_shape=[y_shape] + s_shapes, compiler_params=params, name="attn_prompt",
    )(pt, q, qi, wit, kb, vt, kk, ga, cvg, mg, x, wo, gp, *sample_ops)


def _paged_copies(pt_ref, b, hbm_ref, buf, sem, slot, n_pages, page):
    return [pltpu.make_async_copy(hbm_ref.at[pt_ref[b, p]], buf.at[slot, :, pl.ds(p * page, page)], sem.at[slot])
            for p in range(n_pages)]


def _sidx_kernel(pt_ref, qi_ref, wi_ref, knew_ref, kidx_hbm, sc_ref, dn_ref, buf, sem):
    b = pl.program_id(0)
    nb = pl.num_programs(0)
    n_pages = pt_ref.shape[1]
    page = kidx_hbm.shape[2]
    copies = lambda bb, slot: _paged_copies(pt_ref, bb, kidx_hbm, buf, sem, slot, n_pages, page)
    slot = lax.rem(b, 2)

    @pl.when(b == 0)
    def _():
        for p, cp in enumerate(copies(0, 0)):
            cp.start(priority=p % 2)

    @pl.when(b + 1 < nb)
    def _():
        for p, cp in enumerate(copies(b + 1, 1 - slot)):
            cp.start(priority=p % 2)

    for cp in copies(b, slot):
        cp.wait()

    qi = qi_ref[0]
    wi = wi_ref[0]
    weigh = lambda d: jnp.sum(jnp.maximum(d, 0.0) * wi, axis=0, keepdims=True)
    sub = 1024
    for j in range(n_pages * page // sub):
        sc_ref[0, :, j * sub:(j + 1) * sub] = weigh(_dot(qi, buf[slot, :, j * sub:(j + 1) * sub].astype(bf16)))
    dn_ref[0] = weigh(_dot_nt(qi, knew_ref[...]))


def _sidx(pt, qi8, wi8, knew, kidx_t_pool):
    nb, n_pages = pt.shape
    page = kidx_t_pool.shape[2]
    past = n_pages * page
    return pl.pallas_call(
        _sidx_kernel,
        grid_spec=pltpu.PrefetchScalarGridSpec(
            num_scalar_prefetch=1,
            grid=(nb,),
            in_specs=[pl.BlockSpec((1, N_IDX_HEADS, IDX_DIM), lambda b, pt: (b, 0, 0)),
                      pl.BlockSpec((1, N_IDX_HEADS, 1), lambda b, pt: (b, 0, 0)),
                      pl.BlockSpec((nb, IDX_DIM), lambda b, pt: (0, 0)),
                      pl.BlockSpec(memory_space=pl.ANY)],
            out_specs=[pl.BlockSpec((1, 1, past), lambda b, pt: (b, 0, 0)),
                       pl.BlockSpec((1, 1, nb), lambda b, pt: (b, 0, 0))],
            scratch_shapes=[pltpu.VMEM((2, IDX_DIM, past), f32), pltpu.SemaphoreType.DMA((2,))]),
        out_shape=[jax.ShapeDtypeStruct((nb, 1, past), f32), jax.ShapeDtypeStruct((nb, 1, nb), f32)],
        compiler_params=pltpu.CompilerParams(dimension_semantics=("arbitrary",), vmem_limit_bytes=VMEM_LIMIT),
        name="sample_indexer",
    )(pt, qi8, wi8, knew, kidx_t_pool)


def _ssel_kernel(sct_ref, dn_ref, out_ref, sc_ref, bias_ref):
    nc, TK, Q = sc_ref.shape
    for c in range(nc - 1):
        sc_ref[c] = sct_ref[c * TK:(c + 1) * TK, :]
    row = lax.broadcasted_iota(jnp.int32, (TK, Q), 0)
    col = lax.broadcasted_iota(jnp.int32, (TK, Q), 1)
    sc_ref[nc - 1] = jnp.where(row == col, dn_ref[...], -jnp.inf)
    n_valid = (nc - 1) * TK + 1
    k_eff = jnp.full((1, Q), float(min(TOPK_MAX, n_valid // 4)), f32)
    _select_bias_t(sc_ref, bias_ref, nc, k_eff, jnp.full((1, Q), float(n_valid), f32))
    for c in range(nc):
        out_ref[c * TK:(c + 1) * TK, :] = bias_ref[c]


def _ssel(sc_t, dn_t):
    past, nb = sc_t.shape
    nc = past // nb + 1
    return pl.pallas_call(
        _ssel_kernel,
        out_shape=jax.ShapeDtypeStruct((past + nb, nb), f32),
        scratch_shapes=[pltpu.VMEM((nc, nb, nb), f32), pltpu.VMEM((nc, nb, nb), f32)],
        compiler_params=pltpu.CompilerParams(vmem_limit_bytes=VMEM_LIMIT),
        name="sample_select",
    )(sc_t, dn_t)


def _sample_copies(pt_ref, s, slot, k_hbm, v_hbm, kbuf, vbuf, semk, semv):
    n_pages, page = pt_ref.shape[1], k_hbm.shape[2]
    return (_paged_copies(pt_ref, s, k_hbm, kbuf, semk, slot, n_pages, page)
            + _paged_copies(pt_ref, s, v_hbm, vbuf, semv, slot, n_pages, page))


def _sample_prefetch(pt_ref, s, ns, k_hbm, v_hbm, kbuf, vbuf, semk, semv):
    slot = lax.rem(s, 2)

    @pl.when(s == 0)
    def _():
        for cp in _sample_copies(pt_ref, 0, 0, k_hbm, v_hbm, kbuf, vbuf, semk, semv):
            cp.start()

    @pl.when(s + 1 < ns)
    def _():
        for cp in _sample_copies(pt_ref, s + 1, 1 - slot, k_hbm, v_hbm, kbuf, vbuf, semk, semv):
            cp.start()


def _sample_attend(pt_ref, s, qbd_ref, bias_ref, knew_ref, vnew_ref, qmbd_ref, cmk_ref, cmv_ref, k_hbm, v_hbm,
                   o_ref, mo_ref, kbuf, vbuf, semk, semv):
    past = kbuf.shape[2]
    nb = knew_ref.shape[0]
    slot = lax.rem(s, 2)

    qm = qmbd_ref[0]
    pm, lm = _softmax_parts(_dot(qm, cmk_ref[0].astype(bf16)))
    mo = _dot_nt(pm.astype(bf16), cmv_ref[0].astype(bf16)) / lm
    for h in range(N_MEM_HEADS):
        hs = slice(h * HEAD_DIM, (h + 1) * HEAD_DIM)
        mo_ref[0, :, hs] = mo[h:h + 1, hs]

    for cp in _sample_copies(pt_ref, s, slot, k_hbm, v_hbm, kbuf, vbuf, semk, semv):
        cp.wait()

    qbd = qbd_ref[0]

    def softmax_part(logits, value_product):
        m_j = jnp.max(logits, axis=1, keepdims=True)
        p_j = jnp.exp2(logits - m_j)
        return m_j, jnp.sum(p_j, axis=1, keepdims=True), value_product(p_j.astype(bf16))

    seg = min(SAMPLE_SEG, past)
    parts = []
    for j in range(past // seg):
        ks = slice(j * seg, (j + 1) * seg)
        parts.append(softmax_part(_dot(qbd, kbuf[slot, :, ks].astype(bf16)) + bias_ref[0, :, ks],
                                  lambda p_j, ks=ks: _dot_nt(p_j, vbuf[slot, :, ks].astype(bf16))))
    parts.append(softmax_part(_dot_nt(qbd, knew_ref[...]) + bias_ref[0, :, past:past + nb],
                              lambda p_j: _dot(p_j, vnew_ref[...])))
    m = functools.reduce(jnp.maximum, [m_j for m_j, _, _ in parts])
    scales = [jnp.exp2(m_j - m) for m_j, _, _ in parts]
    l = sum(l_j * c for (_, l_j, _), c in zip(parts, scales))
    o = sum(o_j * c for (_, _, o_j), c in zip(parts, scales)) / l
    for p in range(N_ATT_HEADS):
        g0 = (p % 2) * HEAD_DIM
        o_ref[0, :, p * HEAD_DIM:(p + 1) * HEAD_DIM] = o[p:p + 1, g0:g0 + HEAD_DIM]


def _sattn_kernel(pt_ref, qbd_ref, bias_ref, knew_ref, vnew_ref, qmbd_ref, cmk_ref, cmv_ref, k_hbm, v_hbm,
                  o_ref, mo_ref, kbuf, vbuf, semk, semv):
    s = pl.program_id(0)
    _sample_prefetch(pt_ref, s, pl.num_programs(0), k_hbm, v_hbm, kbuf, vbuf, semk, semv)
    _sample_attend(pt_ref, s, qbd_ref, bias_ref, knew_ref, vnew_ref, qmbd_ref, cmk_ref, cmv_ref, k_hbm, v_hbm,
                   o_ref, mo_ref, kbuf, vbuf, semk, semv)


def _sattn_specs(pt, qbd, bias, knew, vnew, qmbd, cmk_t, cmv_t, k_t_pool, v_t_pool, index):
    nb, n_pages = pt.shape
    past = n_pages * k_t_pool.shape[2]
    M = cmk_t.shape[2]
    per = lambda r, c: pl.BlockSpec((1, r, c), lambda *g: (index(*g[:-1]), 0, 0))
    full = lambda r, c: pl.BlockSpec((r, c), lambda *g: (0, 0))
    in_specs = [per(N_ATT_HEADS, D_KV), per(1, past + nb), full(nb, D_KV), full(nb, D_KV),
                per(N_MEM_HEADS, D_MEM), per(D_MEM, M), per(D_MEM, M),
                pl.BlockSpec(memory_space=pl.ANY), pl.BlockSpec(memory_space=pl.ANY)]
    out_specs = [per(1, D_ATT), per(1, D_MEM)]
    out_shapes = [jax.ShapeDtypeStruct((nb, 1, D_ATT), f32), jax.ShapeDtypeStruct((nb, 1, D_MEM), f32)]
    scratch = [pltpu.VMEM((2, D_KV, past), f32), pltpu.VMEM((2, D_KV, past), f32),
               pltpu.SemaphoreType.DMA((2,)), pltpu.SemaphoreType.DMA((2,))]
    return in_specs, out_specs, out_shapes, scratch


def _sattn(pt, *ops):
    in_specs, out_specs, out_shapes, scratch = _sattn_specs(pt, *ops, index=lambda b: b)
    return pl.pallas_call(
        _sattn_kernel,
        grid_spec=pltpu.PrefetchScalarGridSpec(num_scalar_prefetch=1, grid=(pt.shape[0],), in_specs=in_specs,
                                               out_specs=out_specs, scratch_shapes=scratch),
        out_shape=out_shapes,
        compiler_params=pltpu.CompilerParams(dimension_semantics=("arbitrary",), vmem_limit_bytes=VMEM_LIMIT),
        name="sample_attention",
    )(pt, *ops)


def _smerge_kernel(att_ref, ga_ref, cvg_ref, mo_ref, gm_ref, x_ref, wo_ref, gp_ref, y_ref):
    att_g = (att_ref[...] * ga_ref[...]).astype(bf16)
    mg = (mo_ref[...] * gm_ref[...]).astype(bf16)
    y_ref[...] = _merge(att_g, cvg_ref[...], mg, x_ref[...], wo_ref, gp_ref[...])


def _smerge(att, ga, cvg, mo, gm, x, wo, gp):
    return pl.pallas_call(
        _smerge_kernel,
        out_shape=jax.ShapeDtypeStruct(x.shape, f32),
        compiler_params=pltpu.CompilerParams(vmem_limit_bytes=VMEM_LIMIT),
        name="sample_merge",
    )(att, ga, cvg, mo, gm, x, wo, gp)


def _rope_tables(pos):
    half = HEAD_DIM // 2
    inv = ROPE_THETA ** (-jnp.arange(half, dtype=f32) / half)
    ang = pos.astype(f32)[:, None] * inv[None, :]
    cos, sin = jnp.cos(ang), jnp.sin(ang)
    zero = jnp.zeros_like(sin)
    rep = LANES // HEAD_DIM
    return (jnp.tile(jnp.concatenate([cos, cos], axis=1), (1, rep)),
            jnp.tile(jnp.concatenate([-sin, zero], axis=1), (1, rep)),
            jnp.tile(jnp.concatenate([zero, sin], axis=1), (1, rep)))


def _pack_w_in(w):
    head = lambda base: jnp.concatenate([w[:, base + h * HEAD_DIM:base + (h + 1) * HEAD_DIM] for h in ATT_ORDER], 1)
    kidx = w[:, R_KIDX:R_WIDX]
    packed = jnp.concatenate([head(R_Q), w[:, R_K:R_GATT], head(R_GATT), w[:, R_QIDX:R_KIDX], kidx, kidx,
                              w[:, R_U:]], axis=1)
    w_t = jnp.concatenate([w[:, R_V:R_GATT], w[:, R_WIDX:R_U], jnp.zeros((D_MODEL, D_T - D_KV - N_IDX_HEADS), w.dtype)],
                          axis=1).T
    return packed.astype(bf16), w_t.astype(bf16)


def _pack_w_out(w):
    att = jnp.concatenate([w[h * HEAD_DIM:(h + 1) * HEAD_DIM] for h in ATT_ORDER], axis=0)
    return jnp.concatenate([att, w[D_ATT:]], axis=0).astype(bf16)


def _slab_rows(x, n_rows):
    T = x.shape[0]
    slabs = jnp.repeat(x.reshape(T, n_rows // 2, 1, 2, HEAD_DIM), 2, axis=2)
    keep = jnp.arange(2)[:, None] == jnp.arange(2)[None, :]
    return jnp.where(keep[None, None, :, :, None], slabs, jnp.zeros((), x.dtype)).reshape(T, n_rows, LANES)


def _diag_rows(x, n_heads):
    T = x.shape[0]
    keep = jnp.arange(n_heads)[:, None] == jnp.arange(n_heads)[None, :]
    xh = x.reshape(T, 1, n_heads, HEAD_DIM)
    return jnp.where(keep[None, :, :, None], xh, jnp.zeros((), x.dtype)).reshape(T, n_heads, n_heads * HEAD_DIM)


def _layer(l, xp, xs, mem_prompt, cache_k, cache_v, cache_kidx, cache_mem_k, cache_mem_v, state_conv,
           page_table, g_pre, w_in, conv_w, g_memnorm, w_mem_kv, w_out, g_post):
    B, S, _ = xp.shape
    nb, T, _ = xs.shape
    assert T == 1
    n_pages = page_table.shape[1]
    n_pool, page = cache_k.shape[1], cache_k.shape[2]
    past = n_pages * page
    M = mem_prompt.shape[1]
    assert page == nb and past % 1024 == 0 and past % min(SAMPLE_SEG, past) == 0

    row = lambda a: a[l][None, :]
    w_pack, w_t = _pack_w_in(w_in[l])
    wo = _pack_w_out(w_out[l])
    gp = row(g_post)

    mk, mv, mkb, mvb = _memkv(mem_prompt, row(g_memnorm), w_mem_kv[l].astype(bf16))
    cos, sa, sb = _rope_tables(jnp.arange(S, dtype=jnp.int32))
    TQ = min(256, S)
    TM = min(1024, S)
    (q, k, v, kb, vt, ga, qi, ki, kk, wit, cvg, cst, mg) = _front_prompt(
        xp, row(g_pre), w_pack, w_t, cos, sa, sb, conv_w[l], mkb, mvb, TM, TQ)

    cos, sa, sb = _rope_tables(jnp.full((nb,), past, jnp.int32))
    st = state_conv[l]
    (q_s, k_s, v_s, kb_s, vt_s, ga_s, qi_s, ki_s, kk_s, wit_s, cvg_s, u_s, qm_s, gm_s) = _front_sample(
        xs.reshape(1, nb, D_MODEL), row(g_pre), w_pack, w_t, cos, sa, sb, conv_w[l], st[:, 0, :], st[:, 1, :])
    kidx_t = jnp.swapaxes(cache_kidx[l], 1, 2)
    k_t = jnp.transpose(cache_k[l], (0, 2, 3, 1)).reshape(n_pool, D_KV, page)
    v_t = jnp.transpose(cache_v[l], (0, 2, 3, 1)).reshape(n_pool, D_KV, page)
    cmk_t = jnp.transpose(cache_mem_k[l], (0, 2, 3, 1)).reshape(nb, D_MEM, M)
    cmv_t = jnp.transpose(cache_mem_v[l], (0, 2, 3, 1)).reshape(nb, D_MEM, M)

    sc, dn = _sidx(page_table, qi_s.reshape(nb, N_IDX_HEADS, IDX_DIM), wit_s[0].T.reshape(nb, N_IDX_HEADS, 1),
                   kk_s[0, :, :IDX_DIM], kidx_t)
    bias_t = _ssel(sc.reshape(nb, past).T, dn.reshape(nb, nb).T)
    sample_ops = (page_table, _slab_rows(q_s[0], N_ATT_HEADS), bias_t.T.reshape(nb, 1, past + nb),
                  kb_s[0], vt_s[0, 0].T, _diag_rows(qm_s[0], N_MEM_HEADS), cmk_t, cmv_t, k_t, v_t)

    prompt_ops = (q, qi, wit, kb, vt, kk, ga, cvg, mg, xp, wo, gp, TQ)
    if nb == B * (S // TQ):
        yp, att_s, mo_s = _attn_prompt(*prompt_ops, sample=sample_ops)
    else:
        yp = _attn_prompt(*prompt_ops)
        att_s, mo_s = _sattn(*sample_ops)
    ys = _smerge(att_s.reshape(nb, D_ATT), ga_s[0], cvg_s[0], mo_s.reshape(nb, D_MEM), gm_s[0],
                 xs.reshape(nb, D_MODEL), wo, gp)

    outs_p = (k.reshape(B, S, N_KV_HEADS, HEAD_DIM), v.reshape(B, S, N_KV_HEADS, HEAD_DIM), ki, cst,
              mk.reshape(B, M, N_MEM_HEADS, HEAD_DIM), mv.reshape(B, M, N_MEM_HEADS, HEAD_DIM))
    outs_s = (k_s.reshape(nb, 1, N_KV_HEADS, HEAD_DIM), v_s.reshape(nb, 1, N_KV_HEADS, HEAD_DIM),
              ki_s.reshape(nb, 1, IDX_DIM), jnp.stack([st[:, 1, :], u_s[0]], axis=1))
    return yp, ys.reshape(nb, 1, D_MODEL), outs_p, outs_s


def kernel(x_prompt, x_sample, mem_prompt, cache_k, cache_v, cache_kidx, cache_mem_k, cache_mem_v, state_conv,
           page_table, g_pre, w_in, conv_w, g_memnorm, w_mem_kv, w_out, g_post):
    depth = w_in.shape[0]
    xp, xs = x_prompt, x_sample
    per_p, per_s = [], []
    for l in range(depth):
        xp, xs, op, os_ = _layer(l, xp, xs, mem_prompt, cache_k, cache_v, cache_kidx, cache_mem_k, cache_mem_v,
                                 state_conv, page_table, g_pre, w_in, conv_w, g_memnorm, w_mem_kv, w_out, g_post)
        per_p.append(op)
        per_s.append(os_)
    stack = lambda outs, i: jnp.stack([o[i] for o in outs])
    return ((xp, xs) + tuple(stack(per_p, i) for i in range(6)) + tuple(stack(per_s, i) for i in range(4)))
```

```python
import functools

import jax
import jax.numpy as jnp
from jax import lax
from jax.experimental import pallas as pl
from jax.experimental.pallas import tpu as pltpu

D_MODEL = 1024
HEAD_DIM = 64
N_MEM_HEADS = 4
D_MEM = N_MEM_HEADS * HEAD_DIM
D_ATT = 512
N_ATT_HEADS = 8
N_KV_HEADS = 2
D_KV = N_KV_HEADS * HEAD_DIM
D_CONV = 256
N_IDX_HEADS = 8
IDX_DIM = 64
D_QIDX = N_IDX_HEADS * IDX_DIM
TOPK_MAX = 256
CONV_WIDTH = 3
ROPE_THETA = 10000.0
RMS_EPS = 1e-6
LANES = 128

R_Q = 0
R_K = R_Q + D_ATT
R_V = R_K + D_KV
R_GATT = R_V + D_KV
R_QIDX = R_GATT + D_ATT
R_KIDX = R_QIDX + D_QIDX
R_WIDX = R_KIDX + IDX_DIM
R_U = R_WIDX + N_IDX_HEADS

ATT_ORDER = (0, 4, 1, 5, 2, 6, 3, 7)
P_Q = 0
P_K = P_Q + D_ATT
P_V = P_K + D_KV
P_GATT = P_V + D_KV
P_QIDX = P_GATT + D_ATT
P_KK = P_QIDX + D_QIDX
P_U = P_KK + LANES
P_B = P_U + D_CONV
P_C = P_B + D_CONV
P_GCONV = P_C + D_CONV
P_QMEM = P_GCONV + D_CONV
P_GMEM = P_QMEM + D_MEM
P_END = P_GMEM + D_MEM

QK_SCALE = HEAD_DIM ** -0.5
LOG2E = 1.4426950408889634
IDX_SCALE = IDX_DIM ** -0.5
W_SCALE = N_IDX_HEADS ** -0.5

F32_MAX = 3.4028234663852886e38
SAMPLE_SEG = 4096
FIXED_PASSES = 16
KEY_PASS_EVERY = 16
MAX_PASSES = KEY_PASS_EVERY * 34
NEG = -1e30
VMEM_LIMIT = 56 * 1024 * 1024
ACC_ROWS = 32
D_T = D_KV + 16

bf16 = jnp.bfloat16
f32 = jnp.float32
NT = (((1,), (1,)), ((), ()))


def _dot(a, b):
    return jnp.dot(a, b, preferred_element_type=f32)


def _dot_nt(a, b):
    return lax.dot_general(a, b, NT, preferred_element_type=f32)


def _rms(x, g):
    return x * lax.rsqrt(jnp.mean(x * x, axis=-1, keepdims=True) + RMS_EPS) * g


def _silu(x):
    return x / (1.0 + jnp.exp(-x))


def _rope(z, cos, sa, sb):
    return z * cos + pltpu.roll(z, 96, 1) * sa + pltpu.roll(z, 32, 1) * sb


def _key_to_f32(k):
    return lax.bitcast_convert_type(k ^ ((k >> 31) & jnp.int32(0x7FFFFFFF)), f32)


def _softmax_parts(s):
    m = jnp.max(s, axis=-1, keepdims=True)
    p = jnp.exp(s - m)
    return p, jnp.sum(p, axis=-1, keepdims=True)


def _f32_to_key(x):
    bits = lax.bitcast_convert_type(x, jnp.int32)
    return bits ^ ((bits >> 31) & jnp.int32(0x7FFFFFFF))


def _select_bias_t(sc_ref, bias_ref, nc, k_eff, n_valid):
    _, TK, Q = sc_ref.shape
    groups = -(-TOPK_MAX // TK)

    def count_ge(thr):
        thr_b = jnp.broadcast_to(thr, (ACC_ROWS, Q))

        def body(c, acc):
            for j in range(TK // ACC_ROWS):
                acc = acc + jnp.where(sc_ref[c, j * ACC_ROWS:(j + 1) * ACC_ROWS, :] >= thr_b, 1.0, 0.0)
            return acc

        acc = lax.fori_loop(0, nc, body, jnp.zeros((ACC_ROWS, Q), f32))
        return jnp.sum(acc, axis=0, keepdims=True)

    if groups == 1:
        bias_ref[0] = sc_ref[0]

        def gmax_body(c, carry):
            bias_ref[0] = jnp.maximum(bias_ref[0], sc_ref[c])
            return carry

        lax.fori_loop(1, nc, gmax_body, 0)
    else:
        for c in range(nc):
            bias_ref[c % groups] = sc_ref[c] if c < groups else jnp.maximum(bias_ref[c % groups], sc_ref[c])
    gmaxs = [bias_ref[g] for g in range(groups)]
    low = jnp.min(functools.reduce(jnp.minimum, gmaxs), axis=0, keepdims=True)
    top = jnp.max(functools.reduce(jnp.maximum, gmaxs), axis=0, keepdims=True)

    low_b = jnp.broadcast_to(low, (ACC_ROWS, Q))

    def start_body(c, accs):
        a_low, a_ge0, a_gt0 = accs
        for j in range(TK // ACC_ROWS):
            x = sc_ref[c, j * ACC_ROWS:(j + 1) * ACC_ROWS, :]
            a_low = a_low + jnp.where(x >= low_b, 1.0, 0.0)
            a_ge0 = a_ge0 + jnp.where(x >= 0.0, 1.0, 0.0)
            a_gt0 = a_gt0 + jnp.where(x > 0.0, 1.0, 0.0)
        return a_low, a_ge0, a_gt0

    zeros = jnp.zeros((ACC_ROWS, Q), f32)
    c_low, c_ge0, c_gt0 = [jnp.sum(a, axis=0, keepdims=True)
                           for a in lax.fori_loop(0, nc, start_body, (zeros, zeros, zeros))]
    all_in = n_valid == k_eff
    at_zero = (c_gt0 < k_eff) & (c_ge0 >= k_eff)
    above = c_gt0 >= k_eff
    low_ok = low > -F32_MAX
    lo = jnp.where(all_in, -F32_MAX, jnp.where(
        at_zero, 0.0, jnp.where(above, jnp.maximum(low, 0.0), jnp.where(low_ok, low, -F32_MAX))))
    c_lo = jnp.where(all_in, n_valid, jnp.where(
        at_zero, c_ge0, jnp.where(above, jnp.where(low > 0.0, c_low, c_ge0), jnp.where(low_ok, c_low, n_valid))))
    hi = jnp.where(above, _key_to_f32(_f32_to_key(top) + 1), 0.0)
    c_hi = jnp.where(at_zero, c_gt0, jnp.where(above, 0.0, c_ge0))

    settled = all_in | at_zero
    halfway = lambda lo, hi: lo + (hi - lo) * 0.5

    def still_open(lo, hi, c_lo):
        mid = halfway(lo, hi)
        return jnp.where(jnp.logical_not(settled) & (c_lo != k_eff) & (mid > lo) & (mid < hi), 1.0, 0.0)

    def one_pass(it, state):
        lo, hi, c_lo, c_hi, w_lo, w_hi, moved = state
        active = still_open(lo, hi, c_lo) > 0.5
        k_lo, k_hi = _f32_to_key(lo), _f32_to_key(hi)
        a = (c_lo - k_eff + 0.5) * w_lo
        b = (k_eff - 0.5 - c_hi) * w_hi
        by_count = lo + (hi - lo) * (a / (a + b))
        by_key = _key_to_f32(k_lo + lax.shift_right_logical(k_hi - k_lo, 1))
        pivot = jnp.where(it % KEY_PASS_EVERY == KEY_PASS_EVERY - 1, by_key, by_count)
        pivot = jnp.where((pivot > lo) & (pivot < hi), pivot, halfway(lo, hi))
        pivot = jnp.where(active, pivot, lo)
        cnt = count_ge(pivot)
        up = active & (cnt >= k_eff)
        down = active & (cnt < k_eff)
        w_hi = jnp.where(up, jnp.where(moved > 0.5, w_hi * 0.5, 1.0), jnp.where(down, 1.0, w_hi))
        w_lo = jnp.where(down, jnp.where(moved < -0.5, w_lo * 0.5, 1.0), jnp.where(up, 1.0, w_lo))
        moved = jnp.where(up, 1.0, jnp.where(down, -1.0, moved))
        lo, c_lo = jnp.where(up, pivot, lo), jnp.where(up, cnt, c_lo)
        hi, c_hi = jnp.where(down, pivot, hi), jnp.where(down, cnt, c_hi)
        return lo, hi, c_lo, c_hi, w_lo, w_hi, moved

    ones = jnp.ones((1, Q), f32)
    state = lax.fori_loop(0, FIXED_PASSES, one_pass, (lo, hi, c_lo, c_hi, ones, ones, jnp.zeros((1, Q), f32)))
    any_open = lambda st: jnp.max(still_open(st[0], st[1], st[2]))

    def more_passes(carry):
        it, state, _ = carry
        state = one_pass(it + 1, one_pass(it, state))
        return it + 2, state, any_open(state)

    _, state, _ = lax.while_loop(lambda carry: (carry[2] > 0.5) & (carry[0] < MAX_PASSES), more_passes,
                                 (jnp.int32(FIXED_PASSES), state, any_open(state)))
    thr, cnt_u, cnt_above = state[0], state[2], state[3]
    has_tie = jnp.max(jnp.where(cnt_u != k_eff, 1.0, 0.0)) > 0.5

    @pl.when(jnp.logical_not(has_tie))
    def _():
        def bias_body(c, carry):
            bias_ref[c] = jnp.where(sc_ref[c] >= thr, 0.0, NEG)
            return carry

        lax.fori_loop(0, nc, bias_body, 0)

    @pl.when(has_tie)
    def _():
        need = jnp.where(cnt_u == k_eff, F32_MAX, k_eff - cnt_above)
        r_i = lax.broadcasted_iota(jnp.int32, (TK, TK), 0)
        c_i = lax.broadcasted_iota(jnp.int32, (TK, TK), 1)
        tri = jnp.where(c_i <= r_i, 1.0, 0.0).astype(bf16)

        def tie_chunk(c, run):
            x = sc_ref[c]
            eq = x == thr
            incl = _dot(tri, jnp.where(eq, 1.0, 0.0).astype(bf16))
            tied = jnp.where(incl < need + 0.5 - run, jnp.where(eq, 0.0, NEG), NEG)
            bias_ref[c] = jnp.where(x > thr, 0.0, tied)
            return run + incl[TK - 1:TK, :]

        def bias_body(j, run):
            return tie_chunk(2 * j + 1, tie_chunk(2 * j, run))

        run = lax.fori_loop(0, nc // 2, bias_body, jnp.zeros((1, Q), f32))
        lax.cond(nc % 2 == 1, lambda r: tie_chunk(nc - 1, r), lambda r: r, run)


def _merge(att_g, cvg, mg, x, wo_ref, gp):
    o = (_dot(att_g, wo_ref[0:D_ATT, :])
         + _dot(cvg, wo_ref[D_ATT:D_ATT + D_CONV, :])
         + _dot(mg, wo_ref[D_ATT + D_CONV:D_MODEL, :]))
    return x + _rms(o, gp)


def _memkv_kernel(mem_ref, g_ref, w_ref, mk_ref, mv_ref, mkb_ref, mvb_ref):
    kv = _dot(_rms(mem_ref[0], g_ref[...]).astype(bf16), w_ref[...])
    mk = kv[:, :D_MEM]
    mv = kv[:, D_MEM:]
    mk_ref[0] = mk
    mv_ref[0] = mv
    mkb_ref[0] = mk.astype(bf16)
    mvb_ref[0] = mv.astype(bf16)


def _memkv(mem, g, w):
    B, M, _ = mem.shape
    blk = lambda: pl.BlockSpec((1, M, D_MEM), lambda b: (b, 0, 0))
    return pl.pallas_call(
        _memkv_kernel,
        grid=(B,),
        in_specs=[pl.BlockSpec((1, M, D_MODEL), lambda b: (b, 0, 0)),
                  pl.BlockSpec((1, D_MODEL), lambda b: (0, 0)),
                  pl.BlockSpec((D_MODEL, 2 * D_MEM), lambda b: (0, 0))],
        out_specs=[blk(), blk(), blk(), blk()],
        out_shape=[jax.ShapeDtypeStruct((B, M, D_MEM), f32)] * 2
        + [jax.ShapeDtypeStruct((B, M, D_MEM), bf16)] * 2,
        compiler_params=pltpu.CompilerParams(dimension_semantics=("arbitrary",)),
        name="memkv",
    )(mem, g, w)


def _front_common(x_ref, g_ref, w_ref, wt_ref, cos_ref, sa_ref, sb_ref,
                  q_ref, k_ref, v_ref, kb_ref, vt_ref, ga_ref, qi_ref, ki_ref, kk_ref, wit_ref):
    xb = _rms(x_ref[0], g_ref[...]).astype(bf16)
    cos, sa, sb = cos_ref[...], sa_ref[...], sb_ref[...]
    proj = lambda lo, hi: _dot(xb, w_ref[:, lo:hi])
    rope = lambda z: _rope(z, cos, sa, sb)

    zq = proj(P_Q, P_K)
    zqi = proj(P_QIDX, P_KK)
    for s in range(D_ATT // LANES):
        sl = slice(s * LANES, (s + 1) * LANES)
        q_ref[0, :, sl] = (rope(zq[:, sl]) * (QK_SCALE * LOG2E)).astype(bf16)
        qi_ref[0, :, sl] = (rope(zqi[:, sl]) * IDX_SCALE).astype(bf16)
    zkv = proj(P_K, P_GATT)
    zk = rope(zkv[:, :D_KV])
    k_ref[0] = zk
    kb_ref[0] = zk.astype(bf16)
    v_ref[0] = zkv[:, D_KV:]
    ga_ref[0] = _silu(proj(P_GATT, P_QIDX))
    zkk = rope(proj(P_KK, P_U))
    ki_ref[0] = zkk[:, :IDX_DIM]
    kk_ref[0] = zkk.astype(bf16)
    vw_t = _dot_nt(wt_ref[...], xb)
    TK = vt_ref.shape[3]
    for c in range(vt_ref.shape[1]):
        vt_ref[0, c] = vw_t[0:D_KV, c * TK:(c + 1) * TK].astype(bf16)
    wit_ref[0] = vw_t[D_KV:D_KV + N_IDX_HEADS, :] * W_SCALE
    return proj


def _front_prompt_kernel(x_ref, g_ref, w_ref, wt_ref, cos_ref, sa_ref, sb_ref, cw_ref, mkb_ref, mvb_ref,
                         q_ref, k_ref, v_ref, kb_ref, vt_ref, ga_ref, qi_ref, ki_ref, kk_ref, wit_ref,
                         cvg_ref, cst_ref, mg_ref, ubuf):
    TM = x_ref.shape[1]
    proj = _front_common(x_ref, g_ref, w_ref, wt_ref, cos_ref, sa_ref, sb_ref,
                         q_ref, k_ref, v_ref, kb_ref, vt_ref, ga_ref, qi_ref, ki_ref, kk_ref, wit_ref)

    @pl.when(pl.program_id(1) == 0)
    def _():
        ubuf[0:8, :] = jnp.zeros((8, D_CONV), f32)

    u = proj(P_C, P_GCONV) * proj(P_U, P_B)
    ubuf[8:TM + 8, :] = u
    cw = cw_ref[...]
    conv = proj(P_B, P_C) * (cw[0:1] * ubuf[6:TM + 6, :] + cw[1:2] * ubuf[7:TM + 7, :] + cw[2:3] * u)
    cvg_ref[0] = (conv * _silu(proj(P_GCONV, P_QMEM))).astype(bf16)
    tail = u[TM - 2:TM, :]
    ubuf[6:8, :] = tail
    cst_ref[0] = tail

    qm = (proj(P_QMEM, P_GMEM) * QK_SCALE).astype(bf16)
    mkb = mkb_ref[0]
    mvb = mvb_ref[0]
    outs = []
    for h in range(N_MEM_HEADS):
        sl = slice(h * HEAD_DIM, (h + 1) * HEAD_DIM)
        p, l = _softmax_parts(_dot_nt(qm[:, sl], mkb[:, sl]))
        outs.append(_dot(p.astype(bf16), mvb[:, sl]) / l)
    mo = jnp.concatenate(outs, axis=1)
    mg_ref[0] = (mo * _silu(proj(P_GMEM, P_END))).astype(bf16)


def _front_sample_kernel(x_ref, g_ref, w_ref, wt_ref, cos_ref, sa_ref, sb_ref, cw_ref, st0_ref, st1_ref,
                         q_ref, k_ref, v_ref, kb_ref, vt_ref, ga_ref, qi_ref, ki_ref, kk_ref, wit_ref,
                         cvg_ref, u_ref, qm_ref, gm_ref):
    proj = _front_common(x_ref, g_ref, w_ref, wt_ref, cos_ref, sa_ref, sb_ref,
                         q_ref, k_ref, v_ref, kb_ref, vt_ref, ga_ref, qi_ref, ki_ref, kk_ref, wit_ref)
    u = proj(P_C, P_GCONV) * proj(P_U, P_B)
    cw = cw_ref[...]
    conv = proj(P_B, P_C) * (cw[0:1] * st0_ref[...] + cw[1:2] * st1_ref[...] + cw[2:3] * u)
    cvg_ref[0] = (conv * _silu(proj(P_GCONV, P_QMEM))).astype(bf16)
    u_ref[0] = u
    qm_ref[0] = (proj(P_QMEM, P_GMEM) * QK_SCALE).astype(bf16)
    gm_ref[0] = _silu(proj(P_GMEM, P_END))


def _front_out(B, S, TM, TK):
    tok = lambda d: pl.BlockSpec((1, TM, d), lambda b, j: (b, j, 0))
    specs = [tok(D_ATT), tok(D_KV), tok(D_KV), tok(D_KV),
             pl.BlockSpec((1, TM // TK, D_KV, TK), lambda b, j: (b, j, 0, 0)),
             tok(D_ATT), tok(D_QIDX), tok(IDX_DIM), tok(LANES),
             pl.BlockSpec((1, N_IDX_HEADS, TM), lambda b, j: (b, 0, j))]
    sds = lambda d, t: jax.ShapeDtypeStruct((B, S, d), t)
    shapes = [sds(D_ATT, bf16), sds(D_KV, f32), sds(D_KV, f32), sds(D_KV, bf16),
              jax.ShapeDtypeStruct((B, S // TK, D_KV, TK), bf16),
              sds(D_ATT, f32), sds(D_QIDX, bf16), sds(IDX_DIM, f32), sds(LANES, bf16),
              jax.ShapeDtypeStruct((B, N_IDX_HEADS, S), f32)]
    return tok, sds, specs, shapes


def _front_in(TM):
    full = lambda r, c: pl.BlockSpec((r, c), lambda b, j: (0, 0))
    rows = lambda: pl.BlockSpec((TM, LANES), lambda b, j: (j, 0))
    return [pl.BlockSpec((1, TM, D_MODEL), lambda b, j: (b, j, 0)), full(1, D_MODEL), full(D_MODEL, P_END),
            full(D_T, D_MODEL), rows(), rows(), rows(), full(CONV_WIDTH, D_CONV)]


def _front_prompt(x, g, w, wt, cos, sa, sb, cw, mkb, mvb, TM, TK):
    B, S, _ = x.shape
    M = mkb.shape[1]
    tok, sds, specs, shapes = _front_out(B, S, TM, TK)
    mem = lambda: pl.BlockSpec((1, M, D_MEM), lambda b, j: (b, 0, 0))
    return pl.pallas_call(
        _front_prompt_kernel,
        grid=(B, S // TM),
        in_specs=_front_in(TM) + [mem(), mem()],
        out_specs=specs + [tok(D_CONV), pl.BlockSpec((1, CONV_WIDTH - 1, D_CONV), lambda b, j: (b, 0, 0)),
                           tok(D_MEM)],
        out_shape=shapes + [sds(D_CONV, bf16), jax.ShapeDtypeStruct((B, CONV_WIDTH - 1, D_CONV), f32),
                            sds(D_MEM, bf16)],
        scratch_shapes=[pltpu.VMEM((TM + 8, D_CONV), f32)],
        compiler_params=pltpu.CompilerParams(dimension_semantics=("arbitrary", "arbitrary"),
                                             vmem_limit_bytes=VMEM_LIMIT),
        name="front_prompt",
    )(x, g, w, wt, cos, sa, sb, cw, mkb, mvb)


def _front_sample(x, g, w, wt, cos, sa, sb, cw, st0, st1):
    _, T, _ = x.shape
    tok, sds, specs, shapes = _front_out(1, T, T, T)
    st = lambda: pl.BlockSpec((T, D_CONV), lambda b, j: (0, 0))
    return pl.pallas_call(
        _front_sample_kernel,
        grid=(1, 1),
        in_specs=_front_in(T) + [st(), st()],
        out_specs=specs + [tok(D_CONV), tok(D_CONV), tok(D_MEM), tok(D_MEM)],
        out_shape=shapes + [sds(D_CONV, bf16), sds(D_CONV, f32), sds(D_MEM, bf16), sds(D_MEM, f32)],
        compiler_params=pltpu.CompilerParams(dimension_semantics=("arbitrary", "arbitrary"),
                                             vmem_limit_bytes=VMEM_LIMIT),
        name="front_sample",
    )(x, g, w, wt, cos, sa, sb, cw, st0, st1)


def _half_masked(slab, half):
    lane = lax.broadcasted_iota(jnp.int32, slab.shape, 1)
    keep = (lane < HEAD_DIM) if half == 0 else (lane >= HEAD_DIM)
    return jnp.where(keep, slab, jnp.zeros((), slab.dtype))


def _attn_prompt_body(q_ref, qi_ref, wit_ref, kb_ref, vt_ref, kk_ref, ga_ref, cvg_ref, mg_ref, x_ref,
                      wo_ref, gp_ref, y_ref, sc_ref, bias_ref, acc_ref):
    TQ = q_ref.shape[1]
    TK = TQ
    S = kb_ref.shape[1]
    i = pl.program_id(1)
    chunk = lambda c: pl.ds(pl.multiple_of(c * TK, TK), TK)
    slab = lambda ref, p: ref[0, :, (p // 2) * LANES:(p // 2 + 1) * LANES]

    blk = lambda a, p: a[:, p * TQ:(p + 1) * TQ]
    qi_all = jnp.concatenate([_half_masked(slab(qi_ref, h), h % 2) for h in range(N_IDX_HEADS)], axis=0)
    wit = wit_ref[0]

    def scores_t(c):
        d = _dot_nt(kk_ref[0, chunk(c), :], qi_all)
        acc = jnp.maximum(blk(d, 0), 0.0) * wit[0:1, :]
        for h in range(1, N_IDX_HEADS):
            acc = acc + jnp.maximum(blk(d, h), 0.0) * wit[h:h + 1, :]
        return acc

    def score_body(j, carry):
        for r in range(2):
            sc_ref[2 * j + r] = scores_t(2 * j + r)
        return carry

    lax.fori_loop(0, lax.shift_right_logical(i, 1), score_body, 0)

    @pl.when(i % 2 == 1)
    def _():
        sc_ref[i - 1] = scores_t(i - 1)

    key_pos = lax.broadcasted_iota(jnp.int32, (TK, TQ), 0)
    qry_pos = lax.broadcasted_iota(jnp.int32, (TK, TQ), 1)
    sc_ref[i] = jnp.where(key_pos <= qry_pos, scores_t(i), -jnp.inf)

    t1 = i * TQ + 1 + lax.broadcasted_iota(jnp.int32, (1, TQ), 1)
    k_eff = jnp.minimum(t1, min(TOPK_MAX, S // 4)).astype(f32)
    _select_bias_t(sc_ref, bias_ref, i + 1, k_eff, t1.astype(f32))

    q_all = jnp.concatenate([_half_masked(slab(q_ref, p), p % 2) for p in range(N_ATT_HEADS)], axis=0)
    acc_ref[...] = jnp.zeros(acc_ref.shape, f32)

    def att_chunk(c_kv, c_bias, ms, ls):
        bias = bias_ref[c_bias]
        s_all = _dot_nt(kb_ref[0, chunk(c_kv), :], q_all)
        ms_new, ls_new, alphas, pts = [], [], [], []
        for p in range(N_ATT_HEADS):
            s = blk(s_all, p) + bias
            m_new = jnp.maximum(ms[p], jnp.max(s, axis=0, keepdims=True))
            alpha = jnp.exp2(ms[p] - m_new)
            pt = jnp.exp2(s - m_new)
            ls_new.append(alpha * ls[p] + jnp.sum(pt, axis=0, keepdims=True))
            ms_new.append(m_new)
            alphas.append(alpha)
            pts.append(pt.astype(bf16))
        pv_all = _dot(vt_ref[0, c_kv], jnp.concatenate(pts, axis=1))
        for p in range(N_ATT_HEADS):
            rows = slice(p * HEAD_DIM, (p + 1) * HEAD_DIM)
            g0 = (p % 2) * HEAD_DIM
            acc_ref[rows, :] = alphas[p] * acc_ref[rows, :] + blk(pv_all, p)[g0:g0 + HEAD_DIM, :]
        return tuple(ms_new), tuple(ls_new)

    def att_body(j, carry):
        return att_chunk(2 * j + 1, 2 * j + 1, *att_chunk(2 * j, 2 * j, *carry))

    init = (tuple(jnp.full((1, TQ), NEG, f32) for _ in range(N_ATT_HEADS)),
            tuple(jnp.zeros((1, TQ), f32) for _ in range(N_ATT_HEADS)))
    carry = lax.fori_loop(0, lax.shift_right_logical(i + 1, 1), att_body, init)
    _, ls = lax.cond((i + 1) % 2 == 1, lambda ms, ls: att_chunk(i, i, ms, ls), lambda ms, ls: (ms, ls), *carry)
    for p in range(N_ATT_HEADS):
        rows = slice(p * HEAD_DIM, (p + 1) * HEAD_DIM)
        acc_ref[rows, :] = acc_ref[rows, :] / ls[p]
    att_g = (acc_ref[...].T * ga_ref[0]).astype(bf16)
    y_ref[0] = _merge(att_g, cvg_ref[0], mg_ref[0], x_ref[0], wo_ref, gp_ref[...])


N_PROMPT_IN = 12


def _attn_prompt_kernel(*refs):
    _attn_prompt_body(*refs)


def _attn_fused_kernel(pt_ref, *refs):
    prompt_in, refs = refs[:N_PROMPT_IN], refs[N_PROMPT_IN:]
    qbd_ref, sbias_ref, knew_ref, vnew_ref, qmbd_ref, cmk_ref, cmv_ref, k_hbm, v_hbm = refs[:9]
    y_ref, o_ref, mo_ref, sc_ref, bias_ref, acc_ref, kbuf, vbuf, semk, semv = refs[9:]
    s = pl.program_id(0) * pl.num_programs(1) + pl.program_id(1)
    ns = pl.num_programs(0) * pl.num_programs(1)
    _sample_prefetch(pt_ref, s, ns, k_hbm, v_hbm, kbuf, vbuf, semk, semv)
    _attn_prompt_body(*prompt_in, y_ref, sc_ref, bias_ref, acc_ref)
    _sample_attend(pt_ref, s, qbd_ref, sbias_ref, knew_ref, vnew_ref, qmbd_ref, cmk_ref, cmv_ref, k_hbm, v_hbm,
                   o_ref, mo_ref, kbuf, vbuf, semk, semv)


def _attn_prompt(q, qi, wit, kb, vt, kk, ga, cvg, mg, x, wo, gp, TQ, sample=None):
    B, S, _ = x.shape
    NC = S // TQ
    imap = lambda f: (lambda b, i, *_: f(b, i))
    tok = lambda d: pl.BlockSpec((1, TQ, d), imap(lambda b, i: (b, i, 0)))
    seq = lambda d: pl.BlockSpec((1, S, d), imap(lambda b, i: (b, 0, 0)))
    full = lambda r, c: pl.BlockSpec((r, c), imap(lambda b, i: (0, 0)))
    in_specs = [tok(D_ATT), tok(D_QIDX), pl.BlockSpec((1, N_IDX_HEADS, TQ), imap(lambda b, i: (b, 0, i))),
                seq(D_KV), pl.BlockSpec((1, NC, D_KV, TQ), imap(lambda b, i: (b, 0, 0, 0))), seq(LANES),
                tok(D_ATT), tok(D_CONV), tok(D_MEM), tok(D_MODEL), full(D_MODEL, D_MODEL), full(1, D_MODEL)]
    assert len(in_specs) == N_PROMPT_IN
    scratch = [pltpu.VMEM((NC, TQ, TQ), f32), pltpu.VMEM((NC, TQ, TQ), f32), pltpu.VMEM((D_ATT, TQ), f32)]
    params = pltpu.CompilerParams(dimension_semantics=("arbitrary", "arbitrary"), vmem_limit_bytes=VMEM_LIMIT)
    y_shape = jax.ShapeDtypeStruct((B, S, D_MODEL), f32)
    if sample is None:
        return pl.pallas_call(
            _attn_prompt_kernel, grid=(B, NC), in_specs=in_specs, out_specs=tok(D_MODEL), out_shape=y_shape,
            scratch_shapes=scratch, compiler_params=params, name="attn_prompt",
        )(q, qi, wit, kb, vt, kk, ga, cvg, mg, x, wo, gp)
    pt, *sample_ops = sample
    s_in, s_out, s_shapes, s_scratch = _sattn_specs(pt, *sample_ops, index=lambda b, i: b * NC + i)
    return pl.pallas_call(
        _attn_fused_kernel,
        grid_spec=pltpu.PrefetchScalarGridSpec(
            num_scalar_prefetch=1, grid=(B, NC), in_specs=in_specs + s_in, out_specs=[tok(D_MODEL)] + s_out,
            scratch_shapes=scratch + s_scratch),
        out_shape=[y_shape] + s_shapes, compiler_params=params, name="attn_prompt",
    )(pt, q, qi, wit, kb, vt, kk, ga, cvg, mg, x, wo, gp, *sample_ops)


def _paged_copies(pt_ref, b, hbm_ref, buf, sem, slot, n_pages, page):
    return [pltpu.make_async_copy(hbm_ref.at[pt_ref[b, p]], buf.at[slot, :, pl.ds(p * page, page)], sem.at[slot])
            for p in range(n_pages)]


def _sidx_kernel(pt_ref, qi_ref, wi_ref, knew_ref, kidx_hbm, sc_ref, dn_ref, buf, sem):
    b = pl.program_id(0)
    nb = pl.num_programs(0)
    n_pages = pt_ref.shape[1]
    page = kidx_hbm.shape[2]
    copies = lambda bb, slot: _paged_copies(pt_ref, bb, kidx_hbm, buf, sem, slot, n_pages, page)
    slot = lax.rem(b, 2)

    @pl.when(b == 0)
    def _():
        for p, cp in enumerate(copies(0, 0)):
            cp.start(priority=p % 2)

    @pl.when(b + 1 < nb)
    def _():
        for p, cp in enumerate(copies(b + 1, 1 - slot)):
            cp.start(priority=p % 2)

    for cp in copies(b, slot):
        cp.wait()

    qi = qi_ref[0]
    wi = wi_ref[0]
    weigh = lambda d: jnp.sum(jnp.maximum(d, 0.0) * wi, axis=0, keepdims=True)
    sub = 1024
    for j in range(n_pages * page // sub):
        sc_ref[0, :, j * sub:(j + 1) * sub] = weigh(_dot(qi, buf[slot, :, j * sub:(j + 1) * sub].astype(bf16)))
    dn_ref[0] = weigh(_dot_nt(qi, knew_ref[...]))


def _sidx(pt, qi8, wi8, knew, kidx_t_pool):
    nb, n_pages = pt.shape
    page = kidx_t_pool.shape[2]
    past = n_pages * page
    return pl.pallas_call(
        _sidx_kernel,
        grid_spec=pltpu.PrefetchScalarGridSpec(
            num_scalar_prefetch=1,
            grid=(nb,),
            in_specs=[pl.BlockSpec((1, N_IDX_HEADS, IDX_DIM), lambda b, pt: (b, 0, 0)),
                      pl.BlockSpec((1, N_IDX_HEADS, 1), lambda b, pt: (b, 0, 0)),
                      pl.BlockSpec((nb, IDX_DIM), lambda b, pt: (0, 0)),
                      pl.BlockSpec(memory_space=pl.ANY)],
            out_specs=[pl.BlockSpec((1, 1, past), lambda b, pt: (b, 0, 0)),
                       pl.BlockSpec((1, 1, nb), lambda b, pt: (b, 0, 0))],
            scratch_shapes=[pltpu.VMEM((2, IDX_DIM, past), f32), pltpu.SemaphoreType.DMA((2,))]),
        out_shape=[jax.ShapeDtypeStruct((nb, 1, past), f32), jax.ShapeDtypeStruct((nb, 1, nb), f32)],
        compiler_params=pltpu.CompilerParams(dimension_semantics=("arbitrary",), vmem_limit_bytes=VMEM_LIMIT),
        name="sample_indexer",
    )(pt, qi8, wi8, knew, kidx_t_pool)


def _ssel_kernel(sct_ref, dn_ref, out_ref, sc_ref, bias_ref):
    nc, TK, Q = sc_ref.shape
    for c in range(nc - 1):
        sc_ref[c] = sct_ref[c * TK:(c + 1) * TK, :]
    row = lax.broadcasted_iota(jnp.int32, (TK, Q), 0)
    col = lax.broadcasted_iota(jnp.int32, (TK, Q), 1)
    sc_ref[nc - 1] = jnp.where(row == col, dn_ref[...], -jnp.inf)
    n_valid = (nc - 1) * TK + 1
    k_eff = jnp.full((1, Q), float(min(TOPK_MAX, n_valid // 4)), f32)
    _select_bias_t(sc_ref, bias_ref, nc, k_eff, jnp.full((1, Q), float(n_valid), f32))
    for c in range(nc):
        out_ref[c * TK:(c + 1) * TK, :] = bias_ref[c]


def _ssel(sc_t, dn_t):
    past, nb = sc_t.shape
    nc = past // nb + 1
    return pl.pallas_call(
        _ssel_kernel,
        out_shape=jax.ShapeDtypeStruct((past + nb, nb), f32),
        scratch_shapes=[pltpu.VMEM((nc, nb, nb), f32), pltpu.VMEM((nc, nb, nb), f32)],
        compiler_params=pltpu.CompilerParams(vmem_limit_bytes=VMEM_LIMIT),
        name="sample_select",
    )(sc_t, dn_t)


def _sample_copies(pt_ref, s, slot, k_hbm, v_hbm, kbuf, vbuf, semk, semv):
    n_pages, page = pt_ref.shape[1], k_hbm.shape[2]
    return (_paged_copies(pt_ref, s, k_hbm, kbuf, semk, slot, n_pages, page)
            + _paged_copies(pt_ref, s, v_hbm, vbuf, semv, slot, n_pages, page))


def _sample_prefetch(pt_ref, s, ns, k_hbm, v_hbm, kbuf, vbuf, semk, semv):
    slot = lax.rem(s, 2)

    @pl.when(s == 0)
    def _():
        for cp in _sample_copies(pt_ref, 0, 0, k_hbm, v_hbm, kbuf, vbuf, semk, semv):
            cp.start()

    @pl.when(s + 1 < ns)
    def _():
        for cp in _sample_copies(pt_ref, s + 1, 1 - slot, k_hbm, v_hbm, kbuf, vbuf, semk, semv):
            cp.start()


def _sample_attend(pt_ref, s, qbd_ref, bias_ref, knew_ref, vnew_ref, qmbd_ref, cmk_ref, cmv_ref, k_hbm, v_hbm,
                   o_ref, mo_ref, kbuf, vbuf, semk, semv):
    past = kbuf.shape[2]
    nb = knew_ref.shape[0]
    slot = lax.rem(s, 2)

    qm = qmbd_ref[0]
    pm, lm = _softmax_parts(_dot(qm, cmk_ref[0].astype(bf16)))
    mo = _dot_nt(pm.astype(bf16), cmv_ref[0].astype(bf16)) / lm
    for h in range(N_MEM_HEADS):
        hs = slice(h * HEAD_DIM, (h + 1) * HEAD_DIM)
        mo_ref[0, :, hs] = mo[h:h + 1, hs]

    for cp in _sample_copies(pt_ref, s, slot, k_hbm, v_hbm, kbuf, vbuf, semk, semv):
        cp.wait()

    qbd = qbd_ref[0]

    def softmax_part(logits, value_product):
        m_j = jnp.max(logits, axis=1, keepdims=True)
        p_j = jnp.exp2(logits - m_j)
        return m_j, jnp.sum(p_j, axis=1, keepdims=True), value_product(p_j.astype(bf16))

    seg = min(SAMPLE_SEG, past)
    parts = []
    for j in range(past // seg):
        ks = slice(j * seg, (j + 1) * seg)
        parts.append(softmax_part(_dot(qbd, kbuf[slot, :, ks].astype(bf16)) + bias_ref[0, :, ks],
                                  lambda p_j, ks=ks: _dot_nt(p_j, vbuf[slot, :, ks].astype(bf16))))
    parts.append(softmax_part(_dot_nt(qbd, knew_ref[...]) + bias_ref[0, :, past:past + nb],
                              lambda p_j: _dot(p_j, vnew_ref[...])))
    m = functools.reduce(jnp.maximum, [m_j for m_j, _, _ in parts])
    scales = [jnp.exp2(m_j - m) for m_j, _, _ in parts]
    l = sum(l_j * c for (_, l_j, _), c in zip(parts, scales))
    o = sum(o_j * c for (_, _, o_j), c in zip(parts, scales)) / l
    for p in range(N_ATT_HEADS):
        g0 = (p % 2) * HEAD_DIM
        o_ref[0, :, p * HEAD_DIM:(p + 1) * HEAD_DIM] = o[p:p + 1, g0:g0 + HEAD_DIM]


def _sattn_kernel(pt_ref, qbd_ref, bias_ref, knew_ref, vnew_ref, qmbd_ref, cmk_ref, cmv_ref, k_hbm, v_hbm,
                  o_ref, mo_ref, kbuf, vbuf, semk, semv):
    s = pl.program_id(0)
    _sample_prefetch(pt_ref, s, pl.num_programs(0), k_hbm, v_hbm, kbuf, vbuf, semk, semv)
    _sample_attend(pt_ref, s, qbd_ref, bias_ref, knew_ref, vnew_ref, qmbd_ref, cmk_ref, cmv_ref, k_hbm, v_hbm,
                   o_ref, mo_ref, kbuf, vbuf, semk, semv)


def _sattn_specs(pt, qbd, bias, knew, vnew, qmbd, cmk_t, cmv_t, k_t_pool, v_t_pool, index):
    nb, n_pages = pt.shape
    past = n_pages * k_t_pool.shape[2]
    M = cmk_t.shape[2]
    per = lambda r, c: pl.BlockSpec((1, r, c), lambda *g: (index(*g[:-1]), 0, 0))
    full = lambda r, c: pl.BlockSpec((r, c), lambda *g: (0, 0))
    in_specs = [per(N_ATT_HEADS, D_KV), per(1, past + nb), full(nb, D_KV), full(nb, D_KV),
                per(N_MEM_HEADS, D_MEM), per(D_MEM, M), per(D_MEM, M),
                pl.BlockSpec(memory_space=pl.ANY), pl.BlockSpec(memory_space=pl.ANY)]
    out_specs = [per(1, D_ATT), per(1, D_MEM)]
    out_shapes = [jax.ShapeDtypeStruct((nb, 1, D_ATT), f32), jax.ShapeDtypeStruct((nb, 1, D_MEM), f32)]
    scratch = [pltpu.VMEM((2, D_KV, past), f32), pltpu.VMEM((2, D_KV, past), f32),
               pltpu.SemaphoreType.DMA((2,)), pltpu.SemaphoreType.DMA((2,))]
    return in_specs, out_specs, out_shapes, scratch


def _sattn(pt, *ops):
    in_specs, out_specs, out_shapes, scratch = _sattn_specs(pt, *ops, index=lambda b: b)
    return pl.pallas_call(
        _sattn_kernel,
        grid_spec=pltpu.PrefetchScalarGridSpec(num_scalar_prefetch=1, grid=(pt.shape[0],), in_specs=in_specs,
                                               out_specs=out_specs, scratch_shapes=scratch),
        out_shape=out_shapes,
        compiler_params=pltpu.CompilerParams(dimension_semantics=("arbitrary",), vmem_limit_bytes=VMEM_LIMIT),
        name="sample_attention",
    )(pt, *ops)


def _smerge_kernel(att_ref, ga_ref, cvg_ref, mo_ref, gm_ref, x_ref, wo_ref, gp_ref, y_ref):
    att_g = (att_ref[...] * ga_ref[...]).astype(bf16)
    mg = (mo_ref[...] * gm_ref[...]).astype(bf16)
    y_ref[...] = _merge(att_g, cvg_ref[...], mg, x_ref[...], wo_ref, gp_ref[...])


def _smerge(att, ga, cvg, mo, gm, x, wo, gp):
    return pl.pallas_call(
        _smerge_kernel,
        out_shape=jax.ShapeDtypeStruct(x.shape, f32),
        compiler_params=pltpu.CompilerParams(vmem_limit_bytes=VMEM_LIMIT),
        name="sample_merge",
    )(att, ga, cvg, mo, gm, x, wo, gp)


def _rope_tables(pos):
    half = HEAD_DIM // 2
    inv = ROPE_THETA ** (-jnp.arange(half, dtype=f32) / half)
    ang = pos.astype(f32)[:, None] * inv[None, :]
    cos, sin = jnp.cos(ang), jnp.sin(ang)
    zero = jnp.zeros_like(sin)
    rep = LANES // HEAD_DIM
    return (jnp.tile(jnp.concatenate([cos, cos], axis=1), (1, rep)),
            jnp.tile(jnp.concatenate([-sin, zero], axis=1), (1, rep)),
            jnp.tile(jnp.concatenate([zero, sin], axis=1), (1, rep)))


def _pack_w_in(w):
    head = lambda base: jnp.concatenate([w[:, base + h * HEAD_DIM:base + (h + 1) * HEAD_DIM] for h in ATT_ORDER], 1)
    kidx = w[:, R_KIDX:R_WIDX]
    packed = jnp.concatenate([head(R_Q), w[:, R_K:R_GATT], head(R_GATT), w[:, R_QIDX:R_KIDX], kidx, kidx,
                              w[:, R_U:]], axis=1)
    w_t = jnp.concatenate([w[:, R_V:R_GATT], w[:, R_WIDX:R_U], jnp.zeros((D_MODEL, D_T - D_KV - N_IDX_HEADS), w.dtype)],
                          axis=1).T
    return packed.astype(bf16), w_t.astype(bf16)


def _pack_w_out(w):
    att = jnp.concatenate([w[h * HEAD_DIM:(h + 1) * HEAD_DIM] for h in ATT_ORDER], axis=0)
    return jnp.concatenate([att, w[D_ATT:]], axis=0).astype(bf16)


def _slab_rows(x, n_rows):
    T = x.shape[0]
    slabs = jnp.repeat(x.reshape(T, n_rows // 2, 1, 2, HEAD_DIM), 2, axis=2)
    keep = jnp.arange(2)[:, None] == jnp.arange(2)[None, :]
    return jnp.where(keep[None, None, :, :, None], slabs, jnp.zeros((), x.dtype)).reshape(T, n_rows, LANES)


def _diag_rows(x, n_heads):
    T = x.shape[0]
    keep = jnp.arange(n_heads)[:, None] == jnp.arange(n_heads)[None, :]
    xh = x.reshape(T, 1, n_heads, HEAD_DIM)
    return jnp.where(keep[None, :, :, None], xh, jnp.zeros((), x.dtype)).reshape(T, n_heads, n_heads * HEAD_DIM)


def _layer(l, xp, xs, mem_prompt, cache_k, cache_v, cache_kidx, cache_mem_k, cache_mem_v, state_conv,
           page_table, g_pre, w_in, conv_w, g_memnorm, w_mem_kv, w_out, g_post):
    B, S, _ = xp.shape
    nb, T, _ = xs.shape
    assert T == 1
    n_pages = page_table.shape[1]
    n_pool, page = cache_k.shape[1], cache_k.shape[2]
    past = n_pages * page
    M = mem_prompt.shape[1]
    assert page == nb and past % 1024 == 0 and past % min(SAMPLE_SEG, past) == 0

    row = lambda a: a[l][None, :]
    w_pack, w_t = _pack_w_in(w_in[l])
    wo = _pack_w_out(w_out[l])
    gp = row(g_post)

    mk, mv, mkb, mvb = _memkv(mem_prompt, row(g_memnorm), w_mem_kv[l].astype(bf16))
    cos, sa, sb = _rope_tables(jnp.arange(S, dtype=jnp.int32))
    TQ = min(256, S)
    TM = min(1024, S)
    (q, k, v, kb, vt, ga, qi, ki, kk, wit, cvg, cst, mg) = _front_prompt(
        xp, row(g_pre), w_pack, w_t, cos, sa, sb, conv_w[l], mkb, mvb, TM, TQ)

    cos, sa, sb = _rope_tables(jnp.full((nb,), past, jnp.int32))
    st = state_conv[l]
    (q_s, k_s, v_s, kb_s, vt_s, ga_s, qi_s, ki_s, kk_s, wit_s, cvg_s, u_s, qm_s, gm_s) = _front_sample(
        xs.reshape(1, nb, D_MODEL), row(g_pre), w_pack, w_t, cos, sa, sb, conv_w[l], st[:, 0, :], st[:, 1, :])
    kidx_t = jnp.swapaxes(cache_kidx[l], 1, 2)
    k_t = jnp.transpose(cache_k[l], (0, 2, 3, 1)).reshape(n_pool, D_KV, page)
    v_t = jnp.transpose(cache_v[l], (0, 2, 3, 1)).reshape(n_pool, D_KV, page)
    cmk_t = jnp.transpose(cache_mem_k[l], (0, 2, 3, 1)).reshape(nb, D_MEM, M)
    cmv_t = jnp.transpose(cache_mem_v[l], (0, 2, 3, 1)).reshape(nb, D_MEM, M)

    sc, dn = _sidx(page_table, qi_s.reshape(nb, N_IDX_HEADS, IDX_DIM), wit_s[0].T.reshape(nb, N_IDX_HEADS, 1),
                   kk_s[0, :, :IDX_DIM], kidx_t)
    bias_t = _ssel(sc.reshape(nb, past).T, dn.reshape(nb, nb).T)
    sample_ops = (page_table, _slab_rows(q_s[0], N_ATT_HEADS), bias_t.T.reshape(nb, 1, past + nb),
                  kb_s[0], vt_s[0, 0].T, _diag_rows(qm_s[0], N_MEM_HEADS), cmk_t, cmv_t, k_t, v_t)

    prompt_ops = (q, qi, wit, kb, vt, kk, ga, cvg, mg, xp, wo, gp, TQ)
    if nb == B * (S // TQ):
        yp, att_s, mo_s = _attn_prompt(*prompt_ops, sample=sample_ops)
    else:
        yp = _attn_prompt(*prompt_ops)
        att_s, mo_s = _sattn(*sample_ops)
    ys = _smerge(att_s.reshape(nb, D_ATT), ga_s[0], cvg_s[0], mo_s.reshape(nb, D_MEM), gm_s[0],
                 xs.reshape(nb, D_MODEL), wo, gp)

    outs_p = (k.reshape(B, S, N_KV_HEADS, HEAD_DIM), v.reshape(B, S, N_KV_HEADS, HEAD_DIM), ki, cst,
              mk.reshape(B, M, N_MEM_HEADS, HEAD_DIM), mv.reshape(B, M, N_MEM_HEADS, HEAD_DIM))
    outs_s = (k_s.reshape(nb, 1, N_KV_HEADS, HEAD_DIM), v_s.reshape(nb, 1, N_KV_HEADS, HEAD_DIM),
              ki_s.reshape(nb, 1, IDX_DIM), jnp.stack([st[:, 1, :], u_s[0]], axis=1))
    return yp, ys.reshape(nb, 1, D_MODEL), outs_p, outs_s


def kernel(x_prompt, x_sample, mem_prompt, cache_k, cache_v, cache_kidx, cache_mem_k, cache_mem_v, state_conv,
           page_table, g_pre, w_in, conv_w, g_memnorm, w_mem_kv, w_out, g_post):
    depth = w_in.shape[0]
    xp, xs = x_prompt, x_sample
    per_p, per_s = [], []
    for l in range(depth):
        xp, xs, op, os_ = _layer(l, xp, xs, mem_prompt, cache_k, cache_v, cache_kidx, cache_mem_k, cache_mem_v,
                                 state_conv, page_table, g_pre, w_in, conv_w, g_memnorm, w_mem_kv, w_out, g_post)
        per_p.append(op)
        per_s.append(os_)
    stack = lambda outs, i: jnp.stack([o[i] for o in outs])
    return ((xp, xs) + tuple(stack(per_p, i) for i in range(6)) + tuple(stack(per_s, i) for i in range(4)))
```
